```python
import math
import jax, jax.numpy as jnp
from jax import lax
import numpy as np

D_MODEL = 2048
BATCH = 16
SEQ = 2048
DEPTH = 1
DEC_BATCH = 128
DEC_SEQ = 1
PAST_LEN = 16384
PAGE_SIZE = 128

SSM_EXPAND = 2
D_INNER = SSM_EXPAND * D_MODEL
SSM_HEAD_DIM = 64
SSM_HEADS = D_INNER // SSM_HEAD_DIM
SSM_GROUPS = 8
SSM_STATE = 128
CONV_WIDTH = 4
CONV_DIM = D_INNER + 2 * SSM_GROUPS * SSM_STATE
SSD_CHUNK = 128
ATTN_HEADS = 32
KV_HEADS = 8
HEAD_DIM = D_MODEL // ATTN_HEADS
ATTN_WIDTH = ATTN_HEADS * HEAD_DIM
KV_WIDTH = KV_HEADS * HEAD_DIM
WINDOW = 128
ATTN_BLOCK = 128
ROPE_DIM = HEAD_DIM // 4
ROPE_THETA = 500000.0
N_EXPERTS = 32
TOP_K = 4
D_FF = D_MODEL
SWIGLU_LIMIT = 7.0
SWIGLU_ALPHA = 1.702
EXPERT_BLOCK = 128
N_BRANCHES = 2
N_MOD = 6
EPS = 1e-6
Z_END = D_INNER
XBC_END = Z_END + CONV_DIM
DT_END = XBC_END + SSM_HEADS
Q_END = DT_END + ATTN_WIDTH
K_END = Q_END + KV_WIDTH
V_END = K_END + KV_WIDTH
IN_DIM = V_END + N_BRANCHES * D_MODEL

kernel_name = 'hybrid_ssd_swa_sink_moe_adaln_step'

F32 = jnp.float32


def rms_norm(x, g):
    xf = x.astype(F32)
    y = xf * lax.rsqrt(jnp.mean(xf * xf, axis=-1, keepdims=True) + EPS)
    return (y * g.astype(F32)).astype(x.dtype)


def modulate(h, shift, scale):
    return h * (1 + scale) + shift


def gated_group_rms_norm(y, z, g):
    b, l, _ = y.shape
    u = (y.astype(F32) * jax.nn.silu(z.astype(F32))).reshape(b, l, SSM_GROUPS, -1)
    u = u * lax.rsqrt(jnp.mean(u * u, axis=-1, keepdims=True) + EPS)
    return (u.reshape(b, l, D_INNER) * g.astype(F32)).astype(y.dtype)


def causal_depthwise_conv(u, prev, w, b):
    l = u.shape[1]
    full = jnp.concatenate([prev.astype(u.dtype), u], axis=1)
    out = b + sum(full[:, j:j + l] * w[j] for j in range(CONV_WIDTH))
    return jax.nn.silu(out), full[:, l:]


def ssd_chunked_scan(x, dt, a, bm, cm, state0):
    bsz, l = x.shape[:2]
    q = min(SSD_CHUNK, l)
    nc = -(-l // q)
    pad = nc * q - l
    r = SSM_HEADS // SSM_GROUPS

    def chunks(t):
        t = jnp.pad(t.astype(F32), [(0, 0), (0, pad)] + [(0, 0)] * (t.ndim - 2))
        return jnp.moveaxis(t.reshape((bsz, nc, q) + t.shape[2:]), 1, 0)

    xdt = chunks((x.astype(F32) * dt[..., None]).reshape(bsz, l, SSM_GROUPS, r, SSM_HEAD_DIM))
    da = chunks((dt * a).reshape(bsz, l, SSM_GROUPS, r))
    bc = chunks(bm)
    cc = chunks(cm)
    causal = jnp.tril(jnp.ones((q, q), bool))[None, :, :, None, None]

    def step(state, inp):
        xdt_c, da_c, b_c, c_c = inp
        a_cs = jnp.cumsum(da_c, axis=1)
        decay = jnp.exp(jnp.where(causal, a_cs[:, :, None] - a_cs[:, None, :], -jnp.inf))
        cb = jnp.einsum('btgn,bsgn->btsg', c_c, b_c)
        y = jnp.einsum('btsg,btsgr,bsgrp->btgrp', cb, decay, xdt_c)
        y = y + jnp.einsum('btgn,bgrpn->btgrp', c_c, state) * jnp.exp(a_cs)[..., None]
        to_end = jnp.exp(a_cs[:, -1:] - a_cs)
        state = (state * jnp.exp(a_cs[:, -1])[..., None, None]
                 + jnp.einsum('bsgn,bsgr,bsgrp->bgrpn', b_c, to_end, xdt_c))
        return state, y

    s0 = state0.astype(F32).reshape(bsz, SSM_GROUPS, r, SSM_HEAD_DIM, SSM_STATE)
    s_last, ys = lax.scan(step, s0, (xdt, da, bc, cc))
    y = jnp.moveaxis(ys, 0, 1).reshape(bsz, nc * q, SSM_HEADS, SSM_HEAD_DIM)[:, :l]
    return y, s_last.reshape(bsz, SSM_HEADS, SSM_HEAD_DIM, SSM_STATE)


def rope_partial(x, pos):
    half = ROPE_DIM // 2
    inv_freq = ROPE_THETA ** (-jnp.arange(half, dtype=F32) * 2.0 / ROPE_DIM)
    ang = pos.astype(F32)[:, None] * inv_freq[None, :]
    cos = jnp.cos(ang)[None, :, None, :]
    sin = jnp.sin(ang)[None, :, None, :]
    xf = x.astype(F32)
    x1, x2, rest = xf[..., :half], xf[..., half:ROPE_DIM], xf[..., ROPE_DIM:]
    return jnp.concatenate([x1 * cos - x2 * sin, x2 * cos + x1 * sin, rest], axis=-1).astype(x.dtype)


def sliding_window_sink_attention(q, k, v, k_prev, v_prev, pos0, sinks):
    bsz, l = q.shape[:2]
    grp = ATTN_HEADS // KV_HEADS
    qb_len = min(ATTN_BLOCK, l)
    nb = -(-l // qb_len)
    lp = nb * qb_len
    pad = [(0, 0), (0, lp - l), (0, 0), (0, 0)]
    q_blocks = jnp.moveaxis(jnp.pad(q, pad).reshape(bsz, nb, qb_len, KV_HEADS, grp, HEAD_DIM), 1, 0)
    keys = jnp.concatenate([k_prev.astype(k.dtype), jnp.pad(k, pad)], axis=1)
    vals = jnp.concatenate([v_prev.astype(v.dtype), jnp.pad(v, pad)], axis=1)
    key_pos = pos0 - WINDOW + jnp.arange(WINDOW + lp)
    sink = sinks.astype(F32).reshape(1, KV_HEADS, grp, 1, 1)
    scale = HEAD_DIM ** -0.5
    span = qb_len + WINDOW

    def block(args):
        q_blk, i = args
        start = i * qb_len
        ks = lax.dynamic_slice_in_dim(keys, start, span, axis=1)
        vs = lax.dynamic_slice_in_dim(vals, start, span, axis=1)
        kp = lax.dynamic_slice_in_dim(key_pos, start, span)
        qp = pos0 + start + jnp.arange(qb_len)
        rel = qp[:, None] - kp[None, :]
        ok = (rel >= 0) & (rel < WINDOW) & (kp >= 0)[None, :]
        s = jnp.einsum('bqkgd,bskd->bkgqs', q_blk, ks, preferred_element_type=F32) * scale
        s = jnp.where(ok, s, -jnp.inf)
        m = jnp.maximum(jnp.max(s, axis=-1, keepdims=True), sink)
        p = jnp.exp(s - m)
        p = p / (jnp.sum(p, axis=-1, keepdims=True) + jnp.exp(sink - m))
        return jnp.einsum('bkgqs,bskd->bqkgd', p.astype(vals.dtype), vs)

    out = lax.map(block, (q_blocks, jnp.arange(nb)))
    return jnp.moveaxis(out, 0, 1).reshape(bsz, lp, ATTN_WIDTH)[:, :l]


def moe_clamped_swiglu(h, w_router, b_router, w1, b1, w2, b2):
    m_tok, d = h.shape
    logits = jnp.dot(h, w_router, preferred_element_type=F32) + b_router.astype(F32)
    top_val, top_idx = lax.top_k(logits, TOP_K)
    gate = jax.nn.softmax(top_val, axis=-1)
    n_assign = m_tok * TOP_K
    e_flat = top_idx.reshape(-1)
    tok_flat = jnp.arange(n_assign, dtype=jnp.int32) // TOP_K
    order = jnp.argsort(e_flat)
    e_sorted = e_flat[order]
    counts = jnp.bincount(e_flat, length=N_EXPERTS)
    padded = (counts + EXPERT_BLOCK - 1) // EXPERT_BLOCK * EXPERT_BLOCK
    pad_end = jnp.cumsum(padded)
    pad_start = pad_end - padded
    grp_start = jnp.cumsum(counts) - counts
    dest = pad_start[e_sorted] + jnp.arange(n_assign) - grp_start[e_sorted]
    n_blocks = -(-n_assign // EXPERT_BLOCK) + N_EXPERTS
    n_slots = n_blocks * EXPERT_BLOCK
    slot_tok = jnp.full((n_slots,), m_tok, jnp.int32).at[dest].set(tok_flat[order])
    slot_gate = jnp.zeros((n_slots,), F32).at[dest].set(gate.reshape(-1)[order])
    h_ext = jnp.concatenate([h, jnp.zeros((1, d), h.dtype)], axis=0)
    x_blocks = h_ext[slot_tok].reshape(n_blocks, EXPERT_BLOCK, d)
    block_expert = jnp.minimum(
        jnp.searchsorted(pad_end, jnp.arange(n_blocks) * EXPERT_BLOCK, side='right'), N_EXPERTS - 1)

    def expert_block(args):
        xb, e = args
        gu = xb @ w1[e] + b1[e]
        x_glu = jnp.minimum(gu[:, :D_FF], SWIGLU_LIMIT)
        x_lin = jnp.clip(gu[:, D_FF:], -SWIGLU_LIMIT, SWIGLU_LIMIT)
        act = x_glu * jax.nn.sigmoid(SWIGLU_ALPHA * x_glu) * (x_lin + 1)
        return act @ w2[e] + b2[e]

    y_slots = lax.map(expert_block, (x_blocks, block_expert)).reshape(n_slots, d)
    y = jax.ops.segment_sum(y_slots * slot_gate[:, None].astype(y_slots.dtype), slot_tok,
                            num_segments=m_tok + 1)
    return y[:m_tok]


def hybrid_layer(x, c, pos0, ssm0, conv0, k_prev, v_prev, lw):
    bsz, l, d = x.shape
    mod = (jax.nn.silu(c) @ lw['w_ada'] + lw['b_ada']).reshape(bsz, 1, N_MOD, d)
    shift1, scale1, gate1, shift2, scale2, gate2 = (mod[:, :, i] for i in range(N_MOD))
    h = modulate(rms_norm(x, lw['g_norm1']), shift1, scale1)
    proj = h @ lw['w_in']
    z, xbc, dt_raw, q, k, v, gate_logits = jnp.split(
        proj, [Z_END, XBC_END, DT_END, Q_END, K_END, V_END], axis=-1)
    xbc, conv_new = causal_depthwise_conv(xbc, conv0, lw['w_conv'], lw['b_conv'])
    xs, bm, cm = jnp.split(xbc, [D_INNER, D_INNER + SSM_GROUPS * SSM_STATE], axis=-1)
    xs = xs.reshape(bsz, l, SSM_HEADS, SSM_HEAD_DIM)
    dt = jax.nn.softplus(dt_raw.astype(F32) + lw['dt_bias'].astype(F32))
    a = -jnp.exp(lw['a_log'].astype(F32))
    y_ssd, ssm_new = ssd_chunked_scan(xs, dt, a, bm.reshape(bsz, l, SSM_GROUPS, SSM_STATE),
                                      cm.reshape(bsz, l, SSM_GROUPS, SSM_STATE), ssm0)
    y_ssd = (y_ssd + lw['d_skip'].astype(F32)[:, None] * xs.astype(F32)).astype(x.dtype)
    y_ssd = gated_group_rms_norm(y_ssd.reshape(bsz, l, D_INNER), z, lw['g_ssm_norm'])
    pos = pos0 + jnp.arange(l)
    q = rope_partial(q.reshape(bsz, l, ATTN_HEADS, HEAD_DIM), pos)
    k = rope_partial(k.reshape(bsz, l, KV_HEADS, HEAD_DIM), pos)
    v = v.reshape(bsz, l, KV_HEADS, HEAD_DIM)
    y_att = sliding_window_sink_attention(q, k, v, k_prev, v_prev, pos0, lw['sinks'])
    k_new = jnp.concatenate([k_prev.astype(k.dtype), k], axis=1)[:, -WINDOW:]
    v_new = jnp.concatenate([v_prev.astype(v.dtype), v], axis=1)[:, -WINDOW:]
    gates = jax.nn.sigmoid(gate_logits.astype(F32)).astype(x.dtype)
    mixed = (gates[..., :d] * (y_ssd @ lw['w_br_ssd'])
             + gates[..., d:] * (y_att @ lw['w_br_att'])) @ lw['w_out']
    x = x + gate1 * mixed
    h2 = modulate(rms_norm(x, lw['g_norm2']), shift2, scale2)
    ffn = moe_clamped_swiglu(h2.reshape(bsz * l, d), lw['w_router'], lw['b_router'], lw['w_expert_in'],
                             lw['b_expert_in'], lw['w_expert_out'], lw['b_expert_out'])
    x = x + gate2 * ffn.reshape(bsz, l, d)
    return x, ssm_new.astype(ssm0.dtype), conv_new, k_new, v_new


def setup_inputs(seed: int = 0) -> dict:
    key = jax.random.key(seed)
    ks = jax.random.split(key, 40)
    D = D_MODEL

    def nrm(k, shape, scale):
        return jax.random.normal(k, shape, F32) * scale

    dt0 = jnp.exp(jax.random.uniform(ks[12], (DEPTH, SSM_HEADS), F32,
                                     minval=math.log(1e-3), maxval=math.log(1e-1)))
    return {
        'x_prompt': nrm(ks[0], (BATCH, SEQ, D), 1.0),
        'x_sample': nrm(ks[1], (DEC_BATCH, DEC_SEQ, D), 1.0),
        'state_ssm': nrm(ks[2], (DEPTH, DEC_BATCH, SSM_HEADS, SSM_HEAD_DIM, SSM_STATE), 0.1),
        'state_conv': nrm(ks[3], (DEPTH, DEC_BATCH, CONV_WIDTH - 1, CONV_DIM), 1.0),
        'cache_win_k': nrm(ks[4], (DEPTH, DEC_BATCH, WINDOW, KV_HEADS, HEAD_DIM), 1.0),
        'cache_win_v': nrm(ks[5], (DEPTH, DEC_BATCH, WINDOW, KV_HEADS, HEAD_DIM), 1.0),
        'c_prompt': nrm(ks[6], (BATCH, D), 1.0),
        'c_sample': nrm(ks[7], (DEC_BATCH, D), 1.0),
        'w_ada': nrm(ks[8], (DEPTH, D, N_MOD * D), D ** -0.5),
        'b_ada': nrm(ks[9], (DEPTH, N_MOD * D), 0.02),
        'g_norm1': 1.0 + nrm(ks[10], (DEPTH, D), 0.02),
        'w_in': nrm(ks[11], (DEPTH, D, IN_DIM), D ** -0.5),
        'w_conv': nrm(ks[13], (DEPTH, CONV_WIDTH, CONV_DIM), CONV_WIDTH ** -0.5),
        'b_conv': nrm(ks[14], (DEPTH, CONV_DIM), 0.02),
        'dt_bias': dt0 + jnp.log(-jnp.expm1(-dt0)),
        'a_log': jnp.log(jax.random.uniform(ks[15], (DEPTH, SSM_HEADS), F32, minval=1.0, maxval=16.0)),
        'd_skip': 1.0 + nrm(ks[16], (DEPTH, SSM_HEADS), 0.1),
        'g_ssm_norm': 1.0 + nrm(ks[17], (DEPTH, D_INNER), 0.02),
        'sinks': nrm(ks[18], (DEPTH, ATTN_HEADS), 1.0),
        'w_br_ssd': nrm(ks[19], (DEPTH, D_INNER, D), D_INNER ** -0.5),
        'w_br_att': nrm(ks[20], (DEPTH, ATTN_WIDTH, D), ATTN_WIDTH ** -0.5),
        'w_out': nrm(ks[21], (DEPTH, D, D), D ** -0.5),
        'g_norm2': 1.0 + nrm(ks[22], (DEPTH, D), 0.02),
        'w_router': nrm(ks[23], (DEPTH, D, N_EXPERTS), D ** -0.5),
        'b_router': nrm(ks[24], (DEPTH, N_EXPERTS), 0.01),
        'w_expert_in': nrm(ks[25], (DEPTH, N_EXPERTS, D, 2 * D_FF), D ** -0.5),
        'b_expert_in': nrm(ks[26], (DEPTH, N_EXPERTS, 2 * D_FF), 0.02),
        'w_expert_out': nrm(ks[27], (DEPTH, N_EXPERTS, D_FF, D), D_FF ** -0.5),
        'b_expert_out': nrm(ks[28], (DEPTH, N_EXPERTS, D), 0.02),
        'g_final': 1.0 + nrm(ks[29], (D,), 0.02),
    }


def reference(x_prompt, x_sample, state_ssm, state_conv, cache_win_k, cache_win_v, c_prompt, c_sample,
              w_ada, b_ada, g_norm1, w_in, w_conv, b_conv, dt_bias, a_log, d_skip, g_ssm_norm, sinks,
              w_br_ssd, w_br_att, w_out, g_norm2, w_router, b_router, w_expert_in, b_expert_in,
              w_expert_out, b_expert_out, g_final):
    xp, xs = x_prompt, x_sample
    bp = xp.shape[0]
    ssm_p, conv_p, wk_p, wv_p = [], [], [], []
    ssm_s, conv_s, wk_s, wv_s = [], [], [], []
    for li in range(DEPTH):
        lw = dict(w_ada=w_ada[li], b_ada=b_ada[li], g_norm1=g_norm1[li], w_in=w_in[li],
                  w_conv=w_conv[li], b_conv=b_conv[li], dt_bias=dt_bias[li], a_log=a_log[li],
                  d_skip=d_skip[li], g_ssm_norm=g_ssm_norm[li], sinks=sinks[li],
                  w_br_ssd=w_br_ssd[li], w_br_att=w_br_att[li], w_out=w_out[li], g_norm2=g_norm2[li],
                  w_router=w_router[li], b_router=b_router[li], w_expert_in=w_expert_in[li],
                  b_expert_in=b_expert_in[li], w_expert_out=w_expert_out[li],
                  b_expert_out=b_expert_out[li])
        zero_ssm = jnp.zeros((bp, SSM_HEADS, SSM_HEAD_DIM, SSM_STATE), xp.dtype)
        zero_conv = jnp.zeros((bp, CONV_WIDTH - 1, CONV_DIM), xp.dtype)
        zero_win = jnp.zeros((bp, WINDOW, KV_HEADS, HEAD_DIM), xp.dtype)
        xp, s1, c1, k1, v1 = hybrid_layer(xp, c_prompt, 0, zero_ssm, zero_conv, zero_win, zero_win, lw)
        xs, s2, c2, k2, v2 = hybrid_layer(xs, c_sample, PAST_LEN, state_ssm[li], state_conv[li],
                                          cache_win_k[li], cache_win_v[li], lw)
        ssm_p.append(s1); conv_p.append(c1); wk_p.append(k1); wv_p.append(v1)
        ssm_s.append(s2); conv_s.append(c2); wk_s.append(k2); wv_s.append(v2)
    y_prompt = rms_norm(xp, g_final)
    y_sample = rms_norm(xs, g_final)
    return (y_prompt, y_sample,
            jnp.stack(ssm_p), jnp.stack(conv_p), jnp.stack(wk_p), jnp.stack(wv_p),
            jnp.stack(ssm_s), jnp.stack(conv_s), jnp.stack(wk_s), jnp.stack(wv_s))
```

```python
import functools
import math

import jax
import jax.numpy as jnp
from jax import lax
from jax.experimental import pallas as pl
from jax.experimental.pallas import tpu as pltpu

F32 = jnp.float32
BF16 = jnp.bfloat16
I32 = jnp.int32
HIGHEST = lax.Precision.HIGHEST

D_MODEL = 2048
PAST_LEN = 16384
D_INNER = 2 * D_MODEL
SSM_HEAD_DIM = 64
SSM_HEADS = D_INNER // SSM_HEAD_DIM
SSM_GROUPS = 8
HEADS_PER_GROUP = SSM_HEADS // SSM_GROUPS
GROUP_WIDTH = HEADS_PER_GROUP * SSM_HEAD_DIM
SSM_STATE = 128
CONV_WIDTH = 4
BC_WIDTH = SSM_GROUPS * SSM_STATE
CONV_DIM = D_INNER + 2 * BC_WIDTH
CHUNK = 128
ATTN_HEADS = 32
KV_HEADS = 8
HEAD_DIM = 64
KV_WIDTH = KV_HEADS * HEAD_DIM
WINDOW = 128
ROPE_DIM = HEAD_DIM // 4
ROPE_THETA = 500000.0
N_EXPERTS = 32
TOP_K = 4
D_FF = D_MODEL
SWIGLU_LIMIT = 7.0
SWIGLU_ALPHA = 1.702
N_MOD = 6
EPS = 1e-6
Z_END = D_INNER
XBC_END = Z_END + CONV_DIM
DT_END = XBC_END + SSM_HEADS
Q_END = DT_END + D_MODEL
K_END = Q_END + KV_WIDTH
V_END = K_END + KV_WIDTH

LANES = 128
VMEM_LIMIT = 56 * 1024 * 1024

PC_Z = 0
PC_X = D_INNER
PC_B = PC_X + D_INNER
PC_C = PC_B + BC_WIDTH
PC_Q = PC_C + BC_WIDTH
PC_K = PC_Q + D_MODEL
PC_V = PC_K + KV_WIDTH
PC_G = PC_V + KV_WIDTH
PC_DT = PC_G + 2 * D_MODEL
DT_PAD = 512
PROJ_W = PC_DT + DT_PAD
IN_TN = 512

MOE_BLK = 512
MOE_TF = 512


def _cparams(sem):
    return pltpu.CompilerParams(dimension_semantics=sem, vmem_limit_bytes=VMEM_LIMIT)


def _silu(x):
    return x * jax.nn.sigmoid(x)


def _ada_kernel(c_ref, w_ref, b_ref, o_ref):
    s = _silu(c_ref[...]).astype(BF16)
    o_ref[...] = jnp.dot(s, w_ref[...].astype(BF16), preferred_element_type=F32) + b_ref[...]


def _ada(c, w_ada, b_ada):
    n, d = c.shape
    nout = w_ada.shape[1]
    tn = 1024
    return pl.pallas_call(
        _ada_kernel,
        out_shape=jax.ShapeDtypeStruct((n, nout), F32),
        grid=(nout // tn,),
        in_specs=[pl.BlockSpec((n, d), lambda j: (0, 0)),
                  pl.BlockSpec((d, tn), lambda j: (0, j)),
                  pl.BlockSpec((1, tn), lambda j: (0, j))],
        out_specs=pl.BlockSpec((n, tn), lambda j: (0, j)),
        compiler_params=_cparams(("arbitrary",)),
        name="ada_mod",
    )(c, w_ada, b_ada.reshape(1, nout))


def _inproj_kernel(x_ref, sh_ref, sc_ref, g_ref, w_ref, o_ref, h_ref):
    @pl.when(pl.program_id(1) == 0)
    def _():
        x = x_ref[...]
        y = x * lax.rsqrt(jnp.mean(x * x, axis=-1, keepdims=True) + EPS) * g_ref[...]
        h_ref[...] = (y * (1.0 + sc_ref[0]) + sh_ref[0]).astype(BF16)

    o_ref[...] = jnp.dot(h_ref[...], w_ref[...], preferred_element_type=F32)


def _mod_spec(tm, seq_len):
    if seq_len == 1:
        return pl.BlockSpec((1, tm, D_MODEL), lambda i, *_: (0, i, 0))
    return pl.BlockSpec((1, 1, D_MODEL), lambda i, *_: ((i * tm) // seq_len, 0, 0))


def _row_tile(cap, m, seq_len):
    tm = min(cap, m if seq_len == 1 else seq_len)
    assert m % tm == 0 and (seq_len == 1 or seq_len % tm == 0)
    return tm


def _mod_arr(m, seq_len):
    return m[None] if seq_len == 1 else m[:, None, :]


def _inproj(x, shift, scale, g, w_in_r, seq_len):
    m = x.shape[0]
    tm = _row_tile(1024, m, seq_len)
    return pl.pallas_call(
        _inproj_kernel,
        out_shape=jax.ShapeDtypeStruct((m, PROJ_W), F32),
        grid=(m // tm, PROJ_W // IN_TN),
        in_specs=[pl.BlockSpec((tm, D_MODEL), lambda i, j: (i, 0)),
                  _mod_spec(tm, seq_len), _mod_spec(tm, seq_len),
                  pl.BlockSpec((1, D_MODEL), lambda i, j: (0, 0)),
                  pl.BlockSpec((D_MODEL, IN_TN), lambda i, j: (0, j))],
        out_specs=pl.BlockSpec((tm, IN_TN), lambda i, j: (i, j)),
        scratch_shapes=[pltpu.VMEM((tm, D_MODEL), BF16)],
        compiler_params=_cparams(("parallel", "arbitrary")),
        name="in_proj",
    )(x, _mod_arr(shift, seq_len), _mod_arr(scale, seq_len), g.reshape(1, D_MODEL), w_in_r)


def _softplus(x):
    return jnp.maximum(x, 0.0) + jnp.log1p(jnp.exp(-jnp.abs(x)))


def _ssd_kernel(z_ref, x_ref, b_ref, c_ref, dt_ref,
                cw_ref, cb_ref, dtb_ref, alog_ref, dskip_ref, gn_ref, exp_ref, sel_ref, tri_ref,
                y_ref, st_ref, conv_ref,
                ext_s, xs_s, xdt_s, acsx_s, acst_s, bm_s, cm_s, state_s):
    ci = pl.program_id(1)
    last_chunk = ci == pl.num_programs(1) - 1

    @pl.when(ci == 0)
    def _():
        ext_s[0:8, :] = jnp.zeros((8, CONV_DIM), F32)
        state_s[...] = jnp.zeros_like(state_s)

    ext_s[8:8 + CHUNK, 0:D_INNER] = x_ref[...]
    ext_s[8:8 + CHUNK, D_INNER:D_INNER + BC_WIDTH] = b_ref[...]
    ext_s[8:8 + CHUNK, D_INNER + BC_WIDTH:CONV_DIM] = c_ref[...]

    def conv(lo, hi):
        acc = cb_ref[:, lo:hi] + cw_ref[3:4, lo:hi] * ext_s[8:8 + CHUNK, lo:hi]
        for j in range(CONV_WIDTH - 1):
            acc = acc + cw_ref[j:j + 1, lo:hi] * ext_s[5 + j:5 + j + CHUNK, lo:hi]
        return _silu(acc)

    xs_s[...] = conv(0, D_INNER)
    bm_s[...] = conv(D_INNER, D_INNER + BC_WIDTH)
    cm_s[...] = conv(D_INNER + BC_WIDTH, CONV_DIM)
    tail = ext_s[5 + CHUNK:8 + CHUNK, :]
    conv_ref[0] = tail
    ext_s[5:8, :] = tail

    dt = _softplus(dt_ref[:, 0:LANES] + dtb_ref[...])
    da = dt * (-jnp.exp(alog_ref[...]))
    acs = jnp.dot(tri_ref[...], da, precision=HIGHEST, preferred_element_type=F32)
    acst_s[...] = acs.T
    acsx_s[...] = jnp.dot(acs, exp_ref[...], precision=HIGHEST, preferred_element_type=F32)
    xdt_s[...] = xs_s[...] * jnp.dot(dt, exp_ref[...], precision=HIGHEST, preferred_element_type=F32)

    row = lax.broadcasted_iota(I32, (CHUNK, CHUNK), 0)
    col = lax.broadcasted_iota(I32, (CHUNK, CHUNK), 1)
    causal = row >= col
    low_half = col < SSM_HEAD_DIM

    def group(g, carry):
        o512 = pl.multiple_of(g * GROUP_WIDTH, GROUP_WIDTH)
        o128 = pl.multiple_of(g * SSM_STATE, SSM_STATE)
        bg = bm_s[:, pl.ds(o128, SSM_STATE)]
        cg16 = cm_s[:, pl.ds(o128, SSM_STATE)].astype(BF16)
        bg16 = bg.astype(BF16)
        cb = lax.dot_general(cg16, bg16, (((1,), (1,)), ((), ())), preferred_element_type=F32)
        sel = sel_ref[:, pl.ds(pl.multiple_of(g * HEADS_PER_GROUP * CHUNK, HEADS_PER_GROUP * CHUNK),
                               HEADS_PER_GROUP * CHUNK)]
        colb = jnp.dot(acs, sel, precision=HIGHEST, preferred_element_type=F32)
        rows = acst_s[pl.ds(pl.multiple_of(g * HEADS_PER_GROUP, HEADS_PER_GROUP), HEADS_PER_GROUP), :]
        xdt_g = xdt_s[:, pl.ds(o512, GROUP_WIDTH)]
        parts = []
        for jp in range(HEADS_PER_GROUP // 2):
            ms = []
            for j in (2 * jp, 2 * jp + 1):
                diff = colb[:, j * CHUNK:(j + 1) * CHUNK] - rows[j:j + 1, :]
                dec = jnp.exp(jnp.where(causal, diff, -jnp.inf))
                ms.append((cb * dec).astype(BF16))
            xp = xdt_g[:, jp * LANES:(jp + 1) * LANES]
            rhs = jnp.concatenate([jnp.where(low_half, xp, 0.0), jnp.where(low_half, 0.0, xp)], axis=0)
            parts.append(jnp.dot(jnp.concatenate(ms, axis=1), rhs.astype(BF16), preferred_element_type=F32))
        y = jnp.concatenate(parts, axis=1)

        acsx_g = acsx_s[:, pl.ds(o512, GROUP_WIDTH)]
        st = state_s[:, pl.ds(o512, GROUP_WIDTH)]
        y = y + jnp.dot(cg16, st.astype(BF16), preferred_element_type=F32) * jnp.exp(acsx_g)
        last = acsx_g[CHUNK - 1:CHUNK, :]
        xw = (xdt_g * jnp.exp(last - acsx_g)).astype(BF16)
        st_new = st * jnp.exp(last) + jnp.dot(bg.T.astype(BF16), xw, preferred_element_type=F32)
        state_s[:, pl.ds(o512, GROUP_WIDTH)] = st_new

        @pl.when(last_chunk)
        def _():
            st_ref[0, pl.ds(o512, GROUP_WIDTH), :] = st_new.T

        y = y + dskip_ref[:, pl.ds(o512, GROUP_WIDTH)] * xs_s[:, pl.ds(o512, GROUP_WIDTH)]
        u = y * _silu(z_ref[:, pl.ds(o512, GROUP_WIDTH)])
        u = u * lax.rsqrt(jnp.mean(u * u, axis=-1, keepdims=True) + EPS)
        y_ref[:, pl.ds(o512, GROUP_WIDTH)] = (u * gn_ref[:, pl.ds(o512, GROUP_WIDTH)]).astype(BF16)
        return carry

    lax.fori_loop(0, SSM_GROUPS, group, 0)


def _pad_heads(v):
    return jnp.pad(v.astype(F32), (0, LANES - SSM_HEADS)).reshape(1, LANES)


def _head_expand():
    h = jnp.arange(LANES)[:, None]
    ch = jnp.arange(D_INNER)[None, :] // SSM_HEAD_DIM
    return (h == ch).astype(F32)


def _head_select():
    h = jnp.arange(LANES)[:, None]
    blk = jnp.arange(SSM_HEADS * CHUNK)[None, :] // CHUNK
    return (h == blk).astype(F32)


def _ssd_prompt(proj, bsz, seq_len, lw):
    nc = seq_len // CHUNK
    m = bsz * seq_len
    tri = (jnp.arange(CHUNK)[:, None] >= jnp.arange(CHUNK)[None, :]).astype(F32)
    const = lambda shape: pl.BlockSpec(shape, lambda b, c: (0, 0))
    rowblk = lambda width, col: pl.BlockSpec((CHUNK, width), lambda b, c: (b * nc + c, col // width))
    y, st, conv = pl.pallas_call(
        _ssd_kernel,
        out_shape=(jax.ShapeDtypeStruct((m, D_INNER), BF16),
                   jax.ShapeDtypeStruct((bsz, D_INNER, SSM_STATE), F32),
                   jax.ShapeDtypeStruct((bsz, CONV_WIDTH - 1, CONV_DIM), F32)),
        grid=(bsz, nc),
        in_specs=[rowblk(D_INNER, PC_Z), rowblk(D_INNER, PC_X), rowblk(BC_WIDTH, PC_B), rowblk(BC_WIDTH, PC_C),
                  rowblk(DT_PAD, PC_DT),
                  const((CONV_WIDTH, CONV_DIM)), const((1, CONV_DIM)), const((1, LANES)), const((1, LANES)),
                  const((1, D_INNER)), const((1, D_INNER)), const((LANES, D_INNER)),
                  const((LANES, SSM_HEADS * CHUNK)), const((CHUNK, CHUNK))],
        out_specs=(pl.BlockSpec((CHUNK, D_INNER), lambda b, c: (b * nc + c, 0)),
                   pl.BlockSpec((1, D_INNER, SSM_STATE), lambda b, c: (b, 0, 0)),
                   pl.BlockSpec((1, CONV_WIDTH - 1, CONV_DIM), lambda b, c: (b, 0, 0))),
        scratch_shapes=[pltpu.VMEM((8 + CHUNK, CONV_DIM), F32),
                        pltpu.VMEM((CHUNK, D_INNER), F32), pltpu.VMEM((CHUNK, D_INNER), F32),
                        pltpu.VMEM((CHUNK, D_INNER), F32), pltpu.VMEM((LANES, CHUNK), F32),
                        pltpu.VMEM((CHUNK, BC_WIDTH), F32), pltpu.VMEM((CHUNK, BC_WIDTH), F32),
                        pltpu.VMEM((SSM_STATE, D_INNER), F32)],
        compiler_params=_cparams(("parallel", "arbitrary")),
        name="ssd_chunks",
    )(proj, proj, proj, proj, proj,
      lw['w_conv'], lw['b_conv'].reshape(1, CONV_DIM), _pad_heads(lw['dt_bias']), _pad_heads(lw['a_log']),
      jnp.repeat(lw['d_skip'].astype(F32), SSM_HEAD_DIM).reshape(1, D_INNER),
      lw['g_ssm_norm'].reshape(1, D_INNER), _head_expand(), _head_select(), tri)
    return y, st.reshape(bsz, SSM_HEADS, SSM_HEAD_DIM, SSM_STATE), conv


def _ssd_step_pre_kernel(x_ref, b_ref, c_ref, dt_ref, p0_ref, p1_ref, p2_ref,
                         cw_ref, cb_ref, dtb_ref, alog_ref, exp_ref,
                         xs_ref, xdt_ref, bm_ref, cm_ref, dec_ref, conv_ref):
    def conv(u_ref, lo, hi):
        acc = cb_ref[:, lo:hi] + cw_ref[3:4, lo:hi] * u_ref[...]
        for j, p_ref in enumerate((p0_ref, p1_ref, p2_ref)):
            acc = acc + cw_ref[j:j + 1, lo:hi] * p_ref[:, lo:hi]
        return _silu(acc)

    xs = conv(x_ref, 0, D_INNER)
    xs_ref[...] = xs
    bm_ref[...] = conv(b_ref, D_INNER, D_INNER + BC_WIDTH)
    cm_ref[...] = conv(c_ref, D_INNER + BC_WIDTH, CONV_DIM)
    dt = _softplus(dt_ref[:, 0:LANES] + dtb_ref[...])
    dec_ref[...] = jnp.exp(dt * (-jnp.exp(alog_ref[...])))
    xdt_ref[...] = xs * jnp.dot(dt, exp_ref[...], precision=HIGHEST, preferred_element_type=F32)
    conv_ref[0] = p1_ref[...]
    conv_ref[1] = p2_ref[...]
    conv_ref[2, :, 0:D_INNER] = x_ref[...]
    conv_ref[2, :, D_INNER:D_INNER + BC_WIDTH] = b_ref[...]
    conv_ref[2, :, D_INNER + BC_WIDTH:CONV_DIM] = c_ref[...]


def _ssd_step_kernel(dec_ref, st_ref, xdtt_ref, bm_ref, cmt_ref, xs_ref, z_ref, dskip_ref, gn_ref,
                     sto_ref, y_ref, yt_s, *, tb):
    i = pl.program_id(0)
    nb = xs_ref.shape[0]
    sub = lax.broadcasted_iota(I32, (nb, SSM_STATE), 0)
    lane = lax.broadcasted_iota(I32, (SSM_STATE, nb), 1)

    @pl.when(i == 0)
    def _():
        yt_s[...] = jnp.zeros_like(yt_s)

    for bi in range(tb):
        b = i * tb + bi
        for g in range(SSM_GROUPS):
            r0 = g * GROUP_WIDTH
            b_row = jnp.where(sub == b, bm_ref[:, g * SSM_STATE:(g + 1) * SSM_STATE], 0.0).astype(BF16)
            term = jnp.dot(xdtt_ref[r0:r0 + GROUP_WIDTH, :].astype(BF16), b_row, preferred_element_type=F32)
            news = []
            for j in range(HEADS_PER_GROUP):
                h = g * HEADS_PER_GROUP + j
                rr = r0 + j * SSM_HEAD_DIM
                new = (st_ref[bi, rr:rr + SSM_HEAD_DIM, :] * dec_ref[b, h]
                       + term[j * SSM_HEAD_DIM:(j + 1) * SSM_HEAD_DIM, :])
                sto_ref[bi, rr:rr + SSM_HEAD_DIM, :] = new
                news.append(new)
            new_g = jnp.concatenate(news, axis=0).astype(BF16)
            c_col = jnp.where(lane == b, cmt_ref[g], 0.0).astype(BF16)
            yt_s[r0:r0 + GROUP_WIDTH, :] += jnp.dot(new_g, c_col, preferred_element_type=F32)

    @pl.when(i == pl.num_programs(0) - 1)
    def _():
        for g in range(SSM_GROUPS):
            sl = slice(g * GROUP_WIDTH, (g + 1) * GROUP_WIDTH)
            y = yt_s[sl, :].T + dskip_ref[:, sl] * xs_ref[:, sl]
            u = y * _silu(z_ref[:, sl])
            u = u * lax.rsqrt(jnp.mean(u * u, axis=-1, keepdims=True) + EPS)
            y_ref[:, sl] = (u * gn_ref[:, sl]).astype(BF16)


def _ssd_sample(proj, state_ssm, state_conv, lw):
    nb = proj.shape[0]
    full = lambda shape: pl.BlockSpec(shape, lambda *_: tuple(0 for _ in shape))
    colblk = lambda width, col: pl.BlockSpec((nb, width), lambda *_: (0, col // width))
    prev = [state_conv[:, j, :] for j in range(CONV_WIDTH - 1)]
    tr = min(32, nb)
    rows = lambda width, col=0: pl.BlockSpec((tr, width), lambda i: (i, col // width))
    xs, xdt, bm, cm, dec, conv = pl.pallas_call(
        _ssd_step_pre_kernel,
        out_shape=(jax.ShapeDtypeStruct((nb, D_INNER), F32), jax.ShapeDtypeStruct((nb, D_INNER), F32),
                   jax.ShapeDtypeStruct((nb, BC_WIDTH), F32), jax.ShapeDtypeStruct((nb, BC_WIDTH), F32),
                   jax.ShapeDtypeStruct((nb, LANES), F32),
                   jax.ShapeDtypeStruct((CONV_WIDTH - 1, nb, CONV_DIM), F32)),
        grid=(nb // tr,),
        in_specs=[rows(D_INNER, PC_X), rows(BC_WIDTH, PC_B), rows(BC_WIDTH, PC_C), rows(DT_PAD, PC_DT),
                  rows(CONV_DIM), rows(CONV_DIM), rows(CONV_DIM),
                  full((CONV_WIDTH, CONV_DIM)), full((1, CONV_DIM)), full((1, LANES)), full((1, LANES)),
                  full((LANES, D_INNER))],
        out_specs=(rows(D_INNER), rows(D_INNER), rows(BC_WIDTH), rows(BC_WIDTH), rows(LANES),
                   pl.BlockSpec((CONV_WIDTH - 1, tr, CONV_DIM), lambda i: (0, i, 0))),
        compiler_params=_cparams(("parallel",)),
        name="ssd_step_pre",
    )(proj, proj, proj, proj, *prev,
      lw['w_conv'], lw['b_conv'].reshape(1, CONV_DIM), _pad_heads(lw['dt_bias']), _pad_heads(lw['a_log']),
      _head_expand())

    tb = 2
    cmt = cm.reshape(nb, SSM_GROUPS, SSM_STATE).transpose(1, 2, 0)
    st_new, y = pl.pallas_call(
        functools.partial(_ssd_step_kernel, tb=tb),
        out_shape=(jax.ShapeDtypeStruct((nb, D_INNER, SSM_STATE), F32),
                   jax.ShapeDtypeStruct((nb, D_INNER), BF16)),
        grid=(nb // tb,),
        in_specs=[pl.BlockSpec(memory_space=pltpu.SMEM),
                  pl.BlockSpec((tb, D_INNER, SSM_STATE), lambda i: (i, 0, 0)),
                  full((D_INNER, nb)), full((nb, BC_WIDTH)), full((SSM_GROUPS, SSM_STATE, nb)),
                  full((nb, D_INNER)), colblk(D_INNER, PC_Z), full((1, D_INNER)), full((1, D_INNER))],
        out_specs=(pl.BlockSpec((tb, D_INNER, SSM_STATE), lambda i: (i, 0, 0)),
                   full((nb, D_INNER))),
        scratch_shapes=[pltpu.VMEM((D_INNER, nb), F32)],
        compiler_params=_cparams(("arbitrary",)),
        name="ssd_step",
    )(dec, state_ssm.reshape(nb, D_INNER, SSM_STATE), xdt.T, bm, cmt, xs, proj,
      jnp.repeat(lw['d_skip'].astype(F32), SSM_HEAD_DIM).reshape(1, D_INNER),
      lw['g_ssm_norm'].reshape(1, D_INNER))
    return (y, st_new.reshape(nb, SSM_HEADS, SSM_HEAD_DIM, SSM_STATE), conv.transpose(1, 0, 2))


def _rope_tables(pos):
    half = ROPE_DIM // 2
    inv_freq = ROPE_THETA ** (-jnp.arange(half, dtype=F32) * 2.0 / ROPE_DIM)
    ang = pos.astype(F32)[:, None] * inv_freq[None, :]
    cos, sin = jnp.cos(ang), jnp.sin(ang)
    n = pos.shape[0]
    ones = jnp.ones((n, HEAD_DIM - ROPE_DIM), F32)
    zeros = jnp.zeros((n, HEAD_DIM - ROPE_DIM), F32)
    zh = jnp.zeros((n, half), F32)
    cos_f = jnp.concatenate([cos, cos, ones], axis=1)
    sin_a = jnp.concatenate([zh, sin, zeros], axis=1)
    sin_b = jnp.concatenate([-sin, zh, zeros], axis=1)
    tile = lambda t: jnp.tile(t, (1, LANES // HEAD_DIM))
    return tile(cos_f), tile(sin_a), tile(sin_b)


def _rope_slab(x, cos_f, sin_a, sin_b):
    half = ROPE_DIM // 2
    return x * cos_f + pltpu.roll(x, half, 1) * sin_a + pltpu.roll(x, LANES - half, 1) * sin_b


def _attn_kernel(sink_ref, q_ref, k_ref, v_ref, cos_ref, sin_a_ref, sin_b_ref,
                 y_ref, kn_ref, vn_ref, kprev_s, vprev_s):
    bi = pl.program_id(1)

    @pl.when(bi == 0)
    def _():
        kprev_s[...] = jnp.zeros_like(kprev_s)
        vprev_s[...] = jnp.zeros_like(vprev_s)

    cos_f, sin_a, sin_b = cos_ref[...], sin_a_ref[...], sin_b_ref[...]
    rope = lambda x: _rope_slab(x, cos_f, sin_a, sin_b)
    kr = jnp.concatenate([rope(k_ref[:, s * LANES:(s + 1) * LANES]) for s in range(KV_WIDTH // LANES)], axis=1)
    v = v_ref[...]
    kspan = jnp.concatenate([kprev_s[...], kr], axis=0)
    vspan = jnp.concatenate([vprev_s[...], v], axis=0)
    kn_ref[0] = kr
    vn_ref[0] = v
    kprev_s[...] = kr
    vprev_s[...] = v

    nq = q_ref.shape[0]
    t = lax.broadcasted_iota(I32, (nq, 2 * nq), 0)
    s = lax.broadcasted_iota(I32, (nq, 2 * nq), 1)
    rel = nq + t - s
    ok = (rel >= 0) & (rel < WINDOW) & ((s >= nq) | (bi > 0))
    low = lax.broadcasted_iota(I32, (2 * nq, LANES), 1) < HEAD_DIM
    scale = HEAD_DIM ** -0.5
    grp = ATTN_HEADS // KV_HEADS

    for kh in range(KV_HEADS):
        slab = slice((kh // 2) * LANES, (kh // 2 + 1) * LANES)

        def both_halves(span):
            nat = span[:, slab]
            swp = pltpu.roll(nat, HEAD_DIM, 1)
            lo, hi = (nat, swp) if kh % 2 == 0 else (swp, nat)
            return jnp.concatenate([jnp.where(low, lo, 0.0), jnp.where(low, 0.0, hi)], axis=0).astype(BF16)

        kbd = both_halves(kspan)
        vbd = both_halves(vspan)
        for p in range(grp // 2):
            h0 = kh * grp + 2 * p
            qs = slice(h0 * HEAD_DIM, (h0 + 2) * HEAD_DIM)
            qp = (rope(q_ref[:, qs]) * scale).astype(BF16)
            sc = lax.dot_general(qp, kbd, (((1,), (1,)), ((), ())), preferred_element_type=F32)
            ps = []
            for a in range(2):
                sa = jnp.where(ok, sc[:, a * 2 * nq:(a + 1) * 2 * nq], -jnp.inf)
                sink = sink_ref[h0 + a]
                mx = jnp.maximum(jnp.max(sa, axis=-1, keepdims=True), sink)
                pa = jnp.exp(sa - mx)
                pa = pa / (jnp.sum(pa, axis=-1, keepdims=True) + jnp.exp(sink - mx))
                ps.append(pa.astype(BF16))
            y_ref[:, qs] = jnp.dot(jnp.concatenate(ps, axis=1), vbd, preferred_element_type=F32).astype(BF16)


def _attn_prompt(proj, bsz, seq_len, sinks):
    nq = CHUNK
    nb = seq_len // nq
    m = bsz * seq_len
    cos_f, sin_a, sin_b = _rope_tables(jnp.arange(seq_len))
    rowblk = lambda width, col: pl.BlockSpec((nq, width), lambda b, i: (b * nb + i, col // width))
    tab = pl.BlockSpec((nq, LANES), lambda b, i: (i, 0))
    y, kn, vn = pl.pallas_call(
        _attn_kernel,
        out_shape=(jax.ShapeDtypeStruct((m, D_MODEL), BF16),
                   jax.ShapeDtypeStruct((bsz, WINDOW, KV_WIDTH), F32),
                   jax.ShapeDtypeStruct((bsz, WINDOW, KV_WIDTH), F32)),
        grid=(bsz, nb),
        in_specs=[pl.BlockSpec(memory_space=pltpu.SMEM),
                  rowblk(D_MODEL, PC_Q), rowblk(KV_WIDTH, PC_K), rowblk(KV_WIDTH, PC_V), tab, tab, tab],
        out_specs=(pl.BlockSpec((nq, D_MODEL), lambda b, i: (b * nb + i, 0)),
                   pl.BlockSpec((1, WINDOW, KV_WIDTH), lambda b, i: (b, 0, 0)),
                   pl.BlockSpec((1, WINDOW, KV_WIDTH), lambda b, i: (b, 0, 0))),
        scratch_shapes=[pltpu.VMEM((nq, KV_WIDTH), F32), pltpu.VMEM((nq, KV_WIDTH), F32)],
        compiler_params=_cparams(("parallel", "arbitrary")),
        name="attn_blocks",
    )(sinks.astype(F32), proj, proj, proj, cos_f, sin_a, sin_b)
    shape = (bsz, WINDOW, KV_HEADS, HEAD_DIM)
    return y, kn.reshape(shape), vn.reshape(shape)


def _attn_step_kernel(sink_ref, q_ref, k_ref, v_ref, cos_ref, sin_a_ref, sin_b_ref, fold_ref, foldt_ref,
                      kc_ref, vc_ref, y_ref, ko_ref, vo_ref, *, tb):
    cos_f, sin_a, sin_b = cos_ref[...], sin_a_ref[...], sin_b_ref[...]
    rope = lambda ref, w: jnp.concatenate(
        [_rope_slab(ref[:, s * LANES:(s + 1) * LANES], cos_f, sin_a, sin_b) for s in range(w // LANES)], axis=1)
    qr = rope(q_ref, D_MODEL) * (HEAD_DIM ** -0.5)
    kr = rope(k_ref, KV_WIDTH)
    v = v_ref[...]

    own = (lax.broadcasted_iota(I32, (ATTN_HEADS, D_MODEL), 1) // HEAD_DIM
           == lax.broadcasted_iota(I32, (ATTN_HEADS, D_MODEL), 0))
    qb = jnp.concatenate([jnp.where(own, jnp.broadcast_to(qr[bi:bi + 1], (ATTN_HEADS, D_MODEL)), 0.0)
                          for bi in range(tb)], axis=0).astype(BF16)
    qm = jnp.dot(qb, fold_ref[...], preferred_element_type=F32).astype(BF16)

    row = lax.broadcasted_iota(I32, (WINDOW, KV_WIDTH), 0)
    w = lax.broadcasted_iota(I32, (ATTN_HEADS, WINDOW), 1)
    sink = sink_ref[...]
    bf = lambda x: x.astype(BF16)
    outs = []
    for bi in range(tb):
        qmb = qm[bi * ATTN_HEADS:(bi + 1) * ATTN_HEADS]
        kc, vc = kc_ref[bi], vc_ref[bi]
        kn, vn = kr[bi:bi + 1], v[bi:bi + 1]
        sc = lax.dot_general(qmb, bf(kc), (((1,), (1,)), ((), ())), preferred_element_type=F32)
        sc = jnp.where(w >= 1, sc, -jnp.inf)
        s_new = jnp.sum(qmb.astype(F32) * bf(kn).astype(F32), axis=-1, keepdims=True)
        mx = jnp.maximum(jnp.maximum(jnp.max(sc, axis=-1, keepdims=True), s_new), sink)
        p = jnp.exp(sc - mx)
        p_new = jnp.exp(s_new - mx)
        den = jnp.sum(p, axis=-1, keepdims=True) + p_new + jnp.exp(sink - mx)
        p = p / den
        p_new = p_new / den
        out = jnp.dot(bf(p), bf(vc), preferred_element_type=F32) + bf(p_new).astype(F32) * bf(vn).astype(F32)
        outs.append(bf(out))
        ko_ref[bi] = jnp.where(row == WINDOW - 1, kn, pltpu.roll(kc, WINDOW - 1, 0))
        vo_ref[bi] = jnp.where(row == WINDOW - 1, vn, pltpu.roll(vc, WINDOW - 1, 0))

    of = jnp.dot(jnp.concatenate(outs, axis=0), foldt_ref[...], preferred_element_type=F32)
    of = of.reshape(tb, ATTN_HEADS, D_MODEL)
    y_ref[...] = jnp.sum(jnp.where(own[None], of, 0.0), axis=1).astype(BF16)


def _attn_sample(proj, cache_k, cache_v, sinks):
    nb = proj.shape[0]
    tb = 8
    grp = ATTN_HEADS // KV_HEADS
    cos_f, sin_a, sin_b = _rope_tables(jnp.full((1,), PAST_LEN))
    c = jnp.arange(D_MODEL)[:, None]
    l = jnp.arange(KV_WIDTH)[None, :]
    fold = ((c % HEAD_DIM == l % HEAD_DIM) & ((c // HEAD_DIM) // grp == l // HEAD_DIM)).astype(BF16)
    full = lambda shape: pl.BlockSpec(shape, lambda i: tuple(0 for _ in shape))
    rowblk = lambda width, col: pl.BlockSpec((tb, width), lambda i: (i, col // width))
    win = pl.BlockSpec((tb, WINDOW, KV_WIDTH), lambda i: (i, 0, 0))
    y, kn, vn = pl.pallas_call(
        functools.partial(_attn_step_kernel, tb=tb),
        out_shape=(jax.ShapeDtypeStruct((nb, D_MODEL), BF16),
                   jax.ShapeDtypeStruct((nb, WINDOW, KV_WIDTH), F32),
                   jax.ShapeDtypeStruct((nb, WINDOW, KV_WIDTH), F32)),
        grid=(nb // tb,),
        in_specs=[full((ATTN_HEADS, 1)), rowblk(D_MODEL, PC_Q), rowblk(KV_WIDTH, PC_K), rowblk(KV_WIDTH, PC_V),
                  full((1, LANES)), full((1, LANES)), full((1, LANES)),
                  full((D_MODEL, KV_WIDTH)), full((KV_WIDTH, D_MODEL)), win, win],
        out_specs=(pl.BlockSpec((tb, D_MODEL), lambda i: (i, 0)), win, win),
        compiler_params=_cparams(("parallel",)),
        name="attn_step",
    )(sinks.astype(F32).reshape(ATTN_HEADS, 1), proj, proj, proj, cos_f, sin_a, sin_b, fold, fold.T,
      cache_k.reshape(nb, WINDOW, KV_WIDTH), cache_v.reshape(nb, WINDOW, KV_WIDTH))
    shape = (nb, WINDOW, KV_HEADS, HEAD_DIM)
    return y, kn.reshape(shape), vn.reshape(shape)


def _merge_kernel(ys_ref, ya_ref, ws_ref, wa_ref, gs_ref, ga_ref, o_ref):
    a = jnp.dot(ys_ref[...], ws_ref[...], preferred_element_type=F32)
    b = jnp.dot(ya_ref[...], wa_ref[...], preferred_element_type=F32)
    o_ref[...] = (jax.nn.sigmoid(gs_ref[...]) * a + jax.nn.sigmoid(ga_ref[...]) * b).astype(BF16)


def _merge(y_ssd, y_att, proj, w_ssd, w_att):
    m = y_ssd.shape[0]
    tm = min(512, m)
    tn = 512
    g0 = PC_G // tn
    return pl.pallas_call(
        _merge_kernel,
        out_shape=jax.ShapeDtypeStruct((m, D_MODEL), BF16),
        grid=(m // tm, D_MODEL // tn),
        in_specs=[pl.BlockSpec((tm, D_INNER), lambda i, j: (i, 0)),
                  pl.BlockSpec((tm, D_MODEL), lambda i, j: (i, 0)),
                  pl.BlockSpec((D_INNER, tn), lambda i, j: (0, j)),
                  pl.BlockSpec((D_MODEL, tn), lambda i, j: (0, j)),
                  pl.BlockSpec((tm, tn), lambda i, j: (i, g0 + j)),
                  pl.BlockSpec((tm, tn), lambda i, j: (i, g0 + D_MODEL // tn + j))],
        out_specs=pl.BlockSpec((tm, tn), lambda i, j: (i, j)),
        compiler_params=_cparams(("parallel", "arbitrary")),
        name="branch_merge",
    )(y_ssd, y_att, w_ssd, w_att, proj, proj)


def _outproj_kernel(pre_ref, x_ref, gate_ref, sh_ref, sc_ref, g_ref, w_ref, wr_ref, br_ref,
                    x1_ref, h2_ref, tg_ref, ti_ref):
    mixed = jnp.dot(pre_ref[...], w_ref[...], preferred_element_type=F32)
    x1 = x_ref[...] + gate_ref[0] * mixed
    x1_ref[...] = x1
    y = x1 * lax.rsqrt(jnp.mean(x1 * x1, axis=-1, keepdims=True) + EPS) * g_ref[...]
    h2 = y * (1.0 + sc_ref[0]) + sh_ref[0]
    h2_ref[...] = h2
    logits = jnp.dot(h2, wr_ref[...], precision=HIGHEST, preferred_element_type=F32) + br_ref[...]
    lane = lax.broadcasted_iota(I32, logits.shape, 1)
    lane_f = lane.astype(F32)
    vals = jnp.zeros_like(logits)
    idxs = jnp.zeros_like(logits)
    top = None
    den = jnp.zeros((logits.shape[0], 1), F32)
    for k in range(TOP_K):
        mx = jnp.max(logits, axis=-1, keepdims=True)
        ix = jnp.min(jnp.where(logits == mx, lane_f, float(LANES)), axis=-1, keepdims=True)
        top = mx if top is None else top
        e = jnp.exp(mx - top)
        den = den + e
        vals = jnp.where(lane == k, e, vals)
        idxs = jnp.where(lane == k, ix, idxs)
        logits = jnp.where(lane_f == ix, -jnp.inf, logits)
    tg_ref[...] = vals / den
    ti_ref[...] = idxs.astype(I32)


def _outproj(pre, x, gate1, shift2, scale2, g2, w_out, w_router, b_router, seq_len):
    m = x.shape[0]
    tm = _row_tile(512, m, seq_len)
    wr =jnp.pad(w_router.astype(F32), ((0, 0), (0, LANES - N_EXPERTS)))
    br = jnp.pad(b_router.astype(F32), (0, LANES - N_EXPERTS), constant_values=-jnp.inf).reshape(1, LANES)
    row = lambda width: pl.BlockSpec((tm, width), lambda i: (i, 0))
    const = lambda shape: pl.BlockSpec(shape, lambda i: (0, 0))
    return pl.pallas_call(
        _outproj_kernel,
        out_shape=(jax.ShapeDtypeStruct((m, D_MODEL), F32), jax.ShapeDtypeStruct((m, D_MODEL), F32),
                   jax.ShapeDtypeStruct((m, LANES), F32), jax.ShapeDtypeStruct((m, LANES), I32)),
        grid=(m // tm,),
        in_specs=[row(D_MODEL), row(D_MODEL), _mod_spec(tm, seq_len), _mod_spec(tm, seq_len), _mod_spec(tm, seq_len),
                  const((1, D_MODEL)), const((D_MODEL, D_MODEL)), const((D_MODEL, LANES)), const((1, LANES))],
        out_specs=(row(D_MODEL), row(D_MODEL), row(LANES), row(LANES)),
        compiler_params=_cparams(("parallel",)),
        name="out_proj_router",
    )(pre, x, _mod_arr(gate1, seq_len), _mod_arr(shift2, seq_len), _mod_arr(scale2, seq_len),
      g2.reshape(1, D_MODEL), w_out, wr, br)


def _ffn_kernel(be_ref, na_ref, tok_hbm, h_hbm, w1g_ref, w1l_ref, b1g_ref, b1l_ref, w2_ref, b2_ref,
                o_ref, tok_s, rows_s, xb_s, acc_s, tok_sem, row_sem):
    b = pl.program_id(0)
    f = pl.program_id(1)
    nf = pl.num_programs(1)
    n_active = na_ref[0]
    slot = b % 2

    def tok_copy(blk, sl):
        return pltpu.make_async_copy(tok_hbm.at[blk], tok_s.at[sl], tok_sem.at[sl])

    def row_copy(tok, r, sl):
        return pltpu.make_async_copy(h_hbm.at[pl.ds(tok, 1)], rows_s.at[sl, pl.ds(r, 1)], row_sem.at[sl])

    def start_rows(blk, sl):
        cp = tok_copy(blk, sl)
        cp.start()
        cp.wait()

        def body(r, c):
            row_copy(tok_s[sl, r], r, sl).start()
            return c
        lax.fori_loop(0, MOE_BLK, body, 0)

    def wait_rows(sl):
        def body(r, c):
            row_copy(0, r, sl).wait()
            return c
        lax.fori_loop(0, MOE_BLK, body, 0)

    @pl.when((f == 0) & (b < n_active))
    def _():
        @pl.when(b == 0)
        def _():
            start_rows(0, 0)

        wait_rows(slot)

        @pl.when(b + 1 < n_active)
        def _():
            start_rows(b + 1, 1 - slot)

        xb_s[...] = rows_s[slot].astype(BF16)

    @pl.when(b < n_active)
    def _():
        xb = xb_s[...]
        glu = jnp.dot(xb, w1g_ref[0], preferred_element_type=F32) + b1g_ref[0]
        lin = jnp.dot(xb, w1l_ref[0], preferred_element_type=F32) + b1l_ref[0]
        glu = jnp.minimum(glu, SWIGLU_LIMIT)
        lin = jnp.clip(lin, -SWIGLU_LIMIT, SWIGLU_LIMIT)
        act = glu * jax.nn.sigmoid(SWIGLU_ALPHA * glu) * (lin + 1.0)
        part = jnp.dot(act.astype(BF16), w2_ref[0], preferred_element_type=F32)

        @pl.when(f == 0)
        def _():
            acc_s[...] = part + b2_ref[0]

        @pl.when(f > 0)
        def _():
            acc_s[...] += part

        @pl.when(f == nf - 1)
        def _():
            o_ref[...] = acc_s[...]

    @pl.when((b >= n_active) & (f == nf - 1))
    def _():
        o_ref[...] = jnp.zeros_like(o_ref)


def _ffn(h_all, slot_tok, block_expert, n_active, w1, b1, w2, b2):
    n_blocks = slot_tok.shape[0]
    nf = D_FF // MOE_TF

    def widx(col_off):
        def index_map(b, f, be, na):
            live = b < na[0]
            bb = jnp.where(live, b, na[0] - 1)
            ff = jnp.where(live, f, nf - 1)
            return be[bb], 0, col_off + ff
        return index_map

    def w2idx(b, f, be, na):
        live = b < na[0]
        return be[jnp.where(live, b, na[0] - 1)], jnp.where(live, f, nf - 1), 0

    def b2idx(b, f, be, na):
        return be[jnp.where(b < na[0], b, na[0] - 1)], 0, 0

    grid_spec = pltpu.PrefetchScalarGridSpec(
        num_scalar_prefetch=2,
        grid=(n_blocks, nf),
        in_specs=[pl.BlockSpec(memory_space=pl.ANY), pl.BlockSpec(memory_space=pl.ANY),
                  pl.BlockSpec((1, D_MODEL, MOE_TF), widx(0)), pl.BlockSpec((1, D_MODEL, MOE_TF), widx(nf)),
                  pl.BlockSpec((1, 1, MOE_TF), widx(0)), pl.BlockSpec((1, 1, MOE_TF), widx(nf)),
                  pl.BlockSpec((1, MOE_TF, D_MODEL), w2idx), pl.BlockSpec((1, 1, D_MODEL), b2idx)],
        out_specs=pl.BlockSpec((MOE_BLK, D_MODEL), lambda b, f, be, na: (b, 0)),
        scratch_shapes=[pltpu.SMEM((2, MOE_BLK), I32),
                        pltpu.VMEM((2, MOE_BLK, D_MODEL), F32),
                        pltpu.VMEM((MOE_BLK, D_MODEL), BF16),
                        pltpu.VMEM((MOE_BLK, D_MODEL), F32),
                        pltpu.SemaphoreType.DMA((2,)), pltpu.SemaphoreType.DMA((2,))])
    return pl.pallas_call(
        _ffn_kernel,
        out_shape=jax.ShapeDtypeStruct((n_blocks * MOE_BLK, D_MODEL), F32),
        grid_spec=grid_spec,
        compiler_params=_cparams(("arbitrary", "arbitrary")),
        name="expert_ffn",
    )(block_expert, n_active, slot_tok, h_all, w1, w1, b1[:, None, :], b1[:, None, :], w2, b2[:, None, :])


def _route(top_idx):
    n_tok = top_idx.shape[0]
    n_assign = n_tok * TOP_K
    e_flat = top_idx.reshape(-1)
    onehot = (e_flat[:, None] == jnp.arange(N_EXPERTS)[None, :]).astype(I32)
    csum = jnp.cumsum(onehot, axis=0)
    rank = jnp.sum(onehot * csum, axis=1) - 1
    counts = csum[-1]
    nblk = (counts + MOE_BLK - 1) // MOE_BLK
    blk_end = jnp.cumsum(nblk)
    pad_start = (blk_end - nblk) * MOE_BLK
    dest = jnp.sum(onehot * pad_start[None, :], axis=1) + rank
    n_blocks = -(-n_assign // MOE_BLK) + N_EXPERTS
    slot_tok = jnp.zeros((n_blocks * MOE_BLK,), I32).at[dest].set(
        jnp.arange(n_assign, dtype=I32) // TOP_K, unique_indices=True)
    block_expert = jnp.minimum(jnp.searchsorted(blk_end, jnp.arange(n_blocks), side='right'),
                               N_EXPERTS - 1).astype(I32)
    n_active = blk_end[-1:].astype(I32)
    return dest.reshape(n_tok, TOP_K).astype(I32), slot_tok.reshape(n_blocks, MOE_BLK), block_expert, n_active


def _combine_kernel(dest_hbm, y_hbm, x1_ref, tg_ref, gate_ref, gf_ref, o_ref, dest_s, rows_s, dest_sem, row_sem,
                    *, tc):
    i = pl.program_id(0)
    cp = pltpu.make_async_copy(dest_hbm.at[i], dest_s, dest_sem)
    cp.start()
    cp.wait()

    def row_copy(src, r):
        return pltpu.make_async_copy(y_hbm.at[pl.ds(src, 1)], rows_s.at[pl.ds(r, 1)], row_sem)

    def issue(r, c):
        row_copy(dest_s[r], r).start()
        return c
    lax.fori_loop(0, TOP_K * tc, issue, 0)

    def drain(r, c):
        row_copy(0, r).wait()
        return c
    lax.fori_loop(0, TOP_K * tc, drain, 0)

    tg = tg_ref[...]
    ffn = tg[:, 0:1] * rows_s[0:tc, :]
    for k in range(1, TOP_K):
        ffn = ffn + tg[:, k:k + 1] * rows_s[k * tc:(k + 1) * tc, :]
    x2 = x1_ref[...] + gate_ref[0] * ffn
    o_ref[...] = x2 * lax.rsqrt(jnp.mean(x2 * x2, axis=-1, keepdims=True) + EPS) * gf_ref[...]


def _combine(dest, y_sorted, x1, top_gate, gate2, g_final, seq_len):
    m = x1.shape[0]
    tc = _row_tile(256, m, seq_len)
    dest_t = dest.reshape(m // tc, tc, TOP_K).transpose(0, 2, 1).reshape(m // tc, TOP_K * tc)
    row = lambda width: pl.BlockSpec((tc, width), lambda i: (i, 0))
    return pl.pallas_call(
        functools.partial(_combine_kernel, tc=tc),
        out_shape=jax.ShapeDtypeStruct((m, D_MODEL), F32),
        grid=(m // tc,),
        in_specs=[pl.BlockSpec(memory_space=pl.ANY), pl.BlockSpec(memory_space=pl.ANY),
                  row(D_MODEL), row(LANES), _mod_spec(tc, seq_len), pl.BlockSpec((1, D_MODEL), lambda i: (0, 0))],
        out_specs=row(D_MODEL),
        scratch_shapes=[pltpu.SMEM((TOP_K * tc,), I32), pltpu.VMEM((TOP_K * tc, D_MODEL), F32),
                        pltpu.SemaphoreType.DMA, pltpu.SemaphoreType.DMA],
        compiler_params=_cparams(("arbitrary",)),
        name="moe_combine",
    )(dest_t, y_sorted, x1, top_gate, _mod_arr(gate2, seq_len), g_final.reshape(1, D_MODEL))


def _reorder_w_in(w_in):
    w = w_in.astype(BF16)
    dt_cols = jnp.pad(w[:, XBC_END:DT_END], ((0, 0), (0, DT_PAD - SSM_HEADS)))
    return jnp.concatenate([w[:, :XBC_END], w[:, DT_END:], dt_cols], axis=1)


def _mixer_tail(x, y_ssd, y_att, proj, mod, lw, seq_len):
    pre = _merge(y_ssd, y_att, proj, lw['w_br_ssd'], lw['w_br_att'])
    return _outproj(pre, x, mod[2], mod[3], mod[4], lw['g_norm2'], lw['w_out'], lw['w_router'], lw['b_router'],
                    seq_len)


def _forward(x_prompt, x_sample, state_ssm, state_conv, cache_win_k, cache_win_v, c_prompt, c_sample, lw, g_final):
    bp, lp, d = x_prompt.shape
    bs = x_sample.shape[0]
    mp = bp * lp
    xp = x_prompt.reshape(mp, d)
    xs = x_sample.reshape(bs, d)

    mod = _ada(jnp.concatenate([c_prompt, c_sample], axis=0), lw['w_ada'], lw['b_ada'])
    mod_p = [mod[:bp, i * d:(i + 1) * d] for i in range(N_MOD)]
    mod_s = [mod[bp:, i * d:(i + 1) * d] for i in range(N_MOD)]

    proj_p = _inproj(xp, mod_p[0], mod_p[1], lw['g_norm1'], lw['w_in'], lp)
    proj_s = _inproj(xs, mod_s[0], mod_s[1], lw['g_norm1'], lw['w_in'], 1)

    yssd_p, ssm_p, conv_p = _ssd_prompt(proj_p, bp, lp, lw)
    yatt_p, wk_p, wv_p = _attn_prompt(proj_p, bp, lp, lw['sinks'])
    yssd_s, ssm_s, conv_s = _ssd_sample(proj_s, state_ssm, state_conv, lw)
    yatt_s, wk_s, wv_s = _attn_sample(proj_s, cache_win_k, cache_win_v, lw['sinks'])

    x1_p, h2_p, tg_p, ti_p = _mixer_tail(xp, yssd_p, yatt_p, proj_p, mod_p, lw, lp)
    x1_s, h2_s, tg_s, ti_s = _mixer_tail(xs, yssd_s, yatt_s, proj_s, mod_s, lw, 1)

    h_all = jnp.concatenate([h2_p, h2_s], axis=0)
    top_idx = jnp.concatenate([ti_p[:, :TOP_K], ti_s[:, :TOP_K]], axis=0)
    dest, slot_tok, block_expert, n_active = _route(top_idx)
    y_sorted = _ffn(h_all, slot_tok, block_expert, n_active,
                    lw['w_expert_in'], lw['b_expert_in'], lw['w_expert_out'], lw['b_expert_out'])

    y_p = _combine(dest[:mp], y_sorted, x1_p, tg_p, mod_p[5], g_final, lp)
    y_s = _combine(dest[mp:], y_sorted, x1_s, tg_s, mod_s[5], g_final, 1)
    return (y_p.reshape(bp, lp, d), y_s.reshape(bs, 1, d),
            ssm_p[None], conv_p[None], wk_p[None], wv_p[None],
            ssm_s[None], conv_s[None], wk_s[None], wv_s[None])


def kernel(x_prompt, x_sample, state_ssm, state_conv, cache_win_k, cache_win_v, c_prompt, c_sample, w_ada, b_ada, g_norm1, w_in, w_conv, b_conv, dt_bias, a_log, d_skip, g_ssm_norm, sinks, w_br_ssd, w_br_att, w_out, g_norm2, w_router, b_router, w_expert_in, b_expert_in, w_expert_out, b_expert_out, g_final):
    assert w_ada.shape[0] == 1, "single-layer stack"
    lw = dict(w_ada=w_ada[0], b_ada=b_ada[0], g_norm1=g_norm1[0], w_in=_reorder_w_in(w_in[0]),
              w_conv=w_conv[0], b_conv=b_conv[0], dt_bias=dt_bias[0], a_log=a_log[0], d_skip=d_skip[0],
              g_ssm_norm=g_ssm_norm[0], sinks=sinks[0],
              w_br_ssd=w_br_ssd[0].astype(BF16), w_br_att=w_br_att[0].astype(BF16), w_out=w_out[0].astype(BF16),
              g_norm2=g_norm2[0], w_router=w_router[0], b_router=b_router[0],
              w_expert_in=w_expert_in[0].astype(BF16), b_expert_in=b_expert_in[0],
              w_expert_out=w_expert_out[0].astype(BF16), b_expert_out=b_expert_out[0])
    return _forward(x_prompt, x_sample, state_ssm[0], state_conv[0], cache_win_k[0], cache_win_v[0],
                    c_prompt, c_sample, lw, g_final)
```

```python
import functools
import math

import jax
import jax.numpy as jnp
from jax import lax
from jax.experimental import pallas as pl
from jax.experimental.pallas import tpu as pltpu

F32 = jnp.float32
BF16 = jnp.bfloat16
I32 = jnp.int32
HIGHEST = lax.Precision.HIGHEST

D_MODEL = 2048
PAST_LEN = 16384
D_INNER = 2 * D_MODEL
SSM_HEAD_DIM = 64
SSM_HEADS = D_INNER // SSM_HEAD_DIM
SSM_GROUPS = 8
HEADS_PER_GROUP = SSM_HEADS // SSM_GROUPS
GROUP_WIDTH = HEADS_PER_GROUP * SSM_HEAD_DIM
SSM_STATE = 128
CONV_WIDTH = 4
BC_WIDTH = SSM_GROUPS * SSM_STATE
CONV_DIM = D_INNER + 2 * BC_WIDTH
CHUNK = 128
ATTN_HEADS = 32
KV_HEADS = 8
HEAD_DIM = 64
KV_WIDTH = KV_HEADS * HEAD_DIM
WINDOW = 128
ROPE_DIM = HEAD_DIM // 4
ROPE_THETA = 500000.0
N_EXPERTS = 32
TOP_K = 4
D_FF = D_MODEL
SWIGLU_LIMIT = 7.0
SWIGLU_ALPHA = 1.702
N_MOD = 6
EPS = 1e-6
Z_END = D_INNER
XBC_END = Z_END + CONV_DIM
DT_END = XBC_END + SSM_HEADS
Q_END = DT_END + D_MODEL
K_END = Q_END + KV_WIDTH
V_END = K_END + KV_WIDTH

LANES = 128
VMEM_LIMIT = 56 * 1024 * 1024

PC_Z = 0
PC_X = D_INNER
PC_B = PC_X + D_INNER
PC_C = PC_B + BC_WIDTH
PC_Q = PC_C + BC_WIDTH
PC_K = PC_Q + D_MODEL
PC_V = PC_K + KV_WIDTH
PC_G = PC_V + KV_WIDTH
PC_DT = PC_G + 2 * D_MODEL
DT_PAD = 512
PROJ_W = PC_DT + DT_PAD
IN_TN = 1280

MOE_BLK = 512
MOE_TF = 1024


def _cparams(sem):
    return pltpu.CompilerParams(dimension_semantics=sem, vmem_limit_bytes=VMEM_LIMIT)


def _silu(x):
    return x * jax.nn.sigmoid(x)


def _ada_kernel(c_ref, w_ref, b_ref, o_ref):
    s = _silu(c_ref[...]).astype(BF16)
    o_ref[...] = jnp.dot(s, w_ref[...].astype(BF16), preferred_element_type=F32) + b_ref[...]


def _ada(c, w_ada, b_ada):
    n, d = c.shape
    nout = w_ada.shape[1]
    tn = 1024
    return pl.pallas_call(
        _ada_kernel,
        out_shape=jax.ShapeDtypeStruct((n, nout), F32),
        grid=(nout // tn,),
        in_specs=[pl.BlockSpec((n, d), lambda j: (0, 0)),
                  pl.BlockSpec((d, tn), lambda j: (0, j)),
                  pl.BlockSpec((1, tn), lambda j: (0, j))],
        out_specs=pl.BlockSpec((n, tn), lambda j: (0, j)),
        compiler_params=_cparams(("arbitrary",)),
        name="ada_mod",
    )(c, w_ada, b_ada.reshape(1, nout))


def _inproj_kernel(x_ref, sh_ref, sc_ref, g_ref, w_ref, o_ref, h_ref):
    @pl.when(pl.program_id(1) == 0)
    def _():
        x = x_ref[...]
        y = x * lax.rsqrt(jnp.mean(x * x, axis=-1, keepdims=True) + EPS) * g_ref[...]
        h_ref[...] = (y * (1.0 + sc_ref[0]) + sh_ref[0]).astype(BF16)

    o_ref[...] = jnp.dot(h_ref[...], w_ref[...], preferred_element_type=F32)


def _mod_spec(tm, seq_len):
    if seq_len == 1:
        return pl.BlockSpec((1, tm, D_MODEL), lambda i, *_: (0, i, 0))
    return pl.BlockSpec((1, 1, D_MODEL), lambda i, *_: ((i * tm) // seq_len, 0, 0))


def _row_tile(cap, m, seq_len):
    tm = min(cap, m if seq_len == 1 else seq_len)
    assert m % tm == 0 and (seq_len == 1 or seq_len % tm == 0)
    return tm


def _mod_arr(m, seq_len):
    return m[None] if seq_len == 1 else m[:, None, :]


def _inproj(x, shift, scale, g, w_in_r, seq_len):
    m = x.shape[0]
    tm = _row_tile(1024, m, seq_len)
    return pl.pallas_call(
        _inproj_kernel,
        out_shape=jax.ShapeDtypeStruct((m, PROJ_W), F32),
        grid=(m // tm, PROJ_W // IN_TN),
        in_specs=[pl.BlockSpec((tm, D_MODEL), lambda i, j: (i, 0)),
                  _mod_spec(tm, seq_len), _mod_spec(tm, seq_len),
                  pl.BlockSpec((1, D_MODEL), lambda i, j: (0, 0)),
                  pl.BlockSpec((D_MODEL, IN_TN), lambda i, j: (0, j))],
        out_specs=pl.BlockSpec((tm, IN_TN), lambda i, j: (i, j)),
        scratch_shapes=[pltpu.VMEM((tm, D_MODEL), BF16)],
        compiler_params=_cparams(("parallel", "arbitrary")),
        name="in_proj",
    )(x, _mod_arr(shift, seq_len), _mod_arr(scale, seq_len), g.reshape(1, D_MODEL), w_in_r)


def _softplus(x):
    return jnp.maximum(x, 0.0) + jnp.log1p(jnp.exp(-jnp.abs(x)))


def _split3(a):
    hi = a.astype(BF16)
    r = a - hi.astype(F32)
    mid = r.astype(BF16)
    lo = (r - mid.astype(F32)).astype(BF16)
    return hi, mid, lo


def _spread(pieces, ones2):
    hi, mid, lo = pieces
    out = jnp.dot(jnp.concatenate([hi, mid], axis=1), ones2, preferred_element_type=F32)
    if lo is not None:
        out = out + jnp.dot(lo, ones2[0:LANES], preferred_element_type=F32)
    return out


def _ssd_kernel(z_ref, x_ref, b_ref, c_ref, dt_ref,
                cw_ref, cb_ref, dtb_ref, alog_ref, dskip_ref, gn_ref, exp_ref, sel_ref, tri_ref,
                y_ref, st_ref, conv_ref,
                ext_s, xs_s, xdt_s, eax_s, tex_s, elx_s, acst_s, bm_s, cm_s, state_s):
    ci = pl.program_id(1)
    last_chunk = ci == pl.num_programs(1) - 1

    @pl.when(ci == 0)
    def _():
        ext_s[0:8, :] = jnp.zeros((8, CONV_DIM), F32)
        state_s[...] = jnp.zeros_like(state_s)

    ext_s[8:8 + CHUNK, 0:D_INNER] = x_ref[...]
    ext_s[8:8 + CHUNK, D_INNER:D_INNER + BC_WIDTH] = b_ref[...]
    ext_s[8:8 + CHUNK, D_INNER + BC_WIDTH:CONV_DIM] = c_ref[...]

    def conv(lo, hi):
        acc = cb_ref[:, lo:hi] + cw_ref[3:4, lo:hi] * ext_s[8:8 + CHUNK, lo:hi]
        for j in range(CONV_WIDTH - 1):
            acc = acc + cw_ref[j:j + 1, lo:hi] * ext_s[5 + j:5 + j + CHUNK, lo:hi]
        return _silu(acc)

    xs_s[...] = conv(0, D_INNER)
    bm_s[...] = conv(D_INNER, D_INNER + BC_WIDTH)
    cm_s[...] = conv(D_INNER + BC_WIDTH, CONV_DIM)
    tail = ext_s[5 + CHUNK:8 + CHUNK, :]
    conv_ref[0] = tail
    ext_s[5:8, :] = tail

    dt = _softplus(dt_ref[:, 0:LANES] + dtb_ref[...])
    da = dt * (-jnp.exp(alog_ref[...]))
    tri = tri_ref[...]
    acs = sum(jnp.dot(tri, p, preferred_element_type=F32) for p in _split3(da))
    acst_s[...] = acs.T
    acs_pieces = _split3(acs)
    last = acs[CHUNK - 1:CHUNK, :]
    two = lambda a: _split3(a)[:2] + (None,)
    exp2 = exp_ref[...]
    xdt_s[...] = xs_s[...] * _spread(two(dt), exp2)
    eax_s[...] = _spread(two(jnp.exp(acs)), exp2)
    tex_s[...] = _spread(two(jnp.exp(last - acs)), exp2)
    elx_s[...] = _spread(_split3(jnp.broadcast_to(jnp.exp(last), (8, LANES))), exp2)

    row = lax.broadcasted_iota(I32, (CHUNK, CHUNK), 0)
    col = lax.broadcasted_iota(I32, (CHUNK, CHUNK), 1)
    causal = row >= col
    low_half = col < SSM_HEAD_DIM

    def group(g, carry):
        o512 = pl.multiple_of(g * GROUP_WIDTH, GROUP_WIDTH)
        o128 = pl.multiple_of(g * SSM_STATE, SSM_STATE)
        bg = bm_s[:, pl.ds(o128, SSM_STATE)]
        cg16 = cm_s[:, pl.ds(o128, SSM_STATE)].astype(BF16)
        bg16 = bg.astype(BF16)
        cb = lax.dot_general(cg16, bg16, (((1,), (1,)), ((), ())), preferred_element_type=F32)
        sel2 = sel_ref[:, pl.ds(pl.multiple_of(g * HEADS_PER_GROUP * CHUNK, HEADS_PER_GROUP * CHUNK),
                                HEADS_PER_GROUP * CHUNK)]
        colb = _spread(acs_pieces, sel2)
        rows = acst_s[pl.ds(pl.multiple_of(g * HEADS_PER_GROUP, HEADS_PER_GROUP), HEADS_PER_GROUP), :]
        xdt_g = xdt_s[:, pl.ds(o512, GROUP_WIDTH)]
        parts = []
        for jp in range(HEADS_PER_GROUP // 2):
            ms = []
            for j in (2 * jp, 2 * jp + 1):
                diff = colb[:, j * CHUNK:(j + 1) * CHUNK] - rows[j:j + 1, :]
                dec = jnp.exp(jnp.where(causal, diff, -jnp.inf))
                ms.append((cb * dec).astype(BF16))
            xp = xdt_g[:, jp * LANES:(jp + 1) * LANES]
            rhs = jnp.concatenate([jnp.where(low_half, xp, 0.0), jnp.where(low_half, 0.0, xp)], axis=0)
            parts.append(jnp.dot(jnp.concatenate(ms, axis=1), rhs.astype(BF16), preferred_element_type=F32))
        y = jnp.concatenate(parts, axis=1)

        st = state_s[:, pl.ds(o512, GROUP_WIDTH)]
        y = y + jnp.dot(cg16, st.astype(BF16), preferred_element_type=F32) * eax_s[:, pl.ds(o512, GROUP_WIDTH)]
        xw = (xdt_g * tex_s[:, pl.ds(o512, GROUP_WIDTH)]).astype(BF16)
        st_new = (st * elx_s[0:1, pl.ds(o512, GROUP_WIDTH)]
                  + jnp.dot(bg.T.astype(BF16), xw, preferred_element_type=F32))
        state_s[:, pl.ds(o512, GROUP_WIDTH)] = st_new

        @pl.when(last_chunk)
        def _():
            st_ref[0, pl.ds(o512, GROUP_WIDTH), :] = st_new.T

        y = y + dskip_ref[:, pl.ds(o512, GROUP_WIDTH)] * xs_s[:, pl.ds(o512, GROUP_WIDTH)]
        u = y * _silu(z_ref[:, pl.ds(o512, GROUP_WIDTH)])
        u = u * lax.rsqrt(jnp.mean(u * u, axis=-1, keepdims=True) + EPS)
        y_ref[:, pl.ds(o512, GROUP_WIDTH)] = (u * gn_ref[:, pl.ds(o512, GROUP_WIDTH)]).astype(BF16)
        return carry

    lax.fori_loop(0, SSM_GROUPS, group, 0)


def _pad_heads(v):
    return jnp.pad(v.astype(F32), (0, LANES - SSM_HEADS)).reshape(1, LANES)


def _head_expand(dtype=BF16, copies=2):
    h = jnp.arange(LANES)[:, None]
    ch = jnp.arange(D_INNER)[None, :] // SSM_HEAD_DIM
    return jnp.tile((h == ch).astype(dtype), (copies, 1))


def _head_select():
    h = jnp.arange(LANES)[:, None]
    blk = jnp.arange(SSM_HEADS * CHUNK)[None, :] // CHUNK
    return jnp.tile((h == blk).astype(BF16), (2, 1))


def _ssd_prompt(proj, bsz, seq_len, lw):
    nc = seq_len // CHUNK
    m = bsz * seq_len
    tri = (jnp.arange(CHUNK)[:, None] >= jnp.arange(CHUNK)[None, :]).astype(BF16)
    const = lambda shape: pl.BlockSpec(shape, lambda b, c: (0, 0))
    rowblk = lambda width, col: pl.BlockSpec((CHUNK, width), lambda b, c: (b * nc + c, col // width))
    y, st, conv = pl.pallas_call(
        _ssd_kernel,
        out_shape=(jax.ShapeDtypeStruct((m, D_INNER), BF16),
                   jax.ShapeDtypeStruct((bsz, D_INNER, SSM_STATE), F32),
                   jax.ShapeDtypeStruct((bsz, CONV_WIDTH - 1, CONV_DIM), F32)),
        grid=(bsz, nc),
        in_specs=[rowblk(D_INNER, PC_Z), rowblk(D_INNER, PC_X), rowblk(BC_WIDTH, PC_B), rowblk(BC_WIDTH, PC_C),
                  rowblk(DT_PAD, PC_DT),
                  const((CONV_WIDTH, CONV_DIM)), const((1, CONV_DIM)), const((1, LANES)), const((1, LANES)),
                  const((1, D_INNER)), const((1, D_INNER)), const((2 * LANES, D_INNER)),
                  const((2 * LANES, SSM_HEADS * CHUNK)), const((CHUNK, CHUNK))],
        out_specs=(pl.BlockSpec((CHUNK, D_INNER), lambda b, c: (b * nc + c, 0)),
                   pl.BlockSpec((1, D_INNER, SSM_STATE), lambda b, c: (b, 0, 0)),
                   pl.BlockSpec((1, CONV_WIDTH - 1, CONV_DIM), lambda b, c: (b, 0, 0))),
        scratch_shapes=[pltpu.VMEM((8 + CHUNK, CONV_DIM), F32),
                        pltpu.VMEM((CHUNK, D_INNER), F32), pltpu.VMEM((CHUNK, D_INNER), F32),
                        pltpu.VMEM((CHUNK, D_INNER), F32), pltpu.VMEM((CHUNK, D_INNER), F32),
                        pltpu.VMEM((8, D_INNER), F32), pltpu.VMEM((LANES, CHUNK), F32),
                        pltpu.VMEM((CHUNK, BC_WIDTH), F32), pltpu.VMEM((CHUNK, BC_WIDTH), F32),
                        pltpu.VMEM((SSM_STATE, D_INNER), F32)],
        compiler_params=_cparams(("parallel", "arbitrary")),
        name="ssd_chunks",
    )(proj, proj, proj, proj, proj,
      lw['w_conv'], lw['b_conv'].reshape(1, CONV_DIM), _pad_heads(lw['dt_bias']), _pad_heads(lw['a_log']),
      jnp.repeat(lw['d_skip'].astype(F32), SSM_HEAD_DIM).reshape(1, D_INNER),
      lw['g_ssm_norm'].reshape(1, D_INNER), _head_expand(), _head_select(), tri)
    return y, st.reshape(bsz, SSM_HEADS, SSM_HEAD_DIM, SSM_STATE), conv


def _ssd_step_pre_kernel(x_ref, b_ref, c_ref, dt_ref, p0_ref, p1_ref, p2_ref,
                         cw_ref, cb_ref, dtb_ref, alog_ref, exp_ref,
                         xs_ref, xdt_ref, bm_ref, cm_ref, dec_ref, conv_ref):
    def conv(u_ref, lo, hi):
        acc = cb_ref[:, lo:hi] + cw_ref[3:4, lo:hi] * u_ref[...]
        for j, p_ref in enumerate((p0_ref, p1_ref, p2_ref)):
            acc = acc + cw_ref[j:j + 1, lo:hi] * p_ref[:, lo:hi]
        return _silu(acc)

    xs = conv(x_ref, 0, D_INNER)
    xs_ref[...] = xs
    bm_ref[...] = conv(b_ref, D_INNER, D_INNER + BC_WIDTH)
    cm_ref[...] = conv(c_ref, D_INNER + BC_WIDTH, CONV_DIM)
    dt = _softplus(dt_ref[:, 0:LANES] + dtb_ref[...])
    dec_ref[...] = jnp.exp(dt * (-jnp.exp(alog_ref[...])))
    xdt_ref[...] = xs * jnp.dot(dt, exp_ref[...], precision=HIGHEST, preferred_element_type=F32)
    conv_ref[0] = p1_ref[...]
    conv_ref[1] = p2_ref[...]
    conv_ref[2, :, 0:D_INNER] = x_ref[...]
    conv_ref[2, :, D_INNER:D_INNER + BC_WIDTH] = b_ref[...]
    conv_ref[2, :, D_INNER + BC_WIDTH:CONV_DIM] = c_ref[...]


def _ssd_step_kernel(dec_ref, st_ref, xdtt_ref, bm_ref, cmt_ref, xs_ref, z_ref, dskip_ref, gn_ref,
                     sto_ref, y_ref, yt_s, *, tb):
    i = pl.program_id(0)
    nb = xs_ref.shape[0]
    sub = lax.broadcasted_iota(I32, (nb, SSM_STATE), 0)
    lane = lax.broadcasted_iota(I32, (SSM_STATE, nb), 1)

    @pl.when(i == 0)
    def _():
        yt_s[...] = jnp.zeros_like(yt_s)

    for bi in range(tb):
        b = i * tb + bi
        for g in range(SSM_GROUPS):
            r0 = g * GROUP_WIDTH
            b_row = jnp.where(sub == b, bm_ref[:, g * SSM_STATE:(g + 1) * SSM_STATE], 0.0).astype(BF16)
            term = jnp.dot(xdtt_ref[r0:r0 + GROUP_WIDTH, :].astype(BF16), b_row, preferred_element_type=F32)
            news = []
            for j in range(HEADS_PER_GROUP):
                h = g * HEADS_PER_GROUP + j
                rr = r0 + j * SSM_HEAD_DIM
                new = (st_ref[bi, rr:rr + SSM_HEAD_DIM, :] * dec_ref[b, h]
                       + term[j * SSM_HEAD_DIM:(j + 1) * SSM_HEAD_DIM, :])
                sto_ref[bi, rr:rr + SSM_HEAD_DIM, :] = new
                news.append(new)
            new_g = jnp.concatenate(news, axis=0).astype(BF16)
            c_col = jnp.where(lane == b, cmt_ref[g], 0.0).astype(BF16)
            yt_s[r0:r0 + GROUP_WIDTH, :] += jnp.dot(new_g, c_col, preferred_element_type=F32)

    @pl.when(i == pl.num_programs(0) - 1)
    def _():
        for g in range(SSM_GROUPS):
            sl = slice(g * GROUP_WIDTH, (g + 1) * GROUP_WIDTH)
            y = yt_s[sl, :].T + dskip_ref[:, sl] * xs_ref[:, sl]
            u = y * _silu(z_ref[:, sl])
            u = u * lax.rsqrt(jnp.mean(u * u, axis=-1, keepdims=True) + EPS)
            y_ref[:, sl] = (u * gn_ref[:, sl]).astype(BF16)


def _ssd_sample(proj, state_ssm, state_conv, lw):
    nb = proj.shape[0]
    full = lambda shape: pl.BlockSpec(shape, lambda *_: tuple(0 for _ in shape))
    colblk = lambda width, col: pl.BlockSpec((nb, width), lambda *_: (0, col // width))
    prev = [state_conv[:, j, :] for j in range(CONV_WIDTH - 1)]
    tr = min(32, nb)
    rows = lambda width, col=0: pl.BlockSpec((tr, width), lambda i: (i, col // width))
    xs, xdt, bm, cm, dec, conv = pl.pallas_call(
        _ssd_step_pre_kernel,
        out_shape=(jax.ShapeDtypeStruct((nb, D_INNER), F32), jax.ShapeDtypeStruct((nb, D_INNER), F32),
                   jax.ShapeDtypeStruct((nb, BC_WIDTH), F32), jax.ShapeDtypeStruct((nb, BC_WIDTH), F32),
                   jax.ShapeDtypeStruct((nb, LANES), F32),
                   jax.ShapeDtypeStruct((CONV_WIDTH - 1, nb, CONV_DIM), F32)),
        grid=(nb // tr,),
        in_specs=[rows(D_INNER, PC_X), rows(BC_WIDTH, PC_B), rows(BC_WIDTH, PC_C), rows(DT_PAD, PC_DT),
                  rows(CONV_DIM), rows(CONV_DIM), rows(CONV_DIM),
                  full((CONV_WIDTH, CONV_DIM)), full((1, CONV_DIM)), full((1, LANES)), full((1, LANES)),
                  full((LANES, D_INNER))],
        out_specs=(rows(D_INNER), rows(D_INNER), rows(BC_WIDTH), rows(BC_WIDTH), rows(LANES),
                   pl.BlockSpec((CONV_WIDTH - 1, tr, CONV_DIM), lambda i: (0, i, 0))),
        compiler_params=_cparams(("parallel",)),
        name="ssd_step_pre",
    )(proj, proj, proj, proj, *prev,
      lw['w_conv'], lw['b_conv'].reshape(1, CONV_DIM), _pad_heads(lw['dt_bias']), _pad_heads(lw['a_log']),
      _head_expand(F32, 1))

    tb = 2
    cmt = cm.reshape(nb, SSM_GROUPS, SSM_STATE).transpose(1, 2, 0)
    st_new, y = pl.pallas_call(
        functools.partial(_ssd_step_kernel, tb=tb),
        out_shape=(jax.ShapeDtypeStruct((nb, D_INNER, SSM_STATE), F32),
                   jax.ShapeDtypeStruct((nb, D_INNER), BF16)),
        grid=(nb // tb,),
        in_specs=[pl.BlockSpec(memory_space=pltpu.SMEM),
                  pl.BlockSpec((tb, D_INNER, SSM_STATE), lambda i: (i, 0, 0)),
                  full((D_INNER, nb)), full((nb, BC_WIDTH)), full((SSM_GROUPS, SSM_STATE, nb)),
                  full((nb, D_INNER)), colblk(D_INNER, PC_Z), full((1, D_INNER)), full((1, D_INNER))],
        out_specs=(pl.BlockSpec((tb, D_INNER, SSM_STATE), lambda i: (i, 0, 0)),
                   full((nb, D_INNER))),
        scratch_shapes=[pltpu.VMEM((D_INNER, nb), F32)],
        compiler_params=_cparams(("arbitrary",)),
        name="ssd_step",
    )(dec, state_ssm.reshape(nb, D_INNER, SSM_STATE), xdt.T, bm, cmt, xs, proj,
      jnp.repeat(lw['d_skip'].astype(F32), SSM_HEAD_DIM).reshape(1, D_INNER),
      lw['g_ssm_norm'].reshape(1, D_INNER))
    return (y, st_new.reshape(nb, SSM_HEADS, SSM_HEAD_DIM, SSM_STATE), conv.transpose(1, 0, 2))


def _rope_tables(pos):
    half = ROPE_DIM // 2
    inv_freq = ROPE_THETA ** (-jnp.arange(half, dtype=F32) * 2.0 / ROPE_DIM)
    ang = pos.astype(F32)[:, None] * inv_freq[None, :]
    cos, sin = jnp.cos(ang), jnp.sin(ang)
    n = pos.shape[0]
    ones = jnp.ones((n, HEAD_DIM - ROPE_DIM), F32)
    zeros = jnp.zeros((n, HEAD_DIM - ROPE_DIM), F32)
    zh = jnp.zeros((n, half), F32)
    cos_f = jnp.concatenate([cos, cos, ones], axis=1)
    sin_a = jnp.concatenate([zh, sin, zeros], axis=1)
    sin_b = jnp.concatenate([-sin, zh, zeros], axis=1)
    tile = lambda t: jnp.tile(t, (1, LANES // HEAD_DIM))
    return tile(cos_f), tile(sin_a), tile(sin_b)


def _rope_slab(x, cos_f, sin_a, sin_b):
    half = ROPE_DIM // 2
    return x * cos_f + pltpu.roll(x, half, 1) * sin_a + pltpu.roll(x, LANES - half, 1) * sin_b


def _attn_kernel(sink_ref, q_ref, k_ref, v_ref, cos_ref, sin_a_ref, sin_b_ref,
                 y_ref, kn_ref, vn_ref, kspan_s, vspan_s, sc_s, p_s):
    bi = pl.program_id(1)
    nq = q_ref.shape[0]

    @pl.when(bi == 0)
    def _():
        kspan_s[0:nq, :] = jnp.zeros((nq, KV_WIDTH), F32)
        vspan_s[0:nq, :] = jnp.zeros((nq, KV_WIDTH), F32)

    cos_f, sin_a, sin_b = cos_ref[...], sin_a_ref[...], sin_b_ref[...]
    rope = lambda x: _rope_slab(x, cos_f, sin_a, sin_b)
    kr = jnp.concatenate([rope(k_ref[:, s * LANES:(s + 1) * LANES]) for s in range(KV_WIDTH // LANES)], axis=1)
    v = v_ref[...]
    kspan_s[nq:2 * nq, :] = kr
    vspan_s[nq:2 * nq, :] = v
    kn_ref[0] = kr
    vn_ref[0] = v

    t = lax.broadcasted_iota(I32, (nq, nq), 0)
    j = lax.broadcasted_iota(I32, (nq, nq), 1)
    own_block = j <= t
    low = lax.broadcasted_iota(I32, (2 * nq, LANES), 1) < HEAD_DIM
    scale = HEAD_DIM ** -0.5
    grp = ATTN_HEADS // KV_HEADS

    def block_diag(span_ref, kh):
        nat = span_ref[:, (kh // 2) * LANES:(kh // 2 + 1) * LANES]
        swp = pltpu.roll(nat, HEAD_DIM, 1)
        lo, hi = (nat, swp) if kh % 2 == 0 else (swp, nat)
        return jnp.concatenate([jnp.where(low, lo, 0.0), jnp.where(low, 0.0, hi)], axis=0).astype(BF16)

    for kh in range(KV_HEADS):
        kbd = block_diag(kspan_s, kh)
        for p in range(grp // 2):
            h0 = kh * grp + 2 * p
            qp = (rope(q_ref[:, h0 * HEAD_DIM:(h0 + 2) * HEAD_DIM]) * scale).astype(BF16)
            sc = lax.dot_general(qp, kbd, (((1,), (1,)), ((), ())), preferred_element_type=F32)
            for a in range(2):
                prev = sc[:, 2 * a * nq:(2 * a + 1) * nq]
                cur = sc[:, (2 * a + 1) * nq:(2 * a + 2) * nq]
                sc_s[h0 + a] = jnp.where(own_block, cur, prev)

    have_prev = bi > 0
    for h in range(ATTN_HEADS):
        sa = sc_s[h]
        sa = jnp.where(own_block | have_prev, sa, -jnp.inf)
        sink = sink_ref[h]
        mx = jnp.maximum(jnp.max(sa, axis=-1, keepdims=True), sink)
        pa = jnp.exp(sa - mx)
        pa = pa / (jnp.sum(pa, axis=-1, keepdims=True) + jnp.exp(sink - mx))
        p_s[h] = pa.astype(BF16)

    zero = jnp.zeros((nq, nq), BF16)
    for kh in range(KV_HEADS):
        vbd = block_diag(vspan_s, kh)
        for p in range(grp // 2):
            h0 = kh * grp + 2 * p
            parts = []
            for a in range(2):
                pa = p_s[h0 + a]
                parts += [jnp.where(own_block, zero, pa), jnp.where(own_block, pa, zero)]
            y_ref[:, h0 * HEAD_DIM:(h0 + 2) * HEAD_DIM] = jnp.dot(
                jnp.concatenate(parts, axis=1), vbd, preferred_element_type=F32).astype(BF16)

    kspan_s[0:nq, :] = kspan_s[nq:2 * nq, :]
    vspan_s[0:nq, :] = vspan_s[nq:2 * nq, :]


def _attn_prompt(proj, bsz, seq_len, sinks):
    nq = CHUNK
    nb = seq_len // nq
    m = bsz * seq_len
    cos_f, sin_a, sin_b = _rope_tables(jnp.arange(seq_len))
    rowblk = lambda width, col: pl.BlockSpec((nq, width), lambda b, i: (b * nb + i, col // width))
    tab = pl.BlockSpec((nq, LANES), lambda b, i: (i, 0))
    y, kn, vn = pl.pallas_call(
        _attn_kernel,
        out_shape=(jax.ShapeDtypeStruct((m, D_MODEL), BF16),
                   jax.ShapeDtypeStruct((bsz, WINDOW, KV_WIDTH), F32),
                   jax.ShapeDtypeStruct((bsz, WINDOW, KV_WIDTH), F32)),
        grid=(bsz, nb),
        in_specs=[pl.BlockSpec(memory_space=pltpu.SMEM),
                  rowblk(D_MODEL, PC_Q), rowblk(KV_WIDTH, PC_K), rowblk(KV_WIDTH, PC_V), tab, tab, tab],
        out_specs=(pl.BlockSpec((nq, D_MODEL), lambda b, i: (b * nb + i, 0)),
                   pl.BlockSpec((1, WINDOW, KV_WIDTH), lambda b, i: (b, 0, 0)),
                   pl.BlockSpec((1, WINDOW, KV_WIDTH), lambda b, i: (b, 0, 0))),
        scratch_shapes=[pltpu.VMEM((2 * nq, KV_WIDTH), F32), pltpu.VMEM((2 * nq, KV_WIDTH), F32),
                        pltpu.VMEM((ATTN_HEADS, nq, nq), F32), pltpu.VMEM((ATTN_HEADS, nq, nq), BF16)],
        compiler_params=_cparams(("parallel", "arbitrary")),
        name="attn_blocks",
    )(sinks.astype(F32), proj, proj, proj, cos_f, sin_a, sin_b)
    shape = (bsz, WINDOW, KV_HEADS, HEAD_DIM)
    return y, kn.reshape(shape), vn.reshape(shape)


def _attn_step_kernel(sink_ref, q_ref, k_ref, v_ref, cos_ref, sin_a_ref, sin_b_ref, fold_ref, foldt_ref,
                      kc_ref, vc_ref, y_ref, ko_ref, vo_ref, *, tb):
    cos_f, sin_a, sin_b = cos_ref[...], sin_a_ref[...], sin_b_ref[...]
    rope = lambda ref, w: jnp.concatenate(
        [_rope_slab(ref[:, s * LANES:(s + 1) * LANES], cos_f, sin_a, sin_b) for s in range(w // LANES)], axis=1)
    qr = rope(q_ref, D_MODEL) * (HEAD_DIM ** -0.5)
    kr = rope(k_ref, KV_WIDTH)
    v = v_ref[...]

    own = (lax.broadcasted_iota(I32, (ATTN_HEADS, D_MODEL), 1) // HEAD_DIM
           == lax.broadcasted_iota(I32, (ATTN_HEADS, D_MODEL), 0))
    qb = jnp.concatenate([jnp.where(own, jnp.broadcast_to(qr[bi:bi + 1], (ATTN_HEADS, D_MODEL)), 0.0)
                          for bi in range(tb)], axis=0).astype(BF16)
    qm = jnp.dot(qb, fold_ref[...], preferred_element_type=F32).astype(BF16)

    row = lax.broadcasted_iota(I32, (WINDOW, KV_WIDTH), 0)
    w = lax.broadcasted_iota(I32, (ATTN_HEADS, WINDOW), 1)
    sink = sink_ref[...]
    bf = lambda x: x.astype(BF16)
    outs = []
    for bi in range(tb):
        qmb = qm[bi * ATTN_HEADS:(bi + 1) * ATTN_HEADS]
        kc, vc = kc_ref[bi], vc_ref[bi]
        kn, vn = kr[bi:bi + 1], v[bi:bi + 1]
        sc = lax.dot_general(qmb, bf(kc), (((1,), (1,)), ((), ())), preferred_element_type=F32)
        sc = jnp.where(w >= 1, sc, -jnp.inf)
        s_new = jnp.sum(qmb.astype(F32) * bf(kn).astype(F32), axis=-1, keepdims=True)
        mx = jnp.maximum(jnp.maximum(jnp.max(sc, axis=-1, keepdims=True), s_new), sink)
        p = jnp.exp(sc - mx)
        p_new = jnp.exp(s_new - mx)
        den = jnp.sum(p, axis=-1, keepdims=True) + p_new + jnp.exp(sink - mx)
        p = p / den
        p_new = p_new / den
        out = jnp.dot(bf(p), bf(vc), preferred_element_type=F32) + bf(p_new).astype(F32) * bf(vn).astype(F32)
        outs.append(bf(out))
        ko_ref[bi] = jnp.where(row == WINDOW - 1, kn, pltpu.roll(kc, WINDOW - 1, 0))
        vo_ref[bi] = jnp.where(row == WINDOW - 1, vn, pltpu.roll(vc, WINDOW - 1, 0))

    of = jnp.dot(jnp.concatenate(outs, axis=0), foldt_ref[...], preferred_element_type=F32)
    of = of.reshape(tb, ATTN_HEADS, D_MODEL)
    y_ref[...] = jnp.sum(jnp.where(own[None], of, 0.0), axis=1).astype(BF16)


def _attn_sample(proj, cache_k, cache_v, sinks):
    nb = proj.shape[0]
    tb = 8
    grp = ATTN_HEADS // KV_HEADS
    cos_f, sin_a, sin_b = _rope_tables(jnp.full((1,), PAST_LEN))
    c = jnp.arange(D_MODEL)[:, None]
    l = jnp.arange(KV_WIDTH)[None, :]
    fold = ((c % HEAD_DIM == l % HEAD_DIM) & ((c // HEAD_DIM) // grp == l // HEAD_DIM)).astype(BF16)
    full = lambda shape: pl.BlockSpec(shape, lambda i: tuple(0 for _ in shape))
    rowblk = lambda width, col: pl.BlockSpec((tb, width), lambda i: (i, col // width))
    win = pl.BlockSpec((tb, WINDOW, KV_WIDTH), lambda i: (i, 0, 0))
    y, kn, vn = pl.pallas_call(
        functools.partial(_attn_step_kernel, tb=tb),
        out_shape=(jax.ShapeDtypeStruct((nb, D_MODEL), BF16),
                   jax.ShapeDtypeStruct((nb, WINDOW, KV_WIDTH), F32),
                   jax.ShapeDtypeStruct((nb, WINDOW, KV_WIDTH), F32)),
        grid=(nb // tb,),
        in_specs=[full((ATTN_HEADS, 1)), rowblk(D_MODEL, PC_Q), rowblk(KV_WIDTH, PC_K), rowblk(KV_WIDTH, PC_V),
                  full((1, LANES)), full((1, LANES)), full((1, LANES)),
                  full((D_MODEL, KV_WIDTH)), full((KV_WIDTH, D_MODEL)), win, win],
        out_specs=(pl.BlockSpec((tb, D_MODEL), lambda i: (i, 0)), win, win),
        compiler_params=_cparams(("parallel",)),
        name="attn_step",
    )(sinks.astype(F32).reshape(ATTN_HEADS, 1), proj, proj, proj, cos_f, sin_a, sin_b, fold, fold.T,
      cache_k.reshape(nb, WINDOW, KV_WIDTH), cache_v.reshape(nb, WINDOW, KV_WIDTH))
    shape = (nb, WINDOW, KV_HEADS, HEAD_DIM)
    return y, kn.reshape(shape), vn.reshape(shape)


def _merge_kernel(ys_ref, ya_ref, ws_ref, wa_ref, gs_ref, ga_ref, o_ref):
    a = jnp.dot(ys_ref[...], ws_ref[...], preferred_element_type=F32)
    b = jnp.dot(ya_ref[...], wa_ref[...], preferred_element_type=F32)
    o_ref[...] = (jax.nn.sigmoid(gs_ref[...]) * a + jax.nn.sigmoid(ga_ref[...]) * b).astype(BF16)


def _merge(y_ssd, y_att, proj, w_ssd, w_att):
    m = y_ssd.shape[0]
    tm = min(512, m)
    tn = 512
    g0 = PC_G // tn
    return pl.pallas_call(
        _merge_kernel,
        out_shape=jax.ShapeDtypeStruct((m, D_MODEL), BF16),
        grid=(m // tm, D_MODEL // tn),
        in_specs=[pl.BlockSpec((tm, D_INNER), lambda i, j: (i, 0)),
                  pl.BlockSpec((tm, D_MODEL), lambda i, j: (i, 0)),
                  pl.BlockSpec((D_INNER, tn), lambda i, j: (0, j)),
                  pl.BlockSpec((D_MODEL, tn), lambda i, j: (0, j)),
                  pl.BlockSpec((tm, tn), lambda i, j: (i, g0 + j)),
                  pl.BlockSpec((tm, tn), lambda i, j: (i, g0 + D_MODEL // tn + j))],
        out_specs=pl.BlockSpec((tm, tn), lambda i, j: (i, j)),
        compiler_params=_cparams(("parallel", "arbitrary")),
        name="branch_merge",
    )(y_ssd, y_att, w_ssd, w_att, proj, proj)


def _outproj_kernel(pre_ref, x_ref, gate_ref, sh_ref, sc_ref, g_ref, w_ref, wr_ref, br_ref,
                    x1_ref, h2_ref, tg_ref, ti_ref):
    mixed = jnp.dot(pre_ref[...], w_ref[...], preferred_element_type=F32)
    x1 = x_ref[...] + gate_ref[0] * mixed
    x1_ref[...] = x1
    y = x1 * lax.rsqrt(jnp.mean(x1 * x1, axis=-1, keepdims=True) + EPS) * g_ref[...]
    h2 = y * (1.0 + sc_ref[0]) + sh_ref[0]
    h2_ref[...] = h2
    logits = jnp.dot(h2, wr_ref[...], precision=HIGHEST, preferred_element_type=F32) + br_ref[...]
    lane = lax.broadcasted_iota(I32, logits.shape, 1)
    lane_f = lane.astype(F32)
    vals = jnp.zeros_like(logits)
    idxs = jnp.zeros_like(logits)
    top = None
    den = jnp.zeros((logits.shape[0], 1), F32)
    for k in range(TOP_K):
        mx = jnp.max(logits, axis=-1, keepdims=True)
        ix = jnp.min(jnp.where(logits == mx, lane_f, float(LANES)), axis=-1, keepdims=True)
        top = mx if top is None else top
        e = jnp.exp(mx - top)
        den = den + e
        vals = jnp.where(lane == k, e, vals)
        idxs = jnp.where(lane == k, ix, idxs)
        logits = jnp.where(lane_f == ix, -jnp.inf, logits)
    tg_ref[...] = vals / den
    ti_ref[...] = idxs.astype(I32)


def _outproj(pre, x, gate1, shift2, scale2, g2, w_out, w_router, b_router, seq_len):
    m = x.shape[0]
    tm = _row_tile(512, m, seq_len)
    wr =jnp.pad(w_router.astype(F32), ((0, 0), (0, LANES - N_EXPERTS)))
    br = jnp.pad(b_router.astype(F32), (0, LANES - N_EXPERTS), constant_values=-jnp.inf).reshape(1, LANES)
    row = lambda width: pl.BlockSpec((tm, width), lambda i: (i, 0))
    const = lambda shape: pl.BlockSpec(shape, lambda i: (0, 0))
    return pl.pallas_call(
        _outproj_kernel,
        out_shape=(jax.ShapeDtypeStruct((m, D_MODEL), F32), jax.ShapeDtypeStruct((m, D_MODEL), F32),
                   jax.ShapeDtypeStruct((m, LANES), F32), jax.ShapeDtypeStruct((m, LANES), I32)),
        grid=(m // tm,),
        in_specs=[row(D_MODEL), row(D_MODEL), _mod_spec(tm, seq_len), _mod_spec(tm, seq_len), _mod_spec(tm, seq_len),
                  const((1, D_MODEL)), const((D_MODEL, D_MODEL)), const((D_MODEL, LANES)), const((1, LANES))],
        out_specs=(row(D_MODEL), row(D_MODEL), row(LANES), row(LANES)),
        compiler_params=_cparams(("parallel",)),
        name="out_proj_router",
    )(pre, x, _mod_arr(gate1, seq_len), _mod_arr(shift2, seq_len), _mod_arr(scale2, seq_len),
      g2.reshape(1, D_MODEL), w_out, wr, br)


def _ffn_kernel(be_ref, na_ref, tok_hbm, h_hbm, w1g_ref, w1l_ref, b1g_ref, b1l_ref, w2_ref, b2_ref,
                o_ref, tok_s, rows_s, xb_s, acc_s, tok_sem, row_sem):
    b = pl.program_id(0)
    f = pl.program_id(1)
    n_blocks = pl.num_programs(0)
    nf = D_FF // MOE_TF
    rows_per_step = MOE_BLK // nf
    n_active = na_ref[0]
    active = b < n_active
    nxt = jnp.minimum(b + 1, n_blocks - 1)
    nxt_slot = (b + 1) % 2

    def tok_copy(blk, sl):
        return pltpu.make_async_copy(tok_hbm.at[blk], tok_s.at[sl], tok_sem.at[sl])

    def row_copy(tok, r):
        return pltpu.make_async_copy(h_hbm.at[pl.ds(tok, 1)], rows_s.at[pl.ds(r, 1)], row_sem)

    def wait_rows():
        for r in range(MOE_BLK):
            row_copy(0, r).wait()

    @pl.when(active & (f == 0))
    def _():
        @pl.when(b == 0)
        def _():
            first = tok_copy(0, 0)
            first.start()
            first.wait()

            def body(r, c):
                row_copy(tok_s[0, r], r).start()
                return c
            lax.fori_loop(0, MOE_BLK, body, 0, unroll=8)
            tok_copy(nxt, 1).start()

        wait_rows()
        xb_s[...] = rows_s[...].astype(BF16)
        tok_copy(nxt, nxt_slot).wait()

    @pl.when(active)
    def _():
        base = f * rows_per_step
        for i in range(rows_per_step):
            row_copy(tok_s[nxt_slot, base + i], base + i).start()
        xb = xb_s[...]
        glu = jnp.dot(xb, w1g_ref[0], preferred_element_type=F32) + b1g_ref[0]
        lin = jnp.dot(xb, w1l_ref[0], preferred_element_type=F32) + b1l_ref[0]
        glu = jnp.minimum(glu, SWIGLU_LIMIT)
        lin = jnp.clip(lin, -SWIGLU_LIMIT, SWIGLU_LIMIT)
        act = glu * jax.nn.sigmoid(SWIGLU_ALPHA * glu) * (lin + 1.0)
        part = jnp.dot(act.astype(BF16), w2_ref[0], preferred_element_type=F32)

        @pl.when(f == 0)
        def _():
            acc_s[...] = part + b2_ref[0]

        @pl.when((f > 0) & (f < nf - 1))
        def _():
            acc_s[...] += part

        @pl.when(f == nf - 1)
        def _():
            o_ref[...] = acc_s[...] + part
            tok_copy(jnp.minimum(b + 2, n_blocks - 1), b % 2).start()

    @pl.when(((b == n_active) & (f == 0)) | (active & (b == n_blocks - 1) & (f == nf - 1)))
    def _():
        wait_rows()
        tok_copy(0, jnp.where(active, b, b + 1) % 2).wait()

    @pl.when((b >= n_active) & (f == nf - 1))
    def _():
        o_ref[...] = jnp.zeros_like(o_ref)


def _ffn(h_all, slot_tok, block_expert, n_active, w1, b1, w2, b2):
    n_blocks = slot_tok.shape[0]
    nf = D_FF // MOE_TF

    def widx(col_off):
        def index_map(b, f, be, na):
            live = b < na[0]
            bb = jnp.where(live, b, na[0] - 1)
            ff = jnp.where(live, f, nf - 1)
            return be[bb], 0, col_off + ff
        return index_map

    def w2idx(b, f, be, na):
        live = b < na[0]
        return be[jnp.where(live, b, na[0] - 1)], jnp.where(live, f, nf - 1), 0

    def b2idx(b, f, be, na):
        return be[jnp.where(b < na[0], b, na[0] - 1)], 0, 0

    grid_spec = pltpu.PrefetchScalarGridSpec(
        num_scalar_prefetch=2,
        grid=(n_blocks, nf),
        in_specs=[pl.BlockSpec(memory_space=pl.ANY), pl.BlockSpec(memory_space=pl.ANY),
                  pl.BlockSpec((1, D_MODEL, MOE_TF), widx(0)), pl.BlockSpec((1, D_MODEL, MOE_TF), widx(nf)),
                  pl.BlockSpec((1, 1, MOE_TF), widx(0)), pl.BlockSpec((1, 1, MOE_TF), widx(nf)),
                  pl.BlockSpec((1, MOE_TF, D_MODEL), w2idx), pl.BlockSpec((1, 1, D_MODEL), b2idx)],
        out_specs=pl.BlockSpec((MOE_BLK, D_MODEL), lambda b, f, be, na: (b, 0)),
        scratch_shapes=[pltpu.SMEM((2, MOE_BLK), I32),
                        pltpu.VMEM((MOE_BLK, D_MODEL), F32),
                        pltpu.VMEM((MOE_BLK, D_MODEL), BF16),
                        pltpu.VMEM((MOE_BLK, D_MODEL), F32),
                        pltpu.SemaphoreType.DMA((2,)), pltpu.SemaphoreType.DMA])
    return pl.pallas_call(
        _ffn_kernel,
        out_shape=jax.ShapeDtypeStruct((n_blocks * MOE_BLK, D_MODEL), F32),
        grid_spec=grid_spec,
        compiler_params=_cparams(("arbitrary", "arbitrary")),
        name="expert_ffn",
    )(block_expert, n_active, slot_tok, h_all, w1, w1, b1[:, None, :], b1[:, None, :], w2, b2[:, None, :])


def _route(top_idx):
    n_tok = top_idx.shape[0]
    n_assign = n_tok * TOP_K
    e_flat = top_idx.reshape(-1)
    onehot = (e_flat[:, None] == jnp.arange(N_EXPERTS)[None, :]).astype(I32)
    csum = jnp.cumsum(onehot, axis=0)
    rank = jnp.sum(onehot * csum, axis=1) - 1
    counts = csum[-1]
    nblk = (counts + MOE_BLK - 1) // MOE_BLK
    blk_end = jnp.cumsum(nblk)
    pad_start = (blk_end - nblk) * MOE_BLK
    dest = jnp.sum(onehot * pad_start[None, :], axis=1) + rank
    n_blocks = -(-n_assign // MOE_BLK) + N_EXPERTS
    slot_tok = jnp.zeros((n_blocks * MOE_BLK,), I32).at[dest].set(
        jnp.arange(n_assign, dtype=I32) // TOP_K, unique_indices=True)
    block_expert = jnp.minimum(jnp.searchsorted(blk_end, jnp.arange(n_blocks), side='right'),
                               N_EXPERTS - 1).astype(I32)
    n_active = blk_end[-1:].astype(I32)
    return dest.reshape(n_tok, TOP_K).astype(I32), slot_tok.reshape(n_blocks, MOE_BLK), block_expert, n_active


def _combine_kernel(dest_hbm, y_hbm, x1_ref, tg_ref, gate_ref, gf_ref, o_ref, dest_s, rows_s, dest_sem, row_sem,
                    *, tc):
    i = pl.program_id(0)
    n = pl.num_programs(0)
    n_rows = TOP_K * tc
    slot = i % 2
    nxt = jnp.minimum(i + 1, n - 1)

    def dest_copy(tile, sl):
        return pltpu.make_async_copy(dest_hbm.at[tile], dest_s.at[sl], dest_sem.at[sl])

    def row_copy(src, r, sl):
        return pltpu.make_async_copy(y_hbm.at[pl.ds(src, 1)], rows_s.at[sl, pl.ds(r, 1)], row_sem.at[sl])

    def wait_rows(sl):
        for r in range(n_rows):
            row_copy(0, r, sl).wait()

    @pl.when(i == 0)
    def _():
        first = dest_copy(0, 0)
        first.start()
        first.wait()

        def body(r, c):
            row_copy(dest_s[0, r], r, 0).start()
            return c
        lax.fori_loop(0, n_rows, body, 0, unroll=8)
        dest_copy(nxt, 1).start()

    dest_copy(nxt, 1 - slot).wait()
    wait_rows(slot)

    for r in range(n_rows):
        row_copy(dest_s[1 - slot, r], r, 1 - slot).start()
    tg = tg_ref[...]
    ffn = tg[:, 0:1] * rows_s[slot, 0:tc, :]
    for k in range(1, TOP_K):
        ffn = ffn + tg[:, k:k + 1] * rows_s[slot, k * tc:(k + 1) * tc, :]
    x2 = x1_ref[...] + gate_ref[0] * ffn
    o_ref[...] = x2 * lax.rsqrt(jnp.mean(x2 * x2, axis=-1, keepdims=True) + EPS) * gf_ref[...]
    dest_copy(jnp.minimum(i + 2, n - 1), slot).start()

    @pl.when(i == n - 1)
    def _():
        wait_rows(1 - slot)
        dest_copy(0, slot).wait()


def _combine(dest, y_sorted, x1, top_gate, gate2, g_final, seq_len):
    m = x1.shape[0]
    tc = _row_tile(256, m, seq_len)
    dest_t = dest.reshape(m // tc, tc, TOP_K).transpose(0, 2, 1).reshape(m // tc, TOP_K * tc)
    row = lambda width: pl.BlockSpec((tc, width), lambda i: (i, 0))
    return pl.pallas_call(
        functools.partial(_combine_kernel, tc=tc),
        out_shape=jax.ShapeDtypeStruct((m, D_MODEL), F32),
        grid=(m // tc,),
        in_specs=[pl.BlockSpec(memory_space=pl.ANY), pl.BlockSpec(memory_space=pl.ANY),
                  row(D_MODEL), row(LANES), _mod_spec(tc, seq_len), pl.BlockSpec((1, D_MODEL), lambda i: (0, 0))],
        out_specs=row(D_MODEL),
        scratch_shapes=[pltpu.SMEM((2, TOP_K * tc), I32), pltpu.VMEM((2, TOP_K * tc, D_MODEL), F32),
                        pltpu.SemaphoreType.DMA((2,)), pltpu.SemaphoreType.DMA((2,))],
        compiler_params=_cparams(("arbitrary",)),
        name="moe_combine",
    )(dest_t, y_sorted, x1, top_gate, _mod_arr(gate2, seq_len), g_final.reshape(1, D_MODEL))


def _reorder_w_in(w_in):
    w = w_in.astype(BF16)
    dt_cols = jnp.pad(w[:, XBC_END:DT_END], ((0, 0), (0, DT_PAD - SSM_HEADS)))
    return jnp.concatenate([w[:, :XBC_END], w[:, DT_END:], dt_cols], axis=1)


def _mixer_tail(x, y_ssd, y_att, proj, mod, lw, seq_len):
    pre = _merge(y_ssd, y_att, proj, lw['w_br_ssd'], lw['w_br_att'])
    return _outproj(pre, x, mod[2], mod[3], mod[4], lw['g_norm2'], lw['w_out'], lw['w_router'], lw['b_router'],
                    seq_len)


def _forward(x_prompt, x_sample, state_ssm, state_conv, cache_win_k, cache_win_v, c_prompt, c_sample, lw, g_final):
    bp, lp, d = x_prompt.shape
    bs = x_sample.shape[0]
    mp = bp * lp
    xp = x_prompt.reshape(mp, d)
    xs = x_sample.reshape(bs, d)

    mod = _ada(jnp.concatenate([c_prompt, c_sample], axis=0), lw['w_ada'], lw['b_ada'])
    mod_p = [mod[:bp, i * d:(i + 1) * d] for i in range(N_MOD)]
    mod_s = [mod[bp:, i * d:(i + 1) * d] for i in range(N_MOD)]

    proj_p = _inproj(xp, mod_p[0], mod_p[1], lw['g_norm1'], lw['w_in'], lp)
    proj_s = _inproj(xs, mod_s[0], mod_s[1], lw['g_norm1'], lw['w_in'], 1)

    yssd_p, ssm_p, conv_p = _ssd_prompt(proj_p, bp, lp, lw)
    yatt_p, wk_p, wv_p = _attn_prompt(proj_p, bp, lp, lw['sinks'])
    yssd_s, ssm_s, conv_s = _ssd_sample(proj_s, state_ssm, state_conv, lw)
    yatt_s, wk_s, wv_s = _attn_sample(proj_s, cache_win_k, cache_win_v, lw['sinks'])

    x1_p, h2_p, tg_p, ti_p = _mixer_tail(xp, yssd_p, yatt_p, proj_p, mod_p, lw, lp)
    x1_s, h2_s, tg_s, ti_s = _mixer_tail(xs, yssd_s, yatt_s, proj_s, mod_s, lw, 1)

    h_all = jnp.concatenate([h2_p, h2_s], axis=0)
    top_idx = jnp.concatenate([ti_p[:, :TOP_K], ti_s[:, :TOP_K]], axis=0)
    dest, slot_tok, block_expert, n_active = _route(top_idx)
    y_sorted = _ffn(h_all, slot_tok, block_expert, n_active,
                    lw['w_expert_in'], lw['b_expert_in'], lw['w_expert_out'], lw['b_expert_out'])

    y_p = _combine(dest[:mp], y_sorted, x1_p, tg_p, mod_p[5], g_final, lp)
    y_s = _combine(dest[mp:], y_sorted, x1_s, tg_s, mod_s[5], g_final, 1)
    return (y_p.reshape(bp, lp, d), y_s.reshape(bs, 1, d),
            ssm_p[None], conv_p[None], wk_p[None], wv_p[None],
            ssm_s[None], conv_s[None], wk_s[None], wv_s[None])


def kernel(x_prompt, x_sample, state_ssm, state_conv, cache_win_k, cache_win_v, c_prompt, c_sample, w_ada, b_ada, g_norm1, w_in, w_conv, b_conv, dt_bias, a_log, d_skip, g_ssm_norm, sinks, w_br_ssd, w_br_att, w_out, g_norm2, w_router, b_router, w_expert_in, b_expert_in, w_expert_out, b_expert_out, g_final):
    assert w_ada.shape[0] == 1, "single-layer stack"
    lw = dict(w_ada=w_ada[0], b_ada=b_ada[0], g_norm1=g_norm1[0], w_in=_reorder_w_in(w_in[0]),
              w_conv=w_conv[0], b_conv=b_conv[0], dt_bias=dt_bias[0], a_log=a_log[0], d_skip=d_skip[0],
              g_ssm_norm=g_ssm_norm[0], sinks=sinks[0],
              w_br_ssd=w_br_ssd[0].astype(BF16), w_br_att=w_br_att[0].astype(BF16), w_out=w_out[0].astype(BF16),
              g_norm2=g_norm2[0], w_router=w_router[0], b_router=b_router[0],
              w_expert_in=w_expert_in[0].astype(BF16), b_expert_in=b_expert_in[0],
              w_expert_out=w_expert_out[0].astype(BF16), b_expert_out=b_expert_out[0])
    return _forward(x_prompt, x_sample, state_ssm[0], state_conv[0], cache_win_k[0], cache_win_v[0],
                    c_prompt, c_sample, lw, g_final)
```

```python
import functools
import math

import jax
import jax.numpy as jnp
from jax import lax
from jax.experimental import pallas as pl
from jax.experimental.pallas import tpu as pltpu

F32 = jnp.float32
BF16 = jnp.bfloat16
I32 = jnp.int32
HIGHEST = lax.Precision.HIGHEST

D_MODEL = 2048
PAST_LEN = 16384
D_INNER = 2 * D_MODEL
SSM_HEAD_DIM = 64
SSM_HEADS = D_INNER // SSM_HEAD_DIM
SSM_GROUPS = 8
HEADS_PER_GROUP = SSM_HEADS // SSM_GROUPS
GROUP_WIDTH = HEADS_PER_GROUP * SSM_HEAD_DIM
SSM_STATE = 128
CONV_WIDTH = 4
BC_WIDTH = SSM_GROUPS * SSM_STATE
CONV_DIM = D_INNER + 2 * BC_WIDTH
CHUNK = 128
ATTN_HEADS = 32
KV_HEADS = 8
HEAD_DIM = 64
KV_WIDTH = KV_HEADS * HEAD_DIM
WINDOW = 128
ROPE_DIM = HEAD_DIM // 4
ROPE_THETA = 500000.0
N_EXPERTS = 32
TOP_K = 4
D_FF = D_MODEL
SWIGLU_LIMIT = 7.0
SWIGLU_ALPHA = 1.702
N_MOD = 6
EPS = 1e-6
Z_END = D_INNER
XBC_END = Z_END + CONV_DIM
DT_END = XBC_END + SSM_HEADS
Q_END = DT_END + D_MODEL
K_END = Q_END + KV_WIDTH
V_END = K_END + KV_WIDTH

LANES = 128
BF16_SUBLANES = 16
VMEM_LIMIT = 56 * 1024 * 1024

PC_Z = 0
PC_X = D_INNER
PC_B = PC_X + D_INNER
PC_C = PC_B + BC_WIDTH
PC_Q = PC_C + BC_WIDTH
PC_K = PC_Q + D_MODEL
PC_V = PC_K + KV_WIDTH
PC_G = PC_V + KV_WIDTH
PC_DT = PC_G + 2 * D_MODEL
DT_PAD = 512
PROJ_W = PC_DT + DT_PAD
IN_TN = 1280

MOE_BLK = 512
MOE_TF = 1024


def _cparams(sem):
    return pltpu.CompilerParams(dimension_semantics=sem, vmem_limit_bytes=VMEM_LIMIT)


def _silu(x):
    return x * jax.nn.sigmoid(x)


def _ada_kernel(c_ref, w_ref, b_ref, o_ref):
    s = _silu(c_ref[...]).astype(BF16)
    o_ref[...] = jnp.dot(s, w_ref[...].astype(BF16), preferred_element_type=F32) + b_ref[...]


def _ada(c, w_ada, b_ada):
    n, d = c.shape
    nout = w_ada.shape[1]
    tn = 1024
    return pl.pallas_call(
        _ada_kernel,
        out_shape=jax.ShapeDtypeStruct((n, nout), F32),
        grid=(nout // tn,),
        in_specs=[pl.BlockSpec((n, d), lambda j: (0, 0)),
                  pl.BlockSpec((d, tn), lambda j: (0, j)),
                  pl.BlockSpec((1, tn), lambda j: (0, j))],
        out_specs=pl.BlockSpec((n, tn), lambda j: (0, j)),
        compiler_params=_cparams(("arbitrary",)),
        name="ada_mod",
    )(c, w_ada, b_ada.reshape(1, nout))


def _inproj_kernel(x_ref, sh_ref, sc_ref, g_ref, w_ref, o_ref, h_ref):
    @pl.when(pl.program_id(1) == 0)
    def _():
        x = x_ref[...]
        y = x * lax.rsqrt(jnp.mean(x * x, axis=-1, keepdims=True) + EPS) * g_ref[...]
        h_ref[...] = (y * (1.0 + sc_ref[0]) + sh_ref[0]).astype(BF16)

    o_ref[...] = jnp.dot(h_ref[...], w_ref[...], preferred_element_type=F32)


def _inproj_cast_kernel(x_ref, sh_ref, sc_ref, g_ref, w_ref, e1_ref, e2_ref, o_ref, c1_ref, c2_ref, h_ref):
    _inproj_kernel(x_ref, sh_ref, sc_ref, g_ref, w_ref, o_ref, h_ref)
    c1_ref[...] = e1_ref[...].astype(BF16)
    c2_ref[...] = e2_ref[...].astype(BF16)


def _mod_spec(tm, seq_len):
    if seq_len == 1:
        return pl.BlockSpec((1, tm, D_MODEL), lambda i, *_: (0, i, 0))
    return pl.BlockSpec((1, 1, D_MODEL), lambda i, *_: ((i * tm) // seq_len, 0, 0))


def _row_tile(cap, m, seq_len):
    tm = min(cap, m if seq_len == 1 else seq_len)
    assert m % tm == 0 and (seq_len == 1 or seq_len % tm == 0)
    return tm


def _mod_arr(m, seq_len):
    return m[None] if seq_len == 1 else m[:, None, :]


def _inproj(x, shift, scale, g, w_in_r, seq_len, narrow=()):
    m = x.shape[0]
    tm = _row_tile(1024, m, seq_len)
    ni, nj = m // tm, PROJ_W // IN_TN
    in_specs = [pl.BlockSpec((tm, D_MODEL), lambda i, j: (i, 0), pipeline_mode=pl.Buffered(1)),
                _mod_spec(tm, seq_len), _mod_spec(tm, seq_len),
                pl.BlockSpec((1, D_MODEL), lambda i, j: (0, 0)),
                pl.BlockSpec((D_MODEL, IN_TN), lambda i, j: (0, j))]
    out_specs = [pl.BlockSpec((tm, IN_TN), lambda i, j: (i, j))]
    out_shape = [jax.ShapeDtypeStruct((m, PROJ_W), F32)]
    assert len(narrow) in (0, 2)
    flat = [a.reshape(-1, a.shape[-1]) for a in narrow]
    for a in flat:
        rows = a.shape[0]
        slab = pl.cdiv(pl.cdiv(rows, ni * nj), BF16_SUBLANES) * BF16_SUBLANES
        last = pl.cdiv(rows, slab) - 1
        spec = pl.BlockSpec((slab, a.shape[1]), lambda i, j, last=last: (jnp.minimum(i * nj + j, last), 0))
        in_specs.append(spec)
        out_specs.append(spec)
        out_shape.append(jax.ShapeDtypeStruct(a.shape, BF16))
    outs = pl.pallas_call(
        _inproj_cast_kernel if narrow else _inproj_kernel,
        out_shape=tuple(out_shape),
        grid=(ni, nj),
        in_specs=in_specs,
        out_specs=tuple(out_specs),
        scratch_shapes=[pltpu.VMEM((tm, D_MODEL), BF16)],
        compiler_params=_cparams(("arbitrary", "arbitrary") if narrow else ("parallel", "arbitrary")),
        name="in_proj",
    )(x, _mod_arr(shift, seq_len), _mod_arr(scale, seq_len), g.reshape(1, D_MODEL), w_in_r, *flat)
    return (outs[0],) + tuple(o.reshape(a.shape) for o, a in zip(outs[1:], narrow))


def _softplus(x):
    return jnp.maximum(x, 0.0) + jnp.log1p(jnp.exp(-jnp.abs(x)))


def _split3(a):
    hi = a.astype(BF16)
    r = a - hi.astype(F32)
    mid = r.astype(BF16)
    lo = (r - mid.astype(F32)).astype(BF16)
    return hi, mid, lo


def _spread(pieces, ones2):
    hi, mid, lo = pieces
    out = jnp.dot(jnp.concatenate([hi, mid], axis=1), ones2, preferred_element_type=F32)
    if lo is not None:
        out = out + jnp.dot(lo, ones2[0:LANES], preferred_element_type=F32)
    return out


def _ssd_kernel(z_ref, x_ref, b_ref, c_ref, dt_ref,
                cw_ref, cb_ref, dtb_ref, alog_ref, dskip_ref, gn_ref, exp_ref, sel_ref, tri_ref,
                y_ref, st_ref, conv_ref,
                ext_s, xs_s, xdt_s, eax_s, tex_s, elx_s, acst_s, bm_s, cm_s, state_s):
    ci = pl.program_id(1)
    last_chunk = ci == pl.num_programs(1) - 1

    @pl.when(ci == 0)
    def _():
        ext_s[0:8, :] = jnp.zeros((8, CONV_DIM), F32)
        state_s[...] = jnp.zeros_like(state_s)

    ext_s[8:8 + CHUNK, 0:D_INNER] = x_ref[...]
    ext_s[8:8 + CHUNK, D_INNER:D_INNER + BC_WIDTH] = b_ref[...]
    ext_s[8:8 + CHUNK, D_INNER + BC_WIDTH:CONV_DIM] = c_ref[...]

    def conv(lo, hi):
        acc = cb_ref[:, lo:hi] + cw_ref[3:4, lo:hi] * ext_s[8:8 + CHUNK, lo:hi]
        for j in range(CONV_WIDTH - 1):
            acc = acc + cw_ref[j:j + 1, lo:hi] * ext_s[5 + j:5 + j + CHUNK, lo:hi]
        return _silu(acc)

    xs_s[...] = conv(0, D_INNER)
    bm_s[...] = conv(D_INNER, D_INNER + BC_WIDTH)
    cm_s[...] = conv(D_INNER + BC_WIDTH, CONV_DIM)
    tail = ext_s[5 + CHUNK:8 + CHUNK, :]
    conv_ref[0] = tail
    ext_s[5:8, :] = tail

    dt = _softplus(dt_ref[:, 0:LANES] + dtb_ref[...])
    da = dt * (-jnp.exp(alog_ref[...]))
    tri = tri_ref[...]
    acs = sum(jnp.dot(tri, p, preferred_element_type=F32) for p in _split3(da))
    acst_s[...] = acs.T
    acs_pieces = _split3(acs)
    last = acs[CHUNK - 1:CHUNK, :]
    two = lambda a: _split3(a)[:2] + (None,)
    exp2 = exp_ref[...]
    xdt_s[...] = xs_s[...] * _spread(two(dt), exp2)
    eax_s[...] = _spread(two(jnp.exp(acs)), exp2)
    tex_s[...] = _spread(two(jnp.exp(last - acs)), exp2)
    elx_s[...] = _spread(_split3(jnp.broadcast_to(jnp.exp(last), (8, LANES))), exp2)

    row = lax.broadcasted_iota(I32, (CHUNK, CHUNK), 0)
    col = lax.broadcasted_iota(I32, (CHUNK, CHUNK), 1)
    causal = row >= col
    low_half = col < SSM_HEAD_DIM

    def group(g, carry):
        o512 = pl.multiple_of(g * GROUP_WIDTH, GROUP_WIDTH)
        o128 = pl.multiple_of(g * SSM_STATE, SSM_STATE)
        bg = bm_s[:, pl.ds(o128, SSM_STATE)]
        cg16 = cm_s[:, pl.ds(o128, SSM_STATE)].astype(BF16)
        bg16 = bg.astype(BF16)
        cb = lax.dot_general(cg16, bg16, (((1,), (1,)), ((), ())), preferred_element_type=F32)
        sel2 = sel_ref[:, pl.ds(pl.multiple_of(g * HEADS_PER_GROUP * CHUNK, HEADS_PER_GROUP * CHUNK),
                                HEADS_PER_GROUP * CHUNK)]
        colb = _spread(acs_pieces, sel2)
        rows = acst_s[pl.ds(pl.multiple_of(g * HEADS_PER_GROUP, HEADS_PER_GROUP), HEADS_PER_GROUP), :]
        xdt_g = xdt_s[:, pl.ds(o512, GROUP_WIDTH)]
        parts = []
        for jp in range(HEADS_PER_GROUP // 2):
            ms = []
            for j in (2 * jp, 2 * jp + 1):
                diff = colb[:, j * CHUNK:(j + 1) * CHUNK] - rows[j:j + 1, :]
                dec = jnp.exp(jnp.where(causal, diff, -jnp.inf))
                ms.append((cb * dec).astype(BF16))
            xp = xdt_g[:, jp * LANES:(jp + 1) * LANES]
            rhs = jnp.concatenate([jnp.where(low_half, xp, 0.0), jnp.where(low_half, 0.0, xp)], axis=0)
            parts.append(jnp.dot(jnp.concatenate(ms, axis=1), rhs.astype(BF16), preferred_element_type=F32))
        y = jnp.concatenate(parts, axis=1)

        st = state_s[:, pl.ds(o512, GROUP_WIDTH)]
        y = y + jnp.dot(cg16, st.astype(BF16), preferred_element_type=F32) * eax_s[:, pl.ds(o512, GROUP_WIDTH)]
        xw = (xdt_g * tex_s[:, pl.ds(o512, GROUP_WIDTH)]).astype(BF16)
        st_new = (st * elx_s[0:1, pl.ds(o512, GROUP_WIDTH)]
                  + jnp.dot(bg.T.astype(BF16), xw, preferred_element_type=F32))
        state_s[:, pl.ds(o512, GROUP_WIDTH)] = st_new

        @pl.when(last_chunk)
        def _():
            st_ref[0, pl.ds(o512, GROUP_WIDTH), :] = st_new.T

        y = y + dskip_ref[:, pl.ds(o512, GROUP_WIDTH)] * xs_s[:, pl.ds(o512, GROUP_WIDTH)]
        u = y * _silu(z_ref[:, pl.ds(o512, GROUP_WIDTH)])
        u = u * lax.rsqrt(jnp.mean(u * u, axis=-1, keepdims=True) + EPS)
        y_ref[:, pl.ds(o512, GROUP_WIDTH)] = (u * gn_ref[:, pl.ds(o512, GROUP_WIDTH)]).astype(BF16)
        return carry

    lax.fori_loop(0, SSM_GROUPS, group, 0, unroll=4)


def _pad_heads(v):
    return jnp.pad(v.astype(F32), (0, LANES - SSM_HEADS)).reshape(1, LANES)


def _head_expand(dtype=BF16, copies=2):
    h = jnp.arange(LANES)[:, None]
    ch = jnp.arange(D_INNER)[None, :] // SSM_HEAD_DIM
    return jnp.tile((h == ch).astype(dtype), (copies, 1))


def _head_select():
    h = jnp.arange(LANES)[:, None]
    blk = jnp.arange(SSM_HEADS * CHUNK)[None, :] // CHUNK
    return jnp.tile((h == blk).astype(BF16), (2, 1))


def _ssd_prompt(proj, bsz, seq_len, lw):
    nc = seq_len // CHUNK
    m = bsz * seq_len
    tri = (jnp.arange(CHUNK)[:, None] >= jnp.arange(CHUNK)[None, :]).astype(BF16)
    const = lambda shape: pl.BlockSpec(shape, lambda b, c: (0, 0))
    rowblk = lambda width, col: pl.BlockSpec((CHUNK, width), lambda b, c: (b * nc + c, col // width))
    y, st, conv = pl.pallas_call(
        _ssd_kernel,
        out_shape=(jax.ShapeDtypeStruct((m, D_INNER), BF16),
                   jax.ShapeDtypeStruct((bsz, D_INNER, SSM_STATE), F32),
                   jax.ShapeDtypeStruct((bsz, CONV_WIDTH - 1, CONV_DIM), F32)),
        grid=(bsz, nc),
        in_specs=[rowblk(D_INNER, PC_Z), rowblk(D_INNER, PC_X), rowblk(BC_WIDTH, PC_B), rowblk(BC_WIDTH, PC_C),
                  rowblk(DT_PAD, PC_DT),
                  const((CONV_WIDTH, CONV_DIM)), const((1, CONV_DIM)), const((1, LANES)), const((1, LANES)),
                  const((1, D_INNER)), const((1, D_INNER)), const((2 * LANES, D_INNER)),
                  const((2 * LANES, SSM_HEADS * CHUNK)), const((CHUNK, CHUNK))],
        out_specs=(pl.BlockSpec((CHUNK, D_INNER), lambda b, c: (b * nc + c, 0)),
                   pl.BlockSpec((1, D_INNER, SSM_STATE), lambda b, c: (b, 0, 0)),
                   pl.BlockSpec((1, CONV_WIDTH - 1, CONV_DIM), lambda b, c: (b, 0, 0))),
        scratch_shapes=[pltpu.VMEM((8 + CHUNK, CONV_DIM), F32),
                        pltpu.VMEM((CHUNK, D_INNER), F32), pltpu.VMEM((CHUNK, D_INNER), F32),
                        pltpu.VMEM((CHUNK, D_INNER), F32), pltpu.VMEM((CHUNK, D_INNER), F32),
                        pltpu.VMEM((8, D_INNER), F32), pltpu.VMEM((LANES, CHUNK), F32),
                        pltpu.VMEM((CHUNK, BC_WIDTH), F32), pltpu.VMEM((CHUNK, BC_WIDTH), F32),
                        pltpu.VMEM((SSM_STATE, D_INNER), F32)],
        compiler_params=_cparams(("parallel", "arbitrary")),
        name="ssd_chunks",
    )(proj, proj, proj, proj, proj,
      lw['w_conv'], lw['b_conv'].reshape(1, CONV_DIM), _pad_heads(lw['dt_bias']), _pad_heads(lw['a_log']),
      jnp.repeat(lw['d_skip'].astype(F32), SSM_HEAD_DIM).reshape(1, D_INNER),
      lw['g_ssm_norm'].reshape(1, D_INNER), _head_expand(), _head_select(), tri)
    return y, st.reshape(bsz, SSM_HEADS, SSM_HEAD_DIM, SSM_STATE), conv


def _ssd_step_pre_kernel(x_ref, b_ref, c_ref, dt_ref, p0_ref, p1_ref, p2_ref,
                         cw_ref, cb_ref, dtb_ref, alog_ref, exp_ref,
                         xs_ref, xdt_ref, bm_ref, cm_ref, dec_ref, conv_ref):
    def conv(u_ref, lo, hi):
        acc = cb_ref[:, lo:hi] + cw_ref[3:4, lo:hi] * u_ref[...]
        for j, p_ref in enumerate((p0_ref, p1_ref, p2_ref)):
            acc = acc + cw_ref[j:j + 1, lo:hi] * p_ref[:, lo:hi]
        return _silu(acc)

    xs = conv(x_ref, 0, D_INNER)
    xs_ref[...] = xs
    bm_ref[...] = conv(b_ref, D_INNER, D_INNER + BC_WIDTH)
    cm_ref[...] = conv(c_ref, D_INNER + BC_WIDTH, CONV_DIM)
    dt = _softplus(dt_ref[:, 0:LANES] + dtb_ref[...])
    dec_ref[...] = jnp.exp(dt * (-jnp.exp(alog_ref[...])))
    xdt_ref[...] = xs * jnp.dot(dt, exp_ref[...], precision=HIGHEST, preferred_element_type=F32)
    conv_ref[0] = p1_ref[...]
    conv_ref[1] = p2_ref[...]
    conv_ref[2, :, 0:D_INNER] = x_ref[...]
    conv_ref[2, :, D_INNER:D_INNER + BC_WIDTH] = b_ref[...]
    conv_ref[2, :, D_INNER + BC_WIDTH:CONV_DIM] = c_ref[...]


def _ssd_step_kernel(dec_ref, st_ref, xdtt_ref, bm_ref, cmt_ref, xs_ref, z_ref, dskip_ref, gn_ref,
                     sto_ref, y_ref, yt_s, *, tb):
    i = pl.program_id(0)
    nb = xs_ref.shape[0]
    sub = lax.broadcasted_iota(I32, (nb, SSM_STATE), 0)
    lane = lax.broadcasted_iota(I32, (SSM_STATE, nb), 1)

    @pl.when(i == 0)
    def _():
        yt_s[...] = jnp.zeros_like(yt_s)

    for bi in range(tb):
        b = i * tb + bi
        for g in range(SSM_GROUPS):
            r0 = g * GROUP_WIDTH
            b_row = jnp.where(sub == b, bm_ref[:, g * SSM_STATE:(g + 1) * SSM_STATE], 0.0).astype(BF16)
            term = jnp.dot(xdtt_ref[r0:r0 + GROUP_WIDTH, :].astype(BF16), b_row, preferred_element_type=F32)
            news = []
            for j in range(HEADS_PER_GROUP):
                h = g * HEADS_PER_GROUP + j
                rr = r0 + j * SSM_HEAD_DIM
                new = (st_ref[bi, rr:rr + SSM_HEAD_DIM, :] * dec_ref[b, h]
                       + term[j * SSM_HEAD_DIM:(j + 1) * SSM_HEAD_DIM, :])
                sto_ref[bi, rr:rr + SSM_HEAD_DIM, :] = new
                news.append(new)
            new_g = jnp.concatenate(news, axis=0).astype(BF16)
            c_col = jnp.where(lane == b, cmt_ref[g], 0.0).astype(BF16)
            yt_s[r0:r0 + GROUP_WIDTH, :] += jnp.dot(new_g, c_col, preferred_element_type=F32)

    @pl.when(i == pl.num_programs(0) - 1)
    def _():
        for g in range(SSM_GROUPS):
            sl = slice(g * GROUP_WIDTH, (g + 1) * GROUP_WIDTH)
            y = yt_s[sl, :].T + dskip_ref[:, sl] * xs_ref[:, sl]
            u = y * _silu(z_ref[:, sl])
            u = u * lax.rsqrt(jnp.mean(u * u, axis=-1, keepdims=True) + EPS)
            y_ref[:, sl] = (u * gn_ref[:, sl]).astype(BF16)


def _ssd_sample(proj, state_ssm, state_conv, lw):
    nb = proj.shape[0]
    full = lambda shape: pl.BlockSpec(shape, lambda *_: tuple(0 for _ in shape))
    colblk = lambda width, col: pl.BlockSpec((nb, width), lambda *_: (0, col // width))
    prev = [state_conv[:, j, :] for j in range(CONV_WIDTH - 1)]
    tr = min(32, nb)
    rows = lambda width, col=0: pl.BlockSpec((tr, width), lambda i: (i, col // width))
    xs, xdt, bm, cm, dec, conv = pl.pallas_call(
        _ssd_step_pre_kernel,
        out_shape=(jax.ShapeDtypeStruct((nb, D_INNER), F32), jax.ShapeDtypeStruct((nb, D_INNER), F32),
                   jax.ShapeDtypeStruct((nb, BC_WIDTH), F32), jax.ShapeDtypeStruct((nb, BC_WIDTH), F32),
                   jax.ShapeDtypeStruct((nb, LANES), F32),
                   jax.ShapeDtypeStruct((CONV_WIDTH - 1, nb, CONV_DIM), F32)),
        grid=(nb // tr,),
        in_specs=[rows(D_INNER, PC_X), rows(BC_WIDTH, PC_B), rows(BC_WIDTH, PC_C), rows(DT_PAD, PC_DT),
                  rows(CONV_DIM), rows(CONV_DIM), rows(CONV_DIM),
                  full((CONV_WIDTH, CONV_DIM)), full((1, CONV_DIM)), full((1, LANES)), full((1, LANES)),
                  full((LANES, D_INNER))],
        out_specs=(rows(D_INNER), rows(D_INNER), rows(BC_WIDTH), rows(BC_WIDTH), rows(LANES),
                   pl.BlockSpec((CONV_WIDTH - 1, tr, CONV_DIM), lambda i: (0, i, 0))),
        compiler_params=_cparams(("parallel",)),
        name="ssd_step_pre",
    )(proj, proj, proj, proj, *prev,
      lw['w_conv'], lw['b_conv'].reshape(1, CONV_DIM), _pad_heads(lw['dt_bias']), _pad_heads(lw['a_log']),
      _head_expand(F32, 1))

    tb = 2
    cmt = cm.reshape(nb, SSM_GROUPS, SSM_STATE).transpose(1, 2, 0)
    st_new, y = pl.pallas_call(
        functools.partial(_ssd_step_kernel, tb=tb),
        out_shape=(jax.ShapeDtypeStruct((nb, D_INNER, SSM_STATE), F32),
                   jax.ShapeDtypeStruct((nb, D_INNER), BF16)),
        grid=(nb // tb,),
        in_specs=[pl.BlockSpec(memory_space=pltpu.SMEM),
                  pl.BlockSpec((tb, D_INNER, SSM_STATE), lambda i: (i, 0, 0)),
                  full((D_INNER, nb)), full((nb, BC_WIDTH)), full((SSM_GROUPS, SSM_STATE, nb)),
                  full((nb, D_INNER)), colblk(D_INNER, PC_Z), full((1, D_INNER)), full((1, D_INNER))],
        out_specs=(pl.BlockSpec((tb, D_INNER, SSM_STATE), lambda i: (i, 0, 0)),
                   full((nb, D_INNER))),
        scratch_shapes=[pltpu.VMEM((D_INNER, nb), F32)],
        compiler_params=_cparams(("arbitrary",)),
        name="ssd_step",
    )(dec, state_ssm.reshape(nb, D_INNER, SSM_STATE), xdt.T, bm, cmt, xs, proj,
      jnp.repeat(lw['d_skip'].astype(F32), SSM_HEAD_DIM).reshape(1, D_INNER),
      lw['g_ssm_norm'].reshape(1, D_INNER))
    return (y, st_new.reshape(nb, SSM_HEADS, SSM_HEAD_DIM, SSM_STATE), conv.transpose(1, 0, 2))


def _rope_tables(pos):
    half = ROPE_DIM // 2
    inv_freq = ROPE_THETA ** (-jnp.arange(half, dtype=F32) * 2.0 / ROPE_DIM)
    ang = pos.astype(F32)[:, None] * inv_freq[None, :]
    cos, sin = jnp.cos(ang), jnp.sin(ang)
    n = pos.shape[0]
    ones = jnp.ones((n, HEAD_DIM - ROPE_DIM), F32)
    zeros = jnp.zeros((n, HEAD_DIM - ROPE_DIM), F32)
    zh = jnp.zeros((n, half), F32)
    cos_f = jnp.concatenate([cos, cos, ones], axis=1)
    sin_a = jnp.concatenate([zh, sin, zeros], axis=1)
    sin_b = jnp.concatenate([-sin, zh, zeros], axis=1)
    tile = lambda t: jnp.tile(t, (1, LANES // HEAD_DIM))
    return tile(cos_f), tile(sin_a), tile(sin_b)


def _rope_slab(x, cos_f, sin_a, sin_b):
    half = ROPE_DIM // 2
    return x * cos_f + pltpu.roll(x, half, 1) * sin_a + pltpu.roll(x, LANES - half, 1) * sin_b


def _attn_kernel(sink_ref, q_ref, k_ref, v_ref, cos_ref, sin_a_ref, sin_b_ref,
                 y_ref, kn_ref, vn_ref, kspan_s, vspan_s, sc_s, p_s):
    bi = pl.program_id(1)
    nq = q_ref.shape[0]

    @pl.when(bi == 0)
    def _():
        kspan_s[0:nq, :] = jnp.zeros((nq, KV_WIDTH), F32)
        vspan_s[0:nq, :] = jnp.zeros((nq, KV_WIDTH), F32)

    cos_f, sin_a, sin_b = cos_ref[...], sin_a_ref[...], sin_b_ref[...]
    rope = lambda x: _rope_slab(x, cos_f, sin_a, sin_b)
    kr = jnp.concatenate([rope(k_ref[:, s * LANES:(s + 1) * LANES]) for s in range(KV_WIDTH // LANES)], axis=1)
    v = v_ref[...]
    kspan_s[nq:2 * nq, :] = kr
    vspan_s[nq:2 * nq, :] = v
    kn_ref[0] = kr
    vn_ref[0] = v

    t = lax.broadcasted_iota(I32, (nq, nq), 0)
    j = lax.broadcasted_iota(I32, (nq, nq), 1)
    own_block = j <= t
    low = lax.broadcasted_iota(I32, (2 * nq, LANES), 1) < HEAD_DIM
    scale = HEAD_DIM ** -0.5
    grp = ATTN_HEADS // KV_HEADS

    def block_diag(span_ref, kh):
        nat = span_ref[:, (kh // 2) * LANES:(kh // 2 + 1) * LANES]
        swp = pltpu.roll(nat, HEAD_DIM, 1)
        lo, hi = (nat, swp) if kh % 2 == 0 else (swp, nat)
        return jnp.concatenate([jnp.where(low, lo, 0.0), jnp.where(low, 0.0, hi)], axis=0).astype(BF16)

    for kh in range(KV_HEADS):
        kbd = block_diag(kspan_s, kh)
        for p in range(grp // 2):
            h0 = kh * grp + 2 * p
            qp = (rope(q_ref[:, h0 * HEAD_DIM:(h0 + 2) * HEAD_DIM]) * scale).astype(BF16)
            sc = lax.dot_general(qp, kbd, (((1,), (1,)), ((), ())), preferred_element_type=F32)
            for a in range(2):
                prev = sc[:, 2 * a * nq:(2 * a + 1) * nq]
                cur = sc[:, (2 * a + 1) * nq:(2 * a + 2) * nq]
                sc_s[h0 + a] = jnp.where(own_block, cur, prev)

    have_prev = bi > 0
    for h in range(ATTN_HEADS):
        sa = sc_s[h]
        sa = jnp.where(own_block | have_prev, sa, -jnp.inf)
        sink = sink_ref[h]
        mx = jnp.maximum(jnp.max(sa, axis=-1, keepdims=True), sink)
        pa = jnp.exp(sa - mx)
        pa = pa / (jnp.sum(pa, axis=-1, keepdims=True) + jnp.exp(sink - mx))
        p_s[h] = pa.astype(BF16)

    zero = jnp.zeros((nq, nq), BF16)
    for kh in range(KV_HEADS):
        vbd = block_diag(vspan_s, kh)
        for p in range(grp // 2):
            h0 = kh * grp + 2 * p
            parts = []
            for a in range(2):
                pa = p_s[h0 + a]
                parts += [jnp.where(own_block, zero, pa), jnp.where(own_block, pa, zero)]
            y_ref[:, h0 * HEAD_DIM:(h0 + 2) * HEAD_DIM] = jnp.dot(
                jnp.concatenate(parts, axis=1), vbd, preferred_element_type=F32).astype(BF16)

    kspan_s[0:nq, :] = kspan_s[nq:2 * nq, :]
    vspan_s[0:nq, :] = vspan_s[nq:2 * nq, :]


def _attn_prompt(proj, bsz, seq_len, sinks):
    nq = CHUNK
    nb = seq_len // nq
    m = bsz * seq_len
    cos_f, sin_a, sin_b = _rope_tables(jnp.arange(seq_len))
    rowblk = lambda width, col: pl.BlockSpec((nq, width), lambda b, i: (b * nb + i, col // width))
    tab = pl.BlockSpec((nq, LANES), lambda b, i: (i, 0))
    y, kn, vn = pl.pallas_call(
        _attn_kernel,
        out_shape=(jax.ShapeDtypeStruct((m, D_MODEL), BF16),
                   jax.ShapeDtypeStruct((bsz, WINDOW, KV_WIDTH), F32),
                   jax.ShapeDtypeStruct((bsz, WINDOW, KV_WIDTH), F32)),
        grid=(bsz, nb),
        in_specs=[pl.BlockSpec(memory_space=pltpu.SMEM),
                  rowblk(D_MODEL, PC_Q), rowblk(KV_WIDTH, PC_K), rowblk(KV_WIDTH, PC_V), tab, tab, tab],
        out_specs=(pl.BlockSpec((nq, D_MODEL), lambda b, i: (b * nb + i, 0)),
                   pl.BlockSpec((1, WINDOW, KV_WIDTH), lambda b, i: (b, 0, 0)),
                   pl.BlockSpec((1, WINDOW, KV_WIDTH), lambda b, i: (b, 0, 0))),
        scratch_shapes=[pltpu.VMEM((2 * nq, KV_WIDTH), F32), pltpu.VMEM((2 * nq, KV_WIDTH), F32),
                        pltpu.VMEM((ATTN_HEADS, nq, nq), F32), pltpu.VMEM((ATTN_HEADS, nq, nq), BF16)],
        compiler_params=_cparams(("parallel", "arbitrary")),
        name="attn_blocks",
    )(sinks.astype(F32), proj, proj, proj, cos_f, sin_a, sin_b)
    shape = (bsz, WINDOW, KV_HEADS, HEAD_DIM)
    return y, kn.reshape(shape), vn.reshape(shape)


def _attn_step_kernel(sink_ref, q_ref, k_ref, v_ref, cos_ref, sin_a_ref, sin_b_ref, fold_ref, foldt_ref,
                      kc_ref, vc_ref, y_ref, ko_ref, vo_ref, *, tb):
    cos_f, sin_a, sin_b = cos_ref[...], sin_a_ref[...], sin_b_ref[...]
    rope = lambda ref, w: jnp.concatenate(
        [_rope_slab(ref[:, s * LANES:(s + 1) * LANES], cos_f, sin_a, sin_b) for s in range(w // LANES)], axis=1)
    qr = rope(q_ref, D_MODEL) * (HEAD_DIM ** -0.5)
    kr = rope(k_ref, KV_WIDTH)
    v = v_ref[...]

    own = (lax.broadcasted_iota(I32, (ATTN_HEADS, D_MODEL), 1) // HEAD_DIM
           == lax.broadcasted_iota(I32, (ATTN_HEADS, D_MODEL), 0))
    qb = jnp.concatenate([jnp.where(own, jnp.broadcast_to(qr[bi:bi + 1], (ATTN_HEADS, D_MODEL)), 0.0)
                          for bi in range(tb)], axis=0).astype(BF16)
    qm = jnp.dot(qb, fold_ref[...], preferred_element_type=F32).astype(BF16)

    row = lax.broadcasted_iota(I32, (WINDOW, KV_WIDTH), 0)
    w = lax.broadcasted_iota(I32, (ATTN_HEADS, WINDOW), 1)
    sink = sink_ref[...]
    bf = lambda x: x.astype(BF16)
    outs = []
    for bi in range(tb):
        qmb = qm[bi * ATTN_HEADS:(bi + 1) * ATTN_HEADS]
        kc, vc = kc_ref[bi], vc_ref[bi]
        kn, vn = kr[bi:bi + 1], v[bi:bi + 1]
        sc = lax.dot_general(qmb, bf(kc), (((1,), (1,)), ((), ())), preferred_element_type=F32)
        sc = jnp.where(w >= 1, sc, -jnp.inf)
        s_new = jnp.sum(qmb.astype(F32) * bf(kn).astype(F32), axis=-1, keepdims=True)
        mx = jnp.maximum(jnp.maximum(jnp.max(sc, axis=-1, keepdims=True), s_new), sink)
        p = jnp.exp(sc - mx)
        p_new = jnp.exp(s_new - mx)
        den = jnp.sum(p, axis=-1, keepdims=True) + p_new + jnp.exp(sink - mx)
        p = p / den
        p_new = p_new / den
        out = jnp.dot(bf(p), bf(vc), preferred_element_type=F32) + bf(p_new).astype(F32) * bf(vn).astype(F32)
        outs.append(bf(out))
        ko_ref[bi] = jnp.where(row == WINDOW - 1, kn, pltpu.roll(kc, WINDOW - 1, 0))
        vo_ref[bi] = jnp.where(row == WINDOW - 1, vn, pltpu.roll(vc, WINDOW - 1, 0))

    of = jnp.dot(jnp.concatenate(outs, axis=0), foldt_ref[...], preferred_element_type=F32)
    of = of.reshape(tb, ATTN_HEADS, D_MODEL)
    y_ref[...] = jnp.sum(jnp.where(own[None], of, 0.0), axis=1).astype(BF16)


def _attn_sample(proj, cache_k, cache_v, sinks):
    nb = proj.shape[0]
    tb = 8
    grp = ATTN_HEADS // KV_HEADS
    cos_f, sin_a, sin_b = _rope_tables(jnp.full((1,), PAST_LEN))
    c = jnp.arange(D_MODEL)[:, None]
    l = jnp.arange(KV_WIDTH)[None, :]
    fold = ((c % HEAD_DIM == l % HEAD_DIM) & ((c // HEAD_DIM) // grp == l // HEAD_DIM)).astype(BF16)
    full = lambda shape: pl.BlockSpec(shape, lambda i: tuple(0 for _ in shape))
    rowblk = lambda width, col: pl.BlockSpec((tb, width), lambda i: (i, col // width))
    win = pl.BlockSpec((tb, WINDOW, KV_WIDTH), lambda i: (i, 0, 0))
    y, kn, vn = pl.pallas_call(
        functools.partial(_attn_step_kernel, tb=tb),
        out_shape=(jax.ShapeDtypeStruct((nb, D_MODEL), BF16),
                   jax.ShapeDtypeStruct((nb, WINDOW, KV_WIDTH), F32),
                   jax.ShapeDtypeStruct((nb, WINDOW, KV_WIDTH), F32)),
        grid=(nb // tb,),
        in_specs=[full((ATTN_HEADS, 1)), rowblk(D_MODEL, PC_Q), rowblk(KV_WIDTH, PC_K), rowblk(KV_WIDTH, PC_V),
                  full((1, LANES)), full((1, LANES)), full((1, LANES)),
                  full((D_MODEL, KV_WIDTH)), full((KV_WIDTH, D_MODEL)), win, win],
        out_specs=(pl.BlockSpec((tb, D_MODEL), lambda i: (i, 0)), win, win),
        compiler_params=_cparams(("parallel",)),
        name="attn_step",
    )(sinks.astype(F32).reshape(ATTN_HEADS, 1), proj, proj, proj, cos_f, sin_a, sin_b, fold, fold.T,
      cache_k.reshape(nb, WINDOW, KV_WIDTH), cache_v.reshape(nb, WINDOW, KV_WIDTH))
    shape = (nb, WINDOW, KV_HEADS, HEAD_DIM)
    return y, kn.reshape(shape), vn.reshape(shape)


def _merge_kernel(ys_ref, ya_ref, ws_ref, wa_ref, gs_ref, ga_ref, o_ref):
    a = jnp.dot(ys_ref[...], ws_ref[...], preferred_element_type=F32)
    b = jnp.dot(ya_ref[...], wa_ref[...], preferred_element_type=F32)
    o_ref[...] = (jax.nn.sigmoid(gs_ref[...]) * a + jax.nn.sigmoid(ga_ref[...]) * b).astype(BF16)


def _merge(y_ssd, y_att, proj, w_ssd, w_att):
    m = y_ssd.shape[0]
    tm = min(512, m)
    tn = 512
    g0 = PC_G // tn
    return pl.pallas_call(
        _merge_kernel,
        out_shape=jax.ShapeDtypeStruct((m, D_MODEL), BF16),
        grid=(m // tm, D_MODEL // tn),
        in_specs=[pl.BlockSpec((tm, D_INNER), lambda i, j: (i, 0)),
                  pl.BlockSpec((tm, D_MODEL), lambda i, j: (i, 0)),
                  pl.BlockSpec((D_INNER, tn), lambda i, j: (0, j)),
                  pl.BlockSpec((D_MODEL, tn), lambda i, j: (0, j)),
                  pl.BlockSpec((tm, tn), lambda i, j: (i, g0 + j)),
                  pl.BlockSpec((tm, tn), lambda i, j: (i, g0 + D_MODEL // tn + j))],
        out_specs=pl.BlockSpec((tm, tn), lambda i, j: (i, j)),
        compiler_params=_cparams(("parallel", "arbitrary")),
        name="branch_merge",
    )(y_ssd, y_att, w_ssd, w_att, proj, proj)


def _outproj_kernel(pre_ref, x_ref, gate_ref, sh_ref, sc_ref, g_ref, w_ref, wr_ref, br_ref, before_ref, c0_ref,
                    x1_ref, h2_ref, tg_ref, ti_ref, tr_ref, cnt_ref, cnt_s):
    @pl.when(pl.program_id(0) == 0)
    def _():
        cnt_s[...] = c0_ref[...]

    mixed = jnp.dot(pre_ref[...], w_ref[...], preferred_element_type=F32)
    x1 = x_ref[...] + gate_ref[0] * mixed
    x1_ref[...] = x1
    y = x1 * lax.rsqrt(jnp.mean(x1 * x1, axis=-1, keepdims=True) + EPS) * g_ref[...]
    h2 = y * (1.0 + sc_ref[0]) + sh_ref[0]
    h2_ref[...] = h2
    h_hi, h_mid, _ = _split3(h2)
    hw = jnp.dot(h_hi, wr_ref[...], preferred_element_type=F32)
    logits = (hw[:, 0:LANES] + hw[:, LANES:2 * LANES]
              + jnp.dot(h_mid, wr_ref[:, 0:LANES], preferred_element_type=F32) + br_ref[...])
    lane = lax.broadcasted_iota(I32, logits.shape, 1)
    lane_f = lane.astype(F32)
    vals = jnp.zeros_like(logits)
    idxs = jnp.zeros_like(logits)
    top = None
    den = jnp.zeros((logits.shape[0], 1), F32)
    chosen = []
    for k in range(TOP_K):
        mx = jnp.max(logits, axis=-1, keepdims=True)
        ix = jnp.min(jnp.where(logits == mx, lane_f, float(LANES)), axis=-1, keepdims=True)
        top = mx if top is None else top
        e = jnp.exp(mx - top)
        den = den + e
        vals = jnp.where(lane == k, e, vals)
        idxs = jnp.where(lane == k, ix, idxs)
        chosen.append(lane_f == ix)
        logits = jnp.where(chosen[-1], -jnp.inf, logits)
    tg_ref[...] = vals / den
    ti_ref[...] = idxs.astype(I32)

    picked = jnp.zeros_like(vals)
    for c in chosen:
        picked = jnp.where(c, 1.0, picked)
    ahead = jnp.dot(before_ref[...], picked.astype(BF16), preferred_element_type=F32) + cnt_s[...]
    ranks = jnp.zeros_like(vals)
    for k, c in enumerate(chosen):
        ranks = jnp.where(lane == k, jnp.sum(jnp.where(c, ahead, 0.0), axis=-1, keepdims=True), ranks)
    tr_ref[...] = ranks.astype(I32)
    cnt_s[...] += jnp.sum(picked, axis=0, keepdims=True)
    cnt_ref[...] = cnt_s[...]


def _outproj(pre, x, gate1, shift2, scale2, g2, w_out, w_router, b_router, seq_len, counts0):
    m = x.shape[0]
    tm = _row_tile(512, m, seq_len)
    before = (jnp.arange(tm)[None, :] < jnp.arange(tm)[:, None]).astype(BF16)
    w_hi, w_mid, _ = _split3(jnp.pad(w_router.astype(F32), ((0, 0), (0, LANES - N_EXPERTS))))
    wr = jnp.concatenate([w_hi, w_mid], axis=1)
    br = jnp.pad(b_router.astype(F32), (0, LANES - N_EXPERTS), constant_values=-jnp.inf).reshape(1, LANES)
    row = lambda width: pl.BlockSpec((tm, width), lambda i: (i, 0))
    const = lambda shape: pl.BlockSpec(shape, lambda i: (0, 0))
    return pl.pallas_call(
        _outproj_kernel,
        out_shape=(jax.ShapeDtypeStruct((m, D_MODEL), F32), jax.ShapeDtypeStruct((m, D_MODEL), F32),
                   jax.ShapeDtypeStruct((m, LANES), F32), jax.ShapeDtypeStruct((m, LANES), I32),
                   jax.ShapeDtypeStruct((m, LANES), I32), jax.ShapeDtypeStruct((1, LANES), F32)),
        grid=(m // tm,),
        in_specs=[row(D_MODEL), row(D_MODEL), _mod_spec(tm, seq_len), _mod_spec(tm, seq_len), _mod_spec(tm, seq_len),
                  const((1, D_MODEL)), const((D_MODEL, D_MODEL)), const((D_MODEL, 2 * LANES)), const((1, LANES)),
                  const((tm, tm)), const((1, LANES))],
        out_specs=(row(D_MODEL), row(D_MODEL), row(LANES), row(LANES), row(LANES), const((1, LANES))),
        scratch_shapes=[pltpu.VMEM((1, LANES), F32)],
        compiler_params=_cparams(("arbitrary",)),
        name="out_proj_router",
    )(pre, x, _mod_arr(gate1, seq_len), _mod_arr(shift2, seq_len), _mod_arr(scale2, seq_len),
      g2.reshape(1, D_MODEL), w_out, wr, br, before, counts0)


def _ffn_kernel(be_ref, na_ref, tok_hbm, h_hbm, w1g_ref, w1l_ref, b1g_ref, b1l_ref, w2_ref, b2_ref,
                o_ref, tok_s, rows_s, xb_s, acc_s, tok_sem, row_sem):
    b = pl.program_id(0)
    f = pl.program_id(1)
    n_blocks = pl.num_programs(0)
    nf = D_FF // MOE_TF
    rows_per_step = MOE_BLK // nf
    n_active = na_ref[0]
    active = b < n_active
    nxt = jnp.minimum(b + 1, n_blocks - 1)
    nxt_slot = (b + 1) % 2

    def tok_copy(blk, sl):
        return pltpu.make_async_copy(tok_hbm.at[blk], tok_s.at[sl], tok_sem.at[sl])

    def row_copy(tok, r):
        return pltpu.make_async_copy(h_hbm.at[pl.ds(tok, 1)], rows_s.at[pl.ds(r, 1)], row_sem)

    def wait_rows():
        for r in range(MOE_BLK):
            row_copy(0, r).wait()

    @pl.when(active & (f == 0))
    def _():
        @pl.when(b == 0)
        def _():
            first = tok_copy(0, 0)
            first.start()
            first.wait()

            def body(r, c):
                row_copy(tok_s[0, r], r).start()
                return c
            lax.fori_loop(0, MOE_BLK, body, 0, unroll=8)
            tok_copy(nxt, 1).start()

        wait_rows()
        xb_s[...] = rows_s[...].astype(BF16)
        tok_copy(nxt, nxt_slot).wait()

    @pl.when(active)
    def _():
        base = f * rows_per_step
        for i in range(rows_per_step):
            row_copy(tok_s[nxt_slot, base + i], base + i).start()
        xb = xb_s[...]
        glu = jnp.dot(xb, w1g_ref[0], preferred_element_type=F32) + b1g_ref[0]
        lin = jnp.dot(xb, w1l_ref[0], preferred_element_type=F32) + b1l_ref[0]
        glu = jnp.minimum(glu, SWIGLU_LIMIT)
        lin = jnp.clip(lin, -SWIGLU_LIMIT, SWIGLU_LIMIT)
        act = glu * jax.nn.sigmoid(SWIGLU_ALPHA * glu) * (lin + 1.0)
        part = jnp.dot(act.astype(BF16), w2_ref[0], preferred_element_type=F32)

        @pl.when(f == 0)
        def _():
            acc_s[...] = part + b2_ref[0]

        @pl.when((f > 0) & (f < nf - 1))
        def _():
            acc_s[...] += part

        @pl.when(f == nf - 1)
        def _():
            o_ref[...] = acc_s[...] + part
            tok_copy(jnp.minimum(b + 2, n_blocks - 1), b % 2).start()

    @pl.when(((b == n_active) & (f == 0)) | (active & (b == n_blocks - 1) & (f == nf - 1)))
    def _():
        wait_rows()
        tok_copy(0, jnp.where(active, b, b + 1) % 2).wait()

    @pl.when((b >= n_active) & (f == nf - 1))
    def _():
        o_ref[...] = jnp.zeros_like(o_ref)


def _ffn(h_all, slot_tok, block_expert, n_active, w1, b1, w2, b2):
    n_blocks = slot_tok.shape[0]
    nf = D_FF // MOE_TF

    def widx(col_off):
        def index_map(b, f, be, na):
            live = b < na[0]
            bb = jnp.where(live, b, na[0] - 1)
            ff = jnp.where(live, f, nf - 1)
            return be[bb], 0, col_off + ff
        return index_map

    def w2idx(b, f, be, na):
        live = b < na[0]
        return be[jnp.where(live, b, na[0] - 1)], jnp.where(live, f, nf - 1), 0

    def b2idx(b, f, be, na):
        return be[jnp.where(b < na[0], b, na[0] - 1)], 0, 0

    grid_spec = pltpu.PrefetchScalarGridSpec(
        num_scalar_prefetch=2,
        grid=(n_blocks, nf),
        in_specs=[pl.BlockSpec(memory_space=pl.ANY), pl.BlockSpec(memory_space=pl.ANY),
                  pl.BlockSpec((1, D_MODEL, MOE_TF), widx(0)), pl.BlockSpec((1, D_MODEL, MOE_TF), widx(nf)),
                  pl.BlockSpec((1, 1, MOE_TF), widx(0)), pl.BlockSpec((1, 1, MOE_TF), widx(nf)),
                  pl.BlockSpec((1, MOE_TF, D_MODEL), w2idx), pl.BlockSpec((1, 1, D_MODEL), b2idx)],
        out_specs=pl.BlockSpec((MOE_BLK, D_MODEL), lambda b, f, be, na: (b, 0)),
        scratch_shapes=[pltpu.SMEM((2, MOE_BLK), I32),
                        pltpu.VMEM((MOE_BLK, D_MODEL), F32),
                        pltpu.VMEM((MOE_BLK, D_MODEL), BF16),
                        pltpu.VMEM((MOE_BLK, D_MODEL), F32),
                        pltpu.SemaphoreType.DMA((2,)), pltpu.SemaphoreType.DMA])
    return pl.pallas_call(
        _ffn_kernel,
        out_shape=jax.ShapeDtypeStruct((n_blocks * MOE_BLK, D_MODEL), F32),
        grid_spec=grid_spec,
        compiler_params=_cparams(("arbitrary", "arbitrary")),
        name="expert_ffn",
    )(block_expert, n_active, slot_tok, h_all, w1, w1, b1[:, None, :], b1[:, None, :], w2, b2[:, None, :])


def _route(top_idx, rank, counts):
    n_tok = top_idx.shape[0]
    n_assign = n_tok * TOP_K
    e_flat = top_idx.reshape(-1)
    rank = rank.reshape(-1)
    onehot = (e_flat[:, None] == jnp.arange(N_EXPERTS)[None, :]).astype(I32)
    nblk = (counts + MOE_BLK - 1) // MOE_BLK
    blk_end = jnp.cumsum(nblk)
    pad_start = (blk_end - nblk) * MOE_BLK
    dest = jnp.sum(onehot * pad_start[None, :], axis=1) + rank
    n_blocks = -(-n_assign // MOE_BLK) + N_EXPERTS
    slot_tok = jnp.zeros((n_blocks * MOE_BLK,), I32).at[dest].set(
        jnp.arange(n_assign, dtype=I32) // TOP_K, unique_indices=True)
    block_expert = jnp.minimum(jnp.searchsorted(blk_end, jnp.arange(n_blocks), side='right'),
                               N_EXPERTS - 1).astype(I32)
    n_active = blk_end[-1:].astype(I32)
    return dest.reshape(n_tok, TOP_K).astype(I32), slot_tok.reshape(n_blocks, MOE_BLK), block_expert, n_active


def _combine_kernel(dest_hbm, y_hbm, x1_ref, tg_ref, gate_ref, gf_ref, o_ref, dest_s, rows_s, dest_sem, row_sem,
                    *, tc):
    i = pl.program_id(0)
    n = pl.num_programs(0)
    n_rows = TOP_K * tc
    slot = i % 2
    nxt = jnp.minimum(i + 1, n - 1)

    def dest_copy(tile, sl):
        return pltpu.make_async_copy(dest_hbm.at[tile], dest_s.at[sl], dest_sem.at[sl])

    def row_copy(src, r, sl):
        return pltpu.make_async_copy(y_hbm.at[pl.ds(src, 1)], rows_s.at[sl, pl.ds(r, 1)], row_sem.at[sl])

    def wait_rows(sl):
        for r in range(n_rows):
            row_copy(0, r, sl).wait()

    @pl.when(i == 0)
    def _():
        first = dest_copy(0, 0)
        first.start()
        first.wait()

        def body(r, c):
            row_copy(dest_s[0, r], r, 0).start()
            return c
        lax.fori_loop(0, n_rows, body, 0, unroll=8)
        dest_copy(nxt, 1).start()

    dest_copy(nxt, 1 - slot).wait()
    wait_rows(slot)

    for r in range(n_rows):
        row_copy(dest_s[1 - slot, r], r, 1 - slot).start()
    tg = tg_ref[...]
    ffn = tg[:, 0:1] * rows_s[slot, 0:tc, :]
    for k in range(1, TOP_K):
        ffn = ffn + tg[:, k:k + 1] * rows_s[slot, k * tc:(k + 1) * tc, :]
    x2 = x1_ref[...] + gate_ref[0] * ffn
    o_ref[...] = x2 * lax.rsqrt(jnp.mean(x2 * x2, axis=-1, keepdims=True) + EPS) * gf_ref[...]
    dest_copy(jnp.minimum(i + 2, n - 1), slot).start()

    @pl.when(i == n - 1)
    def _():
        wait_rows(1 - slot)
        dest_copy(0, slot).wait()


def _combine(dest, y_sorted, x1, top_gate, gate2, g_final, seq_len):
    m = x1.shape[0]
    tc = _row_tile(256, m, seq_len)
    dest_t = dest.reshape(m // tc, tc, TOP_K).transpose(0, 2, 1).reshape(m // tc, TOP_K * tc)
    row = lambda width: pl.BlockSpec((tc, width), lambda i: (i, 0))
    return pl.pallas_call(
        functools.partial(_combine_kernel, tc=tc),
        out_shape=jax.ShapeDtypeStruct((m, D_MODEL), F32),
        grid=(m // tc,),
        in_specs=[pl.BlockSpec(memory_space=pl.ANY), pl.BlockSpec(memory_space=pl.ANY),
                  row(D_MODEL), row(LANES), _mod_spec(tc, seq_len), pl.BlockSpec((1, D_MODEL), lambda i: (0, 0))],
        out_specs=row(D_MODEL),
        scratch_shapes=[pltpu.SMEM((2, TOP_K * tc), I32), pltpu.VMEM((2, TOP_K * tc, D_MODEL), F32),
                        pltpu.SemaphoreType.DMA((2,)), pltpu.SemaphoreType.DMA((2,))],
        compiler_params=_cparams(("arbitrary",)),
        name="moe_combine",
    )(dest_t, y_sorted, x1, top_gate, _mod_arr(gate2, seq_len), g_final.reshape(1, D_MODEL))


def _reorder_w_in(w_in):
    w = w_in.astype(BF16)
    dt_cols = jnp.pad(w[:, XBC_END:DT_END], ((0, 0), (0, DT_PAD - SSM_HEADS)))
    return jnp.concatenate([w[:, :XBC_END], w[:, DT_END:], dt_cols], axis=1)


def _mixer_tail(x, y_ssd, y_att, proj, mod, lw, seq_len, counts0):
    pre = _merge(y_ssd, y_att, proj, lw['w_br_ssd'], lw['w_br_att'])
    return _outproj(pre, x, mod[2], mod[3], mod[4], lw['g_norm2'], lw['w_out'], lw['w_router'], lw['b_router'],
                    seq_len, counts0)


def _forward(x_prompt, x_sample, state_ssm, state_conv, cache_win_k, cache_win_v, c_prompt, c_sample, lw, g_final):
    bp, lp, d = x_prompt.shape
    bs = x_sample.shape[0]
    mp = bp * lp
    xp = x_prompt.reshape(mp, d)
    xs = x_sample.reshape(bs, d)

    mod = _ada(jnp.concatenate([c_prompt, c_sample], axis=0), lw['w_ada'], lw['b_ada'])
    mod_p = [mod[:bp, i * d:(i + 1) * d] for i in range(N_MOD)]
    mod_s = [mod[bp:, i * d:(i + 1) * d] for i in range(N_MOD)]

    proj_p, w_e_in, w_e_out = _inproj(xp, mod_p[0], mod_p[1], lw['g_norm1'], lw['w_in'], lp,
                                      narrow=(lw['w_expert_in'], lw['w_expert_out']))
    proj_s, = _inproj(xs, mod_s[0], mod_s[1], lw['g_norm1'], lw['w_in'], 1)

    yssd_p, ssm_p, conv_p = _ssd_prompt(proj_p, bp, lp, lw)
    yatt_p, wk_p, wv_p = _attn_prompt(proj_p, bp, lp, lw['sinks'])
    yssd_s, ssm_s, conv_s = _ssd_sample(proj_s, state_ssm, state_conv, lw)
    yatt_s, wk_s, wv_s = _attn_sample(proj_s, cache_win_k, cache_win_v, lw['sinks'])

    x1_p, h2_p, tg_p, ti_p, tr_p, cnt_p = _mixer_tail(xp, yssd_p, yatt_p, proj_p, mod_p, lw, lp,
                                                      jnp.zeros((1, LANES), F32))
    x1_s, h2_s, tg_s, ti_s, tr_s, cnt_all = _mixer_tail(xs, yssd_s, yatt_s, proj_s, mod_s, lw, 1, cnt_p)

    h_all = jnp.concatenate([h2_p, h2_s], axis=0)
    top_idx = jnp.concatenate([ti_p[:, :TOP_K], ti_s[:, :TOP_K]], axis=0)
    rank = jnp.concatenate([tr_p[:, :TOP_K], tr_s[:, :TOP_K]], axis=0)
    dest, slot_tok, block_expert, n_active = _route(top_idx, rank, cnt_all[0, :N_EXPERTS].astype(I32))
    y_sorted = _ffn(h_all, slot_tok, block_expert, n_active,
                    w_e_in, lw['b_expert_in'], w_e_out, lw['b_expert_out'])

    y_p = _combine(dest[:mp], y_sorted, x1_p, tg_p, mod_p[5], g_final, lp)
    y_s = _combine(dest[mp:], y_sorted, x1_s, tg_s, mod_s[5], g_final, 1)
    return (y_p.reshape(bp, lp, d), y_s.reshape(bs, 1, d),
            ssm_p[None], conv_p[None], wk_p[None], wv_p[None],
            ssm_s[None], conv_s[None], wk_s[None], wv_s[None])


def kernel(x_prompt, x_sample, state_ssm, state_conv, cache_win_k, cache_win_v, c_prompt, c_sample, w_ada, b_ada, g_norm1, w_in, w_conv, b_conv, dt_bias, a_log, d_skip, g_ssm_norm, sinks, w_br_ssd, w_br_att, w_out, g_norm2, w_router, b_router, w_expert_in, b_expert_in, w_expert_out, b_expert_out, g_final):
    assert w_ada.shape[0] == 1, "single-layer stack"
    lw = dict(w_ada=w_ada[0], b_ada=b_ada[0], g_norm1=g_norm1[0], w_in=_reorder_w_in(w_in[0]),
              w_conv=w_conv[0], b_conv=b_conv[0], dt_bias=dt_bias[0], a_log=a_log[0], d_skip=d_skip[0],
              g_ssm_norm=g_ssm_norm[0], sinks=sinks[0],
              w_br_ssd=w_br_ssd[0].astype(BF16), w_br_att=w_br_att[0].astype(BF16), w_out=w_out[0].astype(BF16),
              g_norm2=g_norm2[0], w_router=w_router[0], b_router=b_router[0],
              w_expert_in=w_expert_in[0], b_expert_in=b_expert_in[0],
              w_expert_out=w_expert_out[0], b_expert_out=b_expert_out[0])
    return _forward(x_prompt, x_sample, state_ssm[0], state_conv[0], cache_win_k[0], cache_win_v[0],
                    c_prompt, c_sample, lw, g_final)
```

```python
import functools
import math

import jax
import jax.numpy as jnp
from jax import lax
from jax.experimental import pallas as pl
from jax.experimental.pallas import tpu as pltpu

F32 = jnp.float32
BF16 = jnp.bfloat16
I32 = jnp.int32
HIGHEST = lax.Precision.HIGHEST

D_MODEL = 2048
PAST_LEN = 16384
D_INNER = 2 * D_MODEL
SSM_HEAD_DIM = 64
SSM_HEADS = D_INNER // SSM_HEAD_DIM
SSM_GROUPS = 8
HEADS_PER_GROUP = SSM_HEADS // SSM_GROUPS
GROUP_WIDTH = HEADS_PER_GROUP * SSM_HEAD_DIM
SSM_STATE = 128
CONV_WIDTH = 4
BC_WIDTH = SSM_GROUPS * SSM_STATE
CONV_DIM = D_INNER + 2 * BC_WIDTH
CHUNK = 128
ATTN_HEADS = 32
KV_HEADS = 8
HEAD_DIM = 64
KV_WIDTH = KV_HEADS * HEAD_DIM
WINDOW = 128
ROPE_DIM = HEAD_DIM // 4
ROPE_THETA = 500000.0
N_EXPERTS = 32
TOP_K = 4
D_FF = D_MODEL
SWIGLU_LIMIT = 7.0
SWIGLU_ALPHA = 1.702
N_MOD = 6
EPS = 1e-6
Z_END = D_INNER
XBC_END = Z_END + CONV_DIM
DT_END = XBC_END + SSM_HEADS
Q_END = DT_END + D_MODEL
K_END = Q_END + KV_WIDTH
V_END = K_END + KV_WIDTH

LANES = 128
BF16_SUBLANES = 16
VMEM_LIMIT = 56 * 1024 * 1024

PC_Z = 0
PC_X = D_INNER
PC_B = PC_X + D_INNER
PC_C = PC_B + BC_WIDTH
PC_Q = PC_C + BC_WIDTH
PC_K = PC_Q + D_MODEL
PC_V = PC_K + KV_WIDTH
PC_G = PC_V + KV_WIDTH
PC_DT = PC_G + 2 * D_MODEL
DT_PAD = 512
PROJ_W = PC_DT + DT_PAD
IN_TN = 1280

MOE_BLK = 512
MOE_TF = 1024


def _cparams(sem):
    return pltpu.CompilerParams(dimension_semantics=sem, vmem_limit_bytes=VMEM_LIMIT)


def _silu(x):
    return x * jax.nn.sigmoid(x)


def _ada_kernel(c_ref, w_ref, b_ref, o_ref):
    s = _silu(c_ref[...]).astype(BF16)
    o_ref[...] = jnp.dot(s, w_ref[...].astype(BF16), preferred_element_type=F32) + b_ref[...]


def _ada(c, w_ada, b_ada):
    n, d = c.shape
    nout = w_ada.shape[1]
    tn = 1024
    return pl.pallas_call(
        _ada_kernel,
        out_shape=jax.ShapeDtypeStruct((n, nout), F32),
        grid=(nout // tn,),
        in_specs=[pl.BlockSpec((n, d), lambda j: (0, 0)),
                  pl.BlockSpec((d, tn), lambda j: (0, j)),
                  pl.BlockSpec((1, tn), lambda j: (0, j))],
        out_specs=pl.BlockSpec((n, tn), lambda j: (0, j)),
        compiler_params=_cparams(("arbitrary",)),
        name="ada_mod",
    )(c, w_ada, b_ada.reshape(1, nout))


def _inproj_kernel(x_ref, sh_ref, sc_ref, g_ref, w_ref, o_ref, h_ref):
    @pl.when(pl.program_id(1) == 0)
    def _():
        x = x_ref[...]
        y = x * lax.rsqrt(jnp.mean(x * x, axis=-1, keepdims=True) + EPS) * g_ref[...]
        h_ref[...] = (y * (1.0 + sc_ref[0]) + sh_ref[0]).astype(BF16)

    o_ref[...] = jnp.dot(h_ref[...], w_ref[...], preferred_element_type=F32)


def _narrow_job(a, n_steps, step_of):
    flat = a.reshape(-1, a.shape[-1])
    rows = flat.shape[0]
    slab = pl.cdiv(pl.cdiv(rows, n_steps), BF16_SUBLANES) * BF16_SUBLANES
    last = pl.cdiv(rows, slab) - 1
    spec = pl.BlockSpec((slab, flat.shape[1]), lambda *g: (jnp.minimum(step_of(*g), last), 0))
    return flat, spec, jax.ShapeDtypeStruct(flat.shape, BF16)


def _mod_spec(tm, seq_len):
    if seq_len == 1:
        return pl.BlockSpec((1, tm, D_MODEL), lambda i, *_: (0, i, 0))
    return pl.BlockSpec((1, 1, D_MODEL), lambda i, *_: ((i * tm) // seq_len, 0, 0))


def _row_tile(cap, m, seq_len):
    tm = min(cap, m if seq_len == 1 else seq_len)
    assert m % tm == 0 and (seq_len == 1 or seq_len % tm == 0)
    return tm


def _mod_arr(m, seq_len):
    return m[None] if seq_len == 1 else m[:, None, :]


def _inproj(x, shift, scale, g, w_in_r, seq_len):
    m = x.shape[0]
    tm = _row_tile(1024, m, seq_len)
    return pl.pallas_call(
        _inproj_kernel,
        out_shape=jax.ShapeDtypeStruct((m, PROJ_W), F32),
        grid=(m // tm, PROJ_W // IN_TN),
        in_specs=[pl.BlockSpec((tm, D_MODEL), lambda i, j: (i, 0)),
                  _mod_spec(tm, seq_len), _mod_spec(tm, seq_len),
                  pl.BlockSpec((1, D_MODEL), lambda i, j: (0, 0)),
                  pl.BlockSpec((D_MODEL, IN_TN), lambda i, j: (0, j))],
        out_specs=pl.BlockSpec((tm, IN_TN), lambda i, j: (i, j)),
        scratch_shapes=[pltpu.VMEM((tm, D_MODEL), BF16)],
        compiler_params=_cparams(("parallel", "arbitrary")),
        name="in_proj",
    )(x, _mod_arr(shift, seq_len), _mod_arr(scale, seq_len), g.reshape(1, D_MODEL), w_in_r)


def _softplus(x):
    return jnp.maximum(x, 0.0) + jnp.log1p(jnp.exp(-jnp.abs(x)))


def _split3(a):
    hi = a.astype(BF16)
    r = a - hi.astype(F32)
    mid = r.astype(BF16)
    lo = (r - mid.astype(F32)).astype(BF16)
    return hi, mid, lo


def _spread(pieces, ones2):
    hi, mid, lo = pieces
    out = jnp.dot(jnp.concatenate([hi, mid], axis=1), ones2, preferred_element_type=F32)
    if lo is not None:
        out = out + jnp.dot(lo, ones2[0:LANES], preferred_element_type=F32)
    return out


def _ssd_kernel(z_ref, x_ref, b_ref, c_ref, dt_ref,
                cw_ref, cb_ref, dtb_ref, alog_ref, dskip_ref, gn_ref, exp_ref, sel_ref, tri_ref, wide_ref,
                y_ref, st_ref, conv_ref, narrow_ref,
                ext_s, xs_s, xdt_s, eax_s, tex_s, elx_s, acst_s, bm_s, cm_s, state_s):
    ci = pl.program_id(1)
    last_chunk = ci == pl.num_programs(1) - 1
    narrow_ref[...] = wide_ref[...].astype(BF16)

    @pl.when(ci == 0)
    def _():
        ext_s[0:8, :] = jnp.zeros((8, CONV_DIM), F32)
        state_s[...] = jnp.zeros_like(state_s)

    ext_s[8:8 + CHUNK, 0:D_INNER] = x_ref[...]
    ext_s[8:8 + CHUNK, D_INNER:D_INNER + BC_WIDTH] = b_ref[...]
    ext_s[8:8 + CHUNK, D_INNER + BC_WIDTH:CONV_DIM] = c_ref[...]

    def conv(lo, hi):
        acc = cb_ref[:, lo:hi] + cw_ref[3:4, lo:hi] * ext_s[8:8 + CHUNK, lo:hi]
        for j in range(CONV_WIDTH - 1):
            acc = acc + cw_ref[j:j + 1, lo:hi] * ext_s[5 + j:5 + j + CHUNK, lo:hi]
        return _silu(acc)

    xs_s[...] = conv(0, D_INNER)
    bm_s[...] = conv(D_INNER, D_INNER + BC_WIDTH)
    cm_s[...] = conv(D_INNER + BC_WIDTH, CONV_DIM)
    tail = ext_s[5 + CHUNK:8 + CHUNK, :]
    conv_ref[0] = tail
    ext_s[5:8, :] = tail

    dt = _softplus(dt_ref[:, 0:LANES] + dtb_ref[...])
    da = dt * (-jnp.exp(alog_ref[...]))
    tri = tri_ref[...]
    acs = sum(jnp.dot(tri, p, preferred_element_type=F32) for p in _split3(da))
    acst_s[...] = acs.T
    acs_pieces = _split3(acs)
    last = acs[CHUNK - 1:CHUNK, :]
    two = lambda a: _split3(a)[:2] + (None,)
    exp2 = exp_ref[...]
    xdt_s[...] = xs_s[...] * _spread(two(dt), exp2)
    eax_s[...] = _spread(two(jnp.exp(acs)), exp2)
    tex_s[...] = _spread(two(jnp.exp(last - acs)), exp2)
    elx_s[...] = _spread(_split3(jnp.broadcast_to(jnp.exp(last), (8, LANES))), exp2)

    row = lax.broadcasted_iota(I32, (CHUNK, CHUNK), 0)
    col = lax.broadcasted_iota(I32, (CHUNK, CHUNK), 1)
    causal = row >= col
    low_half = col < SSM_HEAD_DIM

    def group(g, carry):
        o512 = pl.multiple_of(g * GROUP_WIDTH, GROUP_WIDTH)
        o128 = pl.multiple_of(g * SSM_STATE, SSM_STATE)
        bg = bm_s[:, pl.ds(o128, SSM_STATE)]
        cg16 = cm_s[:, pl.ds(o128, SSM_STATE)].astype(BF16)
        bg16 = bg.astype(BF16)
        cb = lax.dot_general(cg16, bg16, (((1,), (1,)), ((), ())), preferred_element_type=F32)
        sel2 = sel_ref[:, pl.ds(pl.multiple_of(g * HEADS_PER_GROUP * CHUNK, HEADS_PER_GROUP * CHUNK),
                                HEADS_PER_GROUP * CHUNK)]
        colb = _spread(acs_pieces, sel2)
        rows = acst_s[pl.ds(pl.multiple_of(g * HEADS_PER_GROUP, HEADS_PER_GROUP), HEADS_PER_GROUP), :]
        xdt_g = xdt_s[:, pl.ds(o512, GROUP_WIDTH)]
        parts = []
        for jp in range(HEADS_PER_GROUP // 2):
            ms = []
            for j in (2 * jp, 2 * jp + 1):
                diff = colb[:, j * CHUNK:(j + 1) * CHUNK] - rows[j:j + 1, :]
                dec = jnp.exp(jnp.where(causal, diff, -jnp.inf))
                ms.append((cb * dec).astype(BF16))
            xp = xdt_g[:, jp * LANES:(jp + 1) * LANES]
            rhs = jnp.concatenate([jnp.where(low_half, xp, 0.0), jnp.where(low_half, 0.0, xp)], axis=0)
            parts.append(jnp.dot(jnp.concatenate(ms, axis=1), rhs.astype(BF16), preferred_element_type=F32))
        y = jnp.concatenate(parts, axis=1)

        st = state_s[:, pl.ds(o512, GROUP_WIDTH)]
        y = y + jnp.dot(cg16, st.astype(BF16), preferred_element_type=F32) * eax_s[:, pl.ds(o512, GROUP_WIDTH)]
        xw = (xdt_g * tex_s[:, pl.ds(o512, GROUP_WIDTH)]).astype(BF16)
        st_new = (st * elx_s[0:1, pl.ds(o512, GROUP_WIDTH)]
                  + jnp.dot(bg.T.astype(BF16), xw, preferred_element_type=F32))
        state_s[:, pl.ds(o512, GROUP_WIDTH)] = st_new

        @pl.when(last_chunk)
        def _():
            st_ref[0, pl.ds(o512, GROUP_WIDTH), :] = st_new.T

        y = y + dskip_ref[:, pl.ds(o512, GROUP_WIDTH)] * xs_s[:, pl.ds(o512, GROUP_WIDTH)]
        u = y * _silu(z_ref[:, pl.ds(o512, GROUP_WIDTH)])
        u = u * lax.rsqrt(jnp.mean(u * u, axis=-1, keepdims=True) + EPS)
        y_ref[:, pl.ds(o512, GROUP_WIDTH)] = (u * gn_ref[:, pl.ds(o512, GROUP_WIDTH)]).astype(BF16)
        return carry

    lax.fori_loop(0, SSM_GROUPS, group, 0, unroll=4)


def _pad_heads(v):
    return jnp.pad(v.astype(F32), (0, LANES - SSM_HEADS)).reshape(1, LANES)


def _head_expand(dtype=BF16, copies=2):
    h = jnp.arange(LANES)[:, None]
    ch = jnp.arange(D_INNER)[None, :] // SSM_HEAD_DIM
    return jnp.tile((h == ch).astype(dtype), (copies, 1))


def _head_select():
    h = jnp.arange(LANES)[:, None]
    blk = jnp.arange(SSM_HEADS * CHUNK)[None, :] // CHUNK
    return jnp.tile((h == blk).astype(BF16), (2, 1))


def _ssd_prompt(proj, bsz, seq_len, lw, wide):
    nc = seq_len // CHUNK
    m = bsz * seq_len
    tri = (jnp.arange(CHUNK)[:, None] >= jnp.arange(CHUNK)[None, :]).astype(BF16)
    const = lambda shape: pl.BlockSpec(shape, lambda b, c: (0, 0))
    rowblk = lambda width, col: pl.BlockSpec((CHUNK, width), lambda b, c: (b * nc + c, col // width))
    wide2d, wide_spec, narrow_shape = _narrow_job(wide, bsz * nc, lambda b, c: b * nc + c)
    y, st, conv, narrow = pl.pallas_call(
        _ssd_kernel,
        out_shape=(jax.ShapeDtypeStruct((m, D_INNER), BF16),
                   jax.ShapeDtypeStruct((bsz, D_INNER, SSM_STATE), F32),
                   jax.ShapeDtypeStruct((bsz, CONV_WIDTH - 1, CONV_DIM), F32), narrow_shape),
        grid=(bsz, nc),
        in_specs=[rowblk(D_INNER, PC_Z), rowblk(D_INNER, PC_X), rowblk(BC_WIDTH, PC_B), rowblk(BC_WIDTH, PC_C),
                  rowblk(DT_PAD, PC_DT),
                  const((CONV_WIDTH, CONV_DIM)), const((1, CONV_DIM)), const((1, LANES)), const((1, LANES)),
                  const((1, D_INNER)), const((1, D_INNER)), const((2 * LANES, D_INNER)),
                  const((2 * LANES, SSM_HEADS * CHUNK)), const((CHUNK, CHUNK)), wide_spec],
        out_specs=(pl.BlockSpec((CHUNK, D_INNER), lambda b, c: (b * nc + c, 0)),
                   pl.BlockSpec((1, D_INNER, SSM_STATE), lambda b, c: (b, 0, 0)),
                   pl.BlockSpec((1, CONV_WIDTH - 1, CONV_DIM), lambda b, c: (b, 0, 0)), wide_spec),
        scratch_shapes=[pltpu.VMEM((8 + CHUNK, CONV_DIM), F32),
                        pltpu.VMEM((CHUNK, D_INNER), F32), pltpu.VMEM((CHUNK, D_INNER), F32),
                        pltpu.VMEM((CHUNK, D_INNER), F32), pltpu.VMEM((CHUNK, D_INNER), F32),
                        pltpu.VMEM((8, D_INNER), F32), pltpu.VMEM((LANES, CHUNK), F32),
                        pltpu.VMEM((CHUNK, BC_WIDTH), F32), pltpu.VMEM((CHUNK, BC_WIDTH), F32),
                        pltpu.VMEM((SSM_STATE, D_INNER), F32)],
        compiler_params=_cparams(("arbitrary", "arbitrary")),
        name="ssd_chunks",
    )(proj, proj, proj, proj, proj,
      lw['w_conv'], lw['b_conv'].reshape(1, CONV_DIM), _pad_heads(lw['dt_bias']), _pad_heads(lw['a_log']),
      jnp.repeat(lw['d_skip'].astype(F32), SSM_HEAD_DIM).reshape(1, D_INNER),
      lw['g_ssm_norm'].reshape(1, D_INNER), _head_expand(), _head_select(), tri, wide2d)
    return y, st.reshape(bsz, SSM_HEADS, SSM_HEAD_DIM, SSM_STATE), conv, narrow.reshape(wide.shape)


def _ssd_step_pre_kernel(x_ref, b_ref, c_ref, dt_ref, p0_ref, p1_ref, p2_ref,
                         cw_ref, cb_ref, dtb_ref, alog_ref, exp_ref,
                         xs_ref, xdt_ref, bm_ref, cm_ref, dec_ref, conv_ref):
    def conv(u_ref, lo, hi):
        acc = cb_ref[:, lo:hi] + cw_ref[3:4, lo:hi] * u_ref[...]
        for j, p_ref in enumerate((p0_ref, p1_ref, p2_ref)):
            acc = acc + cw_ref[j:j + 1, lo:hi] * p_ref[:, lo:hi]
        return _silu(acc)

    xs = conv(x_ref, 0, D_INNER)
    xs_ref[...] = xs
    bm_ref[...] = conv(b_ref, D_INNER, D_INNER + BC_WIDTH)
    cm_ref[...] = conv(c_ref, D_INNER + BC_WIDTH, CONV_DIM)
    dt = _softplus(dt_ref[:, 0:LANES] + dtb_ref[...])
    dec_ref[...] = jnp.exp(dt * (-jnp.exp(alog_ref[...])))
    xdt_ref[...] = xs * jnp.dot(dt, exp_ref[...], precision=HIGHEST, preferred_element_type=F32)
    conv_ref[0] = p1_ref[...]
    conv_ref[1] = p2_ref[...]
    conv_ref[2, :, 0:D_INNER] = x_ref[...]
    conv_ref[2, :, D_INNER:D_INNER + BC_WIDTH] = b_ref[...]
    conv_ref[2, :, D_INNER + BC_WIDTH:CONV_DIM] = c_ref[...]


def _ssd_step_kernel(dec_ref, st_ref, xdtt_ref, bm_ref, cmt_ref, xs_ref, z_ref, dskip_ref, gn_ref,
                     sto_ref, y_ref, yt_s, *, tb):
    i = pl.program_id(0)
    nb = xs_ref.shape[0]
    sub = lax.broadcasted_iota(I32, (nb, SSM_STATE), 0)
    lane = lax.broadcasted_iota(I32, (SSM_STATE, nb), 1)

    @pl.when(i == 0)
    def _():
        yt_s[...] = jnp.zeros_like(yt_s)

    for bi in range(tb):
        b = i * tb + bi
        for g in range(SSM_GROUPS):
            r0 = g * GROUP_WIDTH
            b_row = jnp.where(sub == b, bm_ref[:, g * SSM_STATE:(g + 1) * SSM_STATE], 0.0).astype(BF16)
            term = jnp.dot(xdtt_ref[r0:r0 + GROUP_WIDTH, :].astype(BF16), b_row, preferred_element_type=F32)
            news = []
            for j in range(HEADS_PER_GROUP):
                h = g * HEADS_PER_GROUP + j
                rr = r0 + j * SSM_HEAD_DIM
                new = (st_ref[bi, rr:rr + SSM_HEAD_DIM, :] * dec_ref[b, h]
                       + term[j * SSM_HEAD_DIM:(j + 1) * SSM_HEAD_DIM, :])
                sto_ref[bi, rr:rr + SSM_HEAD_DIM, :] = new
                news.append(new)
            new_g = jnp.concatenate(news, axis=0).astype(BF16)
            c_col = jnp.where(lane == b, cmt_ref[g], 0.0).astype(BF16)
            yt_s[r0:r0 + GROUP_WIDTH, :] += jnp.dot(new_g, c_col, preferred_element_type=F32)

    @pl.when(i == pl.num_programs(0) - 1)
    def _():
        for g in range(SSM_GROUPS):
            sl = slice(g * GROUP_WIDTH, (g + 1) * GROUP_WIDTH)
            y = yt_s[sl, :].T + dskip_ref[:, sl] * xs_ref[:, sl]
            u = y * _silu(z_ref[:, sl])
            u = u * lax.rsqrt(jnp.mean(u * u, axis=-1, keepdims=True) + EPS)
            y_ref[:, sl] = (u * gn_ref[:, sl]).astype(BF16)


def _ssd_sample(proj, state_ssm, state_conv, lw):
    nb = proj.shape[0]
    full = lambda shape: pl.BlockSpec(shape, lambda *_: tuple(0 for _ in shape))
    colblk = lambda width, col: pl.BlockSpec((nb, width), lambda *_: (0, col // width))
    prev = [state_conv[:, j, :] for j in range(CONV_WIDTH - 1)]
    tr = min(32, nb)
    rows = lambda width, col=0: pl.BlockSpec((tr, width), lambda i: (i, col // width))
    xs, xdt, bm, cm, dec, conv = pl.pallas_call(
        _ssd_step_pre_kernel,
        out_shape=(jax.ShapeDtypeStruct((nb, D_INNER), F32), jax.ShapeDtypeStruct((nb, D_INNER), F32),
                   jax.ShapeDtypeStruct((nb, BC_WIDTH), F32), jax.ShapeDtypeStruct((nb, BC_WIDTH), F32),
                   jax.ShapeDtypeStruct((nb, LANES), F32),
                   jax.ShapeDtypeStruct((CONV_WIDTH - 1, nb, CONV_DIM), F32)),
        grid=(nb // tr,),
        in_specs=[rows(D_INNER, PC_X), rows(BC_WIDTH, PC_B), rows(BC_WIDTH, PC_C), rows(DT_PAD, PC_DT),
                  rows(CONV_DIM), rows(CONV_DIM), rows(CONV_DIM),
                  full((CONV_WIDTH, CONV_DIM)), full((1, CONV_DIM)), full((1, LANES)), full((1, LANES)),
                  full((LANES, D_INNER))],
        out_specs=(rows(D_INNER), rows(D_INNER), rows(BC_WIDTH), rows(BC_WIDTH), rows(LANES),
                   pl.BlockSpec((CONV_WIDTH - 1, tr, CONV_DIM), lambda i: (0, i, 0))),
        compiler_params=_cparams(("parallel",)),
        name="ssd_step_pre",
    )(proj, proj, proj, proj, *prev,
      lw['w_conv'], lw['b_conv'].reshape(1, CONV_DIM), _pad_heads(lw['dt_bias']), _pad_heads(lw['a_log']),
      _head_expand(F32, 1))

    tb = 2
    cmt = cm.reshape(nb, SSM_GROUPS, SSM_STATE).transpose(1, 2, 0)
    st_new, y = pl.pallas_call(
        functools.partial(_ssd_step_kernel, tb=tb),
        out_shape=(jax.ShapeDtypeStruct((nb, D_INNER, SSM_STATE), F32),
                   jax.ShapeDtypeStruct((nb, D_INNER), BF16)),
        grid=(nb // tb,),
        in_specs=[pl.BlockSpec(memory_space=pltpu.SMEM),
                  pl.BlockSpec((tb, D_INNER, SSM_STATE), lambda i: (i, 0, 0)),
                  full((D_INNER, nb)), full((nb, BC_WIDTH)), full((SSM_GROUPS, SSM_STATE, nb)),
                  full((nb, D_INNER)), colblk(D_INNER, PC_Z), full((1, D_INNER)), full((1, D_INNER))],
        out_specs=(pl.BlockSpec((tb, D_INNER, SSM_STATE), lambda i: (i, 0, 0)),
                   full((nb, D_INNER))),
        scratch_shapes=[pltpu.VMEM((D_INNER, nb), F32)],
        compiler_params=_cparams(("arbitrary",)),
        name="ssd_step",
    )(dec, state_ssm.reshape(nb, D_INNER, SSM_STATE), xdt.T, bm, cmt, xs, proj,
      jnp.repeat(lw['d_skip'].astype(F32), SSM_HEAD_DIM).reshape(1, D_INNER),
      lw['g_ssm_norm'].reshape(1, D_INNER))
    return (y, st_new.reshape(nb, SSM_HEADS, SSM_HEAD_DIM, SSM_STATE), conv.transpose(1, 0, 2))


def _rope_tables(pos):
    half = ROPE_DIM // 2
    inv_freq = ROPE_THETA ** (-jnp.arange(half, dtype=F32) * 2.0 / ROPE_DIM)
    ang = pos.astype(F32)[:, None] * inv_freq[None, :]
    cos, sin = jnp.cos(ang), jnp.sin(ang)
    n = pos.shape[0]
    ones = jnp.ones((n, HEAD_DIM - ROPE_DIM), F32)
    zeros = jnp.zeros((n, HEAD_DIM - ROPE_DIM), F32)
    zh = jnp.zeros((n, half), F32)
    cos_f = jnp.concatenate([cos, cos, ones], axis=1)
    sin_a = jnp.concatenate([zh, sin, zeros], axis=1)
    sin_b = jnp.concatenate([-sin, zh, zeros], axis=1)
    tile = lambda t: jnp.tile(t, (1, LANES // HEAD_DIM))
    return tile(cos_f), tile(sin_a), tile(sin_b)


def _rope_slab(x, cos_f, sin_a, sin_b):
    half = ROPE_DIM // 2
    return x * cos_f + pltpu.roll(x, half, 1) * sin_a + pltpu.roll(x, LANES - half, 1) * sin_b


def _attn_kernel(sink_ref, q_ref, k_ref, v_ref, cos_ref, sin_a_ref, sin_b_ref, wide_ref,
                 y_ref, kn_ref, vn_ref, narrow_ref, kspan_s, vspan_s, sc_s, p_s):
    bi = pl.program_id(1)
    nq = q_ref.shape[0]
    narrow_ref[...] = wide_ref[...].astype(BF16)

    @pl.when(bi == 0)
    def _():
        kspan_s[0:nq, :] = jnp.zeros((nq, KV_WIDTH), F32)
        vspan_s[0:nq, :] = jnp.zeros((nq, KV_WIDTH), F32)

    cos_f, sin_a, sin_b = cos_ref[...], sin_a_ref[...], sin_b_ref[...]
    rope = lambda x: _rope_slab(x, cos_f, sin_a, sin_b)
    kr = jnp.concatenate([rope(k_ref[:, s * LANES:(s + 1) * LANES]) for s in range(KV_WIDTH // LANES)], axis=1)
    v = v_ref[...]
    kspan_s[nq:2 * nq, :] = kr
    vspan_s[nq:2 * nq, :] = v
    kn_ref[0] = kr
    vn_ref[0] = v

    t = lax.broadcasted_iota(I32, (nq, nq), 0)
    j = lax.broadcasted_iota(I32, (nq, nq), 1)
    own_block = j <= t
    low = lax.broadcasted_iota(I32, (2 * nq, LANES), 1) < HEAD_DIM
    scale = HEAD_DIM ** -0.5
    grp = ATTN_HEADS // KV_HEADS

    def block_diag(span_ref, kh):
        nat = span_ref[:, (kh // 2) * LANES:(kh // 2 + 1) * LANES]
        swp = pltpu.roll(nat, HEAD_DIM, 1)
        lo, hi = (nat, swp) if kh % 2 == 0 else (swp, nat)
        return jnp.concatenate([jnp.where(low, lo, 0.0), jnp.where(low, 0.0, hi)], axis=0).astype(BF16)

    for kh in range(KV_HEADS):
        kbd = block_diag(kspan_s, kh)
        for p in range(grp // 2):
            h0 = kh * grp + 2 * p
            qp = (rope(q_ref[:, h0 * HEAD_DIM:(h0 + 2) * HEAD_DIM]) * scale).astype(BF16)
            sc = lax.dot_general(qp, kbd, (((1,), (1,)), ((), ())), preferred_element_type=F32)
            for a in range(2):
                prev = sc[:, 2 * a * nq:(2 * a + 1) * nq]
                cur = sc[:, (2 * a + 1) * nq:(2 * a + 2) * nq]
                sc_s[h0 + a] = jnp.where(own_block, cur, prev)

    have_prev = bi > 0
    for h in range(ATTN_HEADS):
        sa = sc_s[h]
        sa = jnp.where(own_block | have_prev, sa, -jnp.inf)
        sink = sink_ref[h]
        mx = jnp.maximum(jnp.max(sa, axis=-1, keepdims=True), sink)
        pa = jnp.exp(sa - mx)
        pa = pa / (jnp.sum(pa, axis=-1, keepdims=True) + jnp.exp(sink - mx))
        p_s[h] = pa.astype(BF16)

    zero = jnp.zeros((nq, nq), BF16)
    for kh in range(KV_HEADS):
        vbd = block_diag(vspan_s, kh)
        for p in range(grp // 2):
            h0 = kh * grp + 2 * p
            parts = []
            for a in range(2):
                pa = p_s[h0 + a]
                parts += [jnp.where(own_block, zero, pa), jnp.where(own_block, pa, zero)]
            y_ref[:, h0 * HEAD_DIM:(h0 + 2) * HEAD_DIM] = jnp.dot(
                jnp.concatenate(parts, axis=1), vbd, preferred_element_type=F32).astype(BF16)

    kspan_s[0:nq, :] = kspan_s[nq:2 * nq, :]
    vspan_s[0:nq, :] = vspan_s[nq:2 * nq, :]


def _attn_prompt(proj, bsz, seq_len, sinks, wide):
    nq = CHUNK
    nb = seq_len // nq
    m = bsz * seq_len
    cos_f, sin_a, sin_b = _rope_tables(jnp.arange(seq_len))
    rowblk = lambda width, col: pl.BlockSpec((nq, width), lambda b, i: (b * nb + i, col // width))
    tab = pl.BlockSpec((nq, LANES), lambda b, i: (i, 0))
    wide2d, wide_spec, narrow_shape = _narrow_job(wide, bsz * nb, lambda b, i: b * nb + i)
    y, kn, vn, narrow = pl.pallas_call(
        _attn_kernel,
        out_shape=(jax.ShapeDtypeStruct((m, D_MODEL), BF16),
                   jax.ShapeDtypeStruct((bsz, WINDOW, KV_WIDTH), F32),
                   jax.ShapeDtypeStruct((bsz, WINDOW, KV_WIDTH), F32), narrow_shape),
        grid=(bsz, nb),
        in_specs=[pl.BlockSpec(memory_space=pltpu.SMEM),
                  rowblk(D_MODEL, PC_Q), rowblk(KV_WIDTH, PC_K), rowblk(KV_WIDTH, PC_V), tab, tab, tab,
                  wide_spec],
        out_specs=(pl.BlockSpec((nq, D_MODEL), lambda b, i: (b * nb + i, 0)),
                   pl.BlockSpec((1, WINDOW, KV_WIDTH), lambda b, i: (b, 0, 0)),
                   pl.BlockSpec((1, WINDOW, KV_WIDTH), lambda b, i: (b, 0, 0)), wide_spec),
        scratch_shapes=[pltpu.VMEM((2 * nq, KV_WIDTH), F32), pltpu.VMEM((2 * nq, KV_WIDTH), F32),
                        pltpu.VMEM((ATTN_HEADS, nq, nq), F32), pltpu.VMEM((ATTN_HEADS, nq, nq), BF16)],
        compiler_params=_cparams(("arbitrary", "arbitrary")),
        name="attn_blocks",
    )(sinks.astype(F32), proj, proj, proj, cos_f, sin_a, sin_b, wide2d)
    shape = (bsz, WINDOW, KV_HEADS, HEAD_DIM)
    return y, kn.reshape(shape), vn.reshape(shape), narrow.reshape(wide.shape)


def _attn_step_kernel(sink_ref, q_ref, k_ref, v_ref, cos_ref, sin_a_ref, sin_b_ref, fold_ref, foldt_ref,
                      kc_ref, vc_ref, y_ref, ko_ref, vo_ref, *, tb):
    cos_f, sin_a, sin_b = cos_ref[...], sin_a_ref[...], sin_b_ref[...]
    rope = lambda ref, w: jnp.concatenate(
        [_rope_slab(ref[:, s * LANES:(s + 1) * LANES], cos_f, sin_a, sin_b) for s in range(w // LANES)], axis=1)
    qr = rope(q_ref, D_MODEL) * (HEAD_DIM ** -0.5)
    kr = rope(k_ref, KV_WIDTH)
    v = v_ref[...]

    own = (lax.broadcasted_iota(I32, (ATTN_HEADS, D_MODEL), 1) // HEAD_DIM
           == lax.broadcasted_iota(I32, (ATTN_HEADS, D_MODEL), 0))
    qb = jnp.concatenate([jnp.where(own, jnp.broadcast_to(qr[bi:bi + 1], (ATTN_HEADS, D_MODEL)), 0.0)
                          for bi in range(tb)], axis=0).astype(BF16)
    qm = jnp.dot(qb, fold_ref[...], preferred_element_type=F32).astype(BF16)

    row = lax.broadcasted_iota(I32, (WINDOW, KV_WIDTH), 0)
    w = lax.broadcasted_iota(I32, (ATTN_HEADS, WINDOW), 1)
    sink = sink_ref[...]
    bf = lambda x: x.astype(BF16)
    outs = []
    for bi in range(tb):
        qmb = qm[bi * ATTN_HEADS:(bi + 1) * ATTN_HEADS]
        kc, vc = kc_ref[bi], vc_ref[bi]
        kn, vn = kr[bi:bi + 1], v[bi:bi + 1]
        sc = lax.dot_general(qmb, bf(kc), (((1,), (1,)), ((), ())), preferred_element_type=F32)
        sc = jnp.where(w >= 1, sc, -jnp.inf)
        s_new = jnp.sum(qmb.astype(F32) * bf(kn).astype(F32), axis=-1, keepdims=True)
        mx = jnp.maximum(jnp.maximum(jnp.max(sc, axis=-1, keepdims=True), s_new), sink)
        p = jnp.exp(sc - mx)
        p_new = jnp.exp(s_new - mx)
        den = jnp.sum(p, axis=-1, keepdims=True) + p_new + jnp.exp(sink - mx)
        p = p / den
        p_new = p_new / den
        out = jnp.dot(bf(p), bf(vc), preferred_element_type=F32) + bf(p_new).astype(F32) * bf(vn).astype(F32)
        outs.append(bf(out))
        ko_ref[bi] = jnp.where(row == WINDOW - 1, kn, pltpu.roll(kc, WINDOW - 1, 0))
        vo_ref[bi] = jnp.where(row == WINDOW - 1, vn, pltpu.roll(vc, WINDOW - 1, 0))

    of = jnp.dot(jnp.concatenate(outs, axis=0), foldt_ref[...], preferred_element_type=F32)
    of = of.reshape(tb, ATTN_HEADS, D_MODEL)
    y_ref[...] = jnp.sum(jnp.where(own[None], of, 0.0), axis=1).astype(BF16)


def _attn_sample(proj, cache_k, cache_v, sinks):
    nb = proj.shape[0]
    tb = 8
    grp = ATTN_HEADS // KV_HEADS
    cos_f, sin_a, sin_b = _rope_tables(jnp.full((1,), PAST_LEN))
    c = jnp.arange(D_MODEL)[:, None]
    l = jnp.arange(KV_WIDTH)[None, :]
    fold = ((c % HEAD_DIM == l % HEAD_DIM) & ((c // HEAD_DIM) // grp == l // HEAD_DIM)).astype(BF16)
    full = lambda shape: pl.BlockSpec(shape, lambda i: tuple(0 for _ in shape))
    rowblk = lambda width, col: pl.BlockSpec((tb, width), lambda i: (i, col // width))
    win = pl.BlockSpec((tb, WINDOW, KV_WIDTH), lambda i: (i, 0, 0))
    y, kn, vn = pl.pallas_call(
        functools.partial(_attn_step_kernel, tb=tb),
        out_shape=(jax.ShapeDtypeStruct((nb, D_MODEL), BF16),
                   jax.ShapeDtypeStruct((nb, WINDOW, KV_WIDTH), F32),
                   jax.ShapeDtypeStruct((nb, WINDOW, KV_WIDTH), F32)),
        grid=(nb // tb,),
        in_specs=[full((ATTN_HEADS, 1)), rowblk(D_MODEL, PC_Q), rowblk(KV_WIDTH, PC_K), rowblk(KV_WIDTH, PC_V),
                  full((1, LANES)), full((1, LANES)), full((1, LANES)),
                  full((D_MODEL, KV_WIDTH)), full((KV_WIDTH, D_MODEL)), win, win],
        out_specs=(pl.BlockSpec((tb, D_MODEL), lambda i: (i, 0)), win, win),
        compiler_params=_cparams(("parallel",)),
        name="attn_step",
    )(sinks.astype(F32).reshape(ATTN_HEADS, 1), proj, proj, proj, cos_f, sin_a, sin_b, fold, fold.T,
      cache_k.reshape(nb, WINDOW, KV_WIDTH), cache_v.reshape(nb, WINDOW, KV_WIDTH))
    shape = (nb, WINDOW, KV_HEADS, HEAD_DIM)
    return y, kn.reshape(shape), vn.reshape(shape)


def _merge_kernel(ys_ref, ya_ref, ws_ref, wa_ref, gs_ref, ga_ref, o_ref):
    a = jnp.dot(ys_ref[...], ws_ref[...], preferred_element_type=F32)
    b = jnp.dot(ya_ref[...], wa_ref[...], preferred_element_type=F32)
    o_ref[...] = (jax.nn.sigmoid(gs_ref[...]) * a + jax.nn.sigmoid(ga_ref[...]) * b).astype(BF16)


def _merge(y_ssd, y_att, proj, w_ssd, w_att):
    m = y_ssd.shape[0]
    tm = min(512, m)
    tn = 512
    g0 = PC_G // tn
    return pl.pallas_call(
        _merge_kernel,
        out_shape=jax.ShapeDtypeStruct((m, D_MODEL), BF16),
        grid=(m // tm, D_MODEL // tn),
        in_specs=[pl.BlockSpec((tm, D_INNER), lambda i, j: (i, 0)),
                  pl.BlockSpec((tm, D_MODEL), lambda i, j: (i, 0)),
                  pl.BlockSpec((D_INNER, tn), lambda i, j: (0, j)),
                  pl.BlockSpec((D_MODEL, tn), lambda i, j: (0, j)),
                  pl.BlockSpec((tm, tn), lambda i, j: (i, g0 + j)),
                  pl.BlockSpec((tm, tn), lambda i, j: (i, g0 + D_MODEL // tn + j))],
        out_specs=pl.BlockSpec((tm, tn), lambda i, j: (i, j)),
        compiler_params=_cparams(("parallel", "arbitrary")),
        name="branch_merge",
    )(y_ssd, y_att, w_ssd, w_att, proj, proj)


def _outproj_kernel(pre_ref, x_ref, gate_ref, sh_ref, sc_ref, g_ref, w_ref, wr_ref, br_ref, before_ref, c0_ref,
                    x1_ref, h2_ref, tg_ref, ti_ref, tr_ref, cnt_ref, cnt_s):
    @pl.when(pl.program_id(0) == 0)
    def _():
        cnt_s[...] = c0_ref[...]

    mixed = jnp.dot(pre_ref[...], w_ref[...], preferred_element_type=F32)
    x1 = x_ref[...] + gate_ref[0] * mixed
    x1_ref[...] = x1
    y = x1 * lax.rsqrt(jnp.mean(x1 * x1, axis=-1, keepdims=True) + EPS) * g_ref[...]
    h2 = y * (1.0 + sc_ref[0]) + sh_ref[0]
    h2_ref[...] = h2
    h_hi, h_mid, _ = _split3(h2)
    hw = jnp.dot(h_hi, wr_ref[...], preferred_element_type=F32)
    logits = (hw[:, 0:LANES] + hw[:, LANES:2 * LANES]
              + jnp.dot(h_mid, wr_ref[:, 0:LANES], preferred_element_type=F32) + br_ref[...])
    lane = lax.broadcasted_iota(I32, logits.shape, 1)
    lane_f = lane.astype(F32)
    vals = jnp.zeros_like(logits)
    idxs = jnp.zeros_like(logits)
    top = None
    den = jnp.zeros((logits.shape[0], 1), F32)
    chosen = []
    for k in range(TOP_K):
        mx = jnp.max(logits, axis=-1, keepdims=True)
        ix = jnp.min(jnp.where(logits == mx, lane_f, float(LANES)), axis=-1, keepdims=True)
        top = mx if top is None else top
        e = jnp.exp(mx - top)
        den = den + e
        vals = jnp.where(lane == k, e, vals)
        idxs = jnp.where(lane == k, ix, idxs)
        chosen.append(lane_f == ix)
        logits = jnp.where(chosen[-1], -jnp.inf, logits)
    tg_ref[...] = vals / den
    ti_ref[...] = idxs.astype(I32)

    picked = jnp.zeros_like(vals)
    for c in chosen:
        picked = jnp.where(c, 1.0, picked)
    ahead = jnp.dot(before_ref[...], picked.astype(BF16), preferred_element_type=F32) + cnt_s[...]
    ranks = jnp.zeros_like(vals)
    for k, c in enumerate(chosen):
        ranks = jnp.where(lane == k, jnp.sum(jnp.where(c, ahead, 0.0), axis=-1, keepdims=True), ranks)
    tr_ref[...] = ranks.astype(I32)
    cnt_s[...] += jnp.sum(picked, axis=0, keepdims=True)
    cnt_ref[...] = cnt_s[...]


def _outproj(pre, x, gate1, shift2, scale2, g2, w_out, w_router, b_router, seq_len, counts0):
    m = x.shape[0]
    tm = _row_tile(512, m, seq_len)
    before = (jnp.arange(tm)[None, :] < jnp.arange(tm)[:, None]).astype(BF16)
    w_hi, w_mid, _ = _split3(jnp.pad(w_router.astype(F32), ((0, 0), (0, LANES - N_EXPERTS))))
    wr = jnp.concatenate([w_hi, w_mid], axis=1)
    br = jnp.pad(b_router.astype(F32), (0, LANES - N_EXPERTS), constant_values=-jnp.inf).reshape(1, LANES)
    row = lambda width: pl.BlockSpec((tm, width), lambda i: (i, 0))
    const = lambda shape: pl.BlockSpec(shape, lambda i: (0, 0))
    return pl.pallas_call(
        _outproj_kernel,
        out_shape=(jax.ShapeDtypeStruct((m, D_MODEL), F32), jax.ShapeDtypeStruct((m, D_MODEL), F32),
                   jax.ShapeDtypeStruct((m, LANES), F32), jax.ShapeDtypeStruct((m, LANES), I32),
                   jax.ShapeDtypeStruct((m, LANES), I32), jax.ShapeDtypeStruct((1, LANES), F32)),
        grid=(m // tm,),
        in_specs=[row(D_MODEL), row(D_MODEL), _mod_spec(tm, seq_len), _mod_spec(tm, seq_len), _mod_spec(tm, seq_len),
                  const((1, D_MODEL)), const((D_MODEL, D_MODEL)), const((D_MODEL, 2 * LANES)), const((1, LANES)),
                  const((tm, tm)), const((1, LANES))],
        out_specs=(row(D_MODEL), row(D_MODEL), row(LANES), row(LANES), row(LANES), const((1, LANES))),
        scratch_shapes=[pltpu.VMEM((1, LANES), F32)],
        compiler_params=_cparams(("arbitrary",)),
        name="out_proj_router",
    )(pre, x, _mod_arr(gate1, seq_len), _mod_arr(shift2, seq_len), _mod_arr(scale2, seq_len),
      g2.reshape(1, D_MODEL), w_out, wr, br, before, counts0)


def _ffn_kernel(be_ref, na_ref, tok_hbm, h_hbm, w1g_ref, w1l_ref, b1g_ref, b1l_ref, w2_ref, b2_ref,
                o_ref, tok_s, rows_s, xb_s, acc_s, tok_sem, row_sem):
    b = pl.program_id(0)
    f = pl.program_id(1)
    n_blocks = pl.num_programs(0)
    nf = D_FF // MOE_TF
    rows_per_step = MOE_BLK // nf
    n_active = na_ref[0]
    active = b < n_active
    nxt = jnp.minimum(b + 1, n_blocks - 1)
    nxt_slot = (b + 1) % 2

    def tok_copy(blk, sl):
        return pltpu.make_async_copy(tok_hbm.at[blk], tok_s.at[sl], tok_sem.at[sl])

    def row_copy(tok, r):
        return pltpu.make_async_copy(h_hbm.at[pl.ds(tok, 1)], rows_s.at[pl.ds(r, 1)], row_sem)

    def wait_rows():
        for r in range(MOE_BLK):
            row_copy(0, r).wait()

    @pl.when(active & (f == 0))
    def _():
        @pl.when(b == 0)
        def _():
            first = tok_copy(0, 0)
            first.start()
            first.wait()

            def body(r, c):
                row_copy(tok_s[0, r], r).start()
                return c
            lax.fori_loop(0, MOE_BLK, body, 0, unroll=8)
            tok_copy(nxt, 1).start()

        wait_rows()
        xb_s[...] = rows_s[...].astype(BF16)
        tok_copy(nxt, nxt_slot).wait()

    @pl.when(active)
    def _():
        base = f * rows_per_step
        for i in range(rows_per_step):
            row_copy(tok_s[nxt_slot, base + i], base + i).start()
        xb = xb_s[...]
        glu = jnp.dot(xb, w1g_ref[0], preferred_element_type=F32) + b1g_ref[0]
        lin = jnp.dot(xb, w1l_ref[0], preferred_element_type=F32) + b1l_ref[0]
        glu = jnp.minimum(glu, SWIGLU_LIMIT)
        lin = jnp.clip(lin, -SWIGLU_LIMIT, SWIGLU_LIMIT)
        act = glu * jax.nn.sigmoid(SWIGLU_ALPHA * glu) * (lin + 1.0)
        part = jnp.dot(act.astype(BF16), w2_ref[0], preferred_element_type=F32)

        @pl.when(f == 0)
        def _():
            acc_s[...] = part + b2_ref[0]

        @pl.when((f > 0) & (f < nf - 1))
        def _():
            acc_s[...] += part

        @pl.when(f == nf - 1)
        def _():
            o_ref[...] = acc_s[...] + part
            tok_copy(jnp.minimum(b + 2, n_blocks - 1), b % 2).start()

    @pl.when(((b == n_active) & (f == 0)) | (active & (b == n_blocks - 1) & (f == nf - 1)))
    def _():
        wait_rows()
        tok_copy(0, jnp.where(active, b, b + 1) % 2).wait()

    @pl.when((b >= n_active) & (f == nf - 1))
    def _():
        o_ref[...] = jnp.zeros_like(o_ref)


def _ffn(h_all, slot_tok, block_expert, n_active, w1, b1, w2, b2):
    n_blocks = slot_tok.shape[0]
    nf = D_FF // MOE_TF

    def widx(col_off):
        def index_map(b, f, be, na):
            live = b < na[0]
            bb = jnp.where(live, b, na[0] - 1)
            ff = jnp.where(live, f, nf - 1)
            return be[bb], 0, col_off + ff
        return index_map

    def w2idx(b, f, be, na):
        live = b < na[0]
        return be[jnp.where(live, b, na[0] - 1)], jnp.where(live, f, nf - 1), 0

    def b2idx(b, f, be, na):
        return be[jnp.where(b < na[0], b, na[0] - 1)], 0, 0

    grid_spec = pltpu.PrefetchScalarGridSpec(
        num_scalar_prefetch=2,
        grid=(n_blocks, nf),
        in_specs=[pl.BlockSpec(memory_space=pl.ANY), pl.BlockSpec(memory_space=pl.ANY),
                  pl.BlockSpec((1, D_MODEL, MOE_TF), widx(0)), pl.BlockSpec((1, D_MODEL, MOE_TF), widx(nf)),
                  pl.BlockSpec((1, 1, MOE_TF), widx(0)), pl.BlockSpec((1, 1, MOE_TF), widx(nf)),
                  pl.BlockSpec((1, MOE_TF, D_MODEL), w2idx), pl.BlockSpec((1, 1, D_MODEL), b2idx)],
        out_specs=pl.BlockSpec((MOE_BLK, D_MODEL), lambda b, f, be, na: (b, 0)),
        scratch_shapes=[pltpu.SMEM((2, MOE_BLK), I32),
                        pltpu.VMEM((MOE_BLK, D_MODEL), F32),
                        pltpu.VMEM((MOE_BLK, D_MODEL), BF16),
                        pltpu.VMEM((MOE_BLK, D_MODEL), F32),
                        pltpu.SemaphoreType.DMA((2,)), pltpu.SemaphoreType.DMA])
    return pl.pallas_call(
        _ffn_kernel,
        out_shape=jax.ShapeDtypeStruct((n_blocks * MOE_BLK, D_MODEL), F32),
        grid_spec=grid_spec,
        compiler_params=_cparams(("arbitrary", "arbitrary")),
        name="expert_ffn",
    )(block_expert, n_active, slot_tok, h_all, w1, w1, b1[:, None, :], b1[:, None, :], w2, b2[:, None, :])


def _route(top_idx, rank, counts):
    n_tok = top_idx.shape[0]
    n_assign = n_tok * TOP_K
    e_flat = top_idx.reshape(-1)
    rank = rank.reshape(-1)
    onehot = (e_flat[:, None] == jnp.arange(N_EXPERTS)[None, :]).astype(I32)
    nblk = (counts + MOE_BLK - 1) // MOE_BLK
    blk_end = jnp.cumsum(nblk)
    pad_start = (blk_end - nblk) * MOE_BLK
    dest = jnp.sum(onehot * pad_start[None, :], axis=1) + rank
    n_blocks = -(-n_assign // MOE_BLK) + N_EXPERTS
    slot_tok = jnp.zeros((n_blocks * MOE_BLK,), I32).at[dest].set(
        jnp.arange(n_assign, dtype=I32) // TOP_K, unique_indices=True)
    block_expert = jnp.minimum(jnp.sum(blk_end[None, :] <= jnp.arange(n_blocks)[:, None], axis=1),
                               N_EXPERTS - 1).astype(I32)
    n_active = blk_end[-1:].astype(I32)
    return dest.reshape(n_tok, TOP_K).astype(I32), slot_tok.reshape(n_blocks, MOE_BLK), block_expert, n_active


def _combine_kernel(dest_hbm, y_hbm, x1_ref, tg_ref, gate_ref, gf_ref, o_ref, dest_s, rows_s, dest_sem, row_sem,
                    *, tc):
    i = pl.program_id(0)
    n = pl.num_programs(0)
    n_rows = TOP_K * tc
    slot = i % 2
    nxt = jnp.minimum(i + 1, n - 1)

    def dest_copy(tile, sl):
        return pltpu.make_async_copy(dest_hbm.at[tile], dest_s.at[sl], dest_sem.at[sl])

    def row_copy(src, r, sl):
        return pltpu.make_async_copy(y_hbm.at[pl.ds(src, 1)], rows_s.at[sl, pl.ds(r, 1)], row_sem.at[sl])

    def wait_rows(sl):
        for r in range(n_rows):
            row_copy(0, r, sl).wait()

    @pl.when(i == 0)
    def _():
        first = dest_copy(0, 0)
        first.start()
        first.wait()

        def body(r, c):
            row_copy(dest_s[0, r], r, 0).start()
            return c
        lax.fori_loop(0, n_rows, body, 0, unroll=8)
        dest_copy(nxt, 1).start()

    dest_copy(nxt, 1 - slot).wait()
    wait_rows(slot)

    for r in range(n_rows):
        row_copy(dest_s[1 - slot, r], r, 1 - slot).start()
    tg = tg_ref[...]
    ffn = tg[:, 0:1] * rows_s[slot, 0:tc, :]
    for k in range(1, TOP_K):
        ffn = ffn + tg[:, k:k + 1] * rows_s[slot, k * tc:(k + 1) * tc, :]
    x2 = x1_ref[...] + gate_ref[0] * ffn
    o_ref[...] = x2 * lax.rsqrt(jnp.mean(x2 * x2, axis=-1, keepdims=True) + EPS) * gf_ref[...]
    dest_copy(jnp.minimum(i + 2, n - 1), slot).start()

    @pl.when(i == n - 1)
    def _():
        wait_rows(1 - slot)
        dest_copy(0, slot).wait()


def _combine(dest, y_sorted, x1, top_gate, gate2, g_final, seq_len):
    m = x1.shape[0]
    tc = _row_tile(256, m, seq_len)
    dest_t = dest.reshape(m // tc, tc, TOP_K).transpose(0, 2, 1).reshape(m // tc, TOP_K * tc)
    row = lambda width: pl.BlockSpec((tc, width), lambda i: (i, 0))
    return pl.pallas_call(
        functools.partial(_combine_kernel, tc=tc),
        out_shape=jax.ShapeDtypeStruct((m, D_MODEL), F32),
        grid=(m // tc,),
        in_specs=[pl.BlockSpec(memory_space=pl.ANY), pl.BlockSpec(memory_space=pl.ANY),
                  row(D_MODEL), row(LANES), _mod_spec(tc, seq_len), pl.BlockSpec((1, D_MODEL), lambda i: (0, 0))],
        out_specs=row(D_MODEL),
        scratch_shapes=[pltpu.SMEM((2, TOP_K * tc), I32), pltpu.VMEM((2, TOP_K * tc, D_MODEL), F32),
                        pltpu.SemaphoreType.DMA((2,)), pltpu.SemaphoreType.DMA((2,))],
        compiler_params=_cparams(("arbitrary",)),
        name="moe_combine",
    )(dest_t, y_sorted, x1, top_gate, _mod_arr(gate2, seq_len), g_final.reshape(1, D_MODEL))


def _reorder_w_in(w_in):
    w = w_in.astype(BF16)
    dt_cols = jnp.pad(w[:, XBC_END:DT_END], ((0, 0), (0, DT_PAD - SSM_HEADS)))
    return jnp.concatenate([w[:, :XBC_END], w[:, DT_END:], dt_cols], axis=1)


def _mixer_tail(x, y_ssd, y_att, proj, mod, lw, seq_len, counts0):
    pre = _merge(y_ssd, y_att, proj, lw['w_br_ssd'], lw['w_br_att'])
    return _outproj(pre, x, mod[2], mod[3], mod[4], lw['g_norm2'], lw['w_out'], lw['w_router'], lw['b_router'],
                    seq_len, counts0)


def _forward(x_prompt, x_sample, state_ssm, state_conv, cache_win_k, cache_win_v, c_prompt, c_sample, lw, g_final):
    bp, lp, d = x_prompt.shape
    bs = x_sample.shape[0]
    mp = bp * lp
    xp = x_prompt.reshape(mp, d)
    xs = x_sample.reshape(bs, d)

    mod = _ada(jnp.concatenate([c_prompt, c_sample], axis=0), lw['w_ada'], lw['b_ada'])
    mod_p = [mod[:bp, i * d:(i + 1) * d] for i in range(N_MOD)]
    mod_s = [mod[bp:, i * d:(i + 1) * d] for i in range(N_MOD)]

    proj_p = _inproj(xp, mod_p[0], mod_p[1], lw['g_norm1'], lw['w_in'], lp)
    proj_s = _inproj(xs, mod_s[0], mod_s[1], lw['g_norm1'], lw['w_in'], 1)

    yssd_p, ssm_p, conv_p, w_e_out = _ssd_prompt(proj_p, bp, lp, lw, lw['w_expert_out'])
    yatt_p, wk_p, wv_p, w_e_in = _attn_prompt(proj_p, bp, lp, lw['sinks'], lw['w_expert_in'])
    yssd_s, ssm_s, conv_s = _ssd_sample(proj_s, state_ssm, state_conv, lw)
    yatt_s, wk_s, wv_s = _attn_sample(proj_s, cache_win_k, cache_win_v, lw['sinks'])

    x1_p, h2_p, tg_p, ti_p, tr_p, cnt_p = _mixer_tail(xp, yssd_p, yatt_p, proj_p, mod_p, lw, lp,
                                                      jnp.zeros((1, LANES), F32))
    x1_s, h2_s, tg_s, ti_s, tr_s, cnt_all = _mixer_tail(xs, yssd_s, yatt_s, proj_s, mod_s, lw, 1, cnt_p)

    h_all = jnp.concatenate([h2_p, h2_s], axis=0)
    top_idx = jnp.concatenate([ti_p[:, :TOP_K], ti_s[:, :TOP_K]], axis=0)
    rank = jnp.concatenate([tr_p[:, :TOP_K], tr_s[:, :TOP_K]], axis=0)
    dest, slot_tok, block_expert, n_active = _route(top_idx, rank, cnt_all[0, :N_EXPERTS].astype(I32))
    y_sorted = _ffn(h_all, slot_tok, block_expert, n_active,
                    w_e_in, lw['b_expert_in'], w_e_out, lw['b_expert_out'])

    y_p = _combine(dest[:mp], y_sorted, x1_p, tg_p, mod_p[5], g_final, lp)
    y_s = _combine(dest[mp:], y_sorted, x1_s, tg_s, mod_s[5], g_final, 1)
    return (y_p.reshape(bp, lp, d), y_s.reshape(bs, 1, d),
            ssm_p[None], conv_p[None], wk_p[None], wv_p[None],
            ssm_s[None], conv_s[None], wk_s[None], wv_s[None])


def kernel(x_prompt, x_sample, state_ssm, state_conv, cache_win_k, cache_win_v, c_prompt, c_sample, w_ada, b_ada, g_norm1, w_in, w_conv, b_conv, dt_bias, a_log, d_skip, g_ssm_norm, sinks, w_br_ssd, w_br_att, w_out, g_norm2, w_router, b_router, w_expert_in, b_expert_in, w_expert_out, b_expert_out, g_final):
    assert w_ada.shape[0] == 1, "single-layer stack"
    lw = dict(w_ada=w_ada[0], b_ada=b_ada[0], g_norm1=g_norm1[0], w_in=_reorder_w_in(w_in[0]),
              w_conv=w_conv[0], b_conv=b_conv[0], dt_bias=dt_bias[0], a_log=a_log[0], d_skip=d_skip[0],
              g_ssm_norm=g_ssm_norm[0], sinks=sinks[0],
              w_br_ssd=w_br_ssd[0].astype(BF16), w_br_att=w_br_att[0].astype(BF16), w_out=w_out[0].astype(BF16),
              g_norm2=g_norm2[0], w_router=w_router[0], b_router=b_router[0],
              w_expert_in=w_expert_in[0], b_expert_in=b_expert_in[0],
              w_expert_out=w_expert_out[0], b_expert_out=b_expert_out[0])
    return _forward(x_prompt, x_sample, state_ssm[0], state_conv[0], cache_win_k[0], cache_win_v[0],
                    c_prompt, c_sample, lw, g_final)
```

```python
import functools
import math

import jax
import jax.numpy as jnp
from jax import lax
from jax.experimental import pallas as pl
from jax.experimental.pallas import tpu as pltpu

F32 = jnp.float32
BF16 = jnp.bfloat16
I32 = jnp.int32
HIGHEST = lax.Precision.HIGHEST

D_MODEL = 2048
PAST_LEN = 16384
D_INNER = 2 * D_MODEL
SSM_HEAD_DIM = 64
SSM_HEADS = D_INNER // SSM_HEAD_DIM
SSM_GROUPS = 8
HEADS_PER_GROUP = SSM_HEADS // SSM_GROUPS
GROUP_WIDTH = HEADS_PER_GROUP * SSM_HEAD_DIM
SSM_STATE = 128
CONV_WIDTH = 4
BC_WIDTH = SSM_GROUPS * SSM_STATE
CONV_DIM = D_INNER + 2 * BC_WIDTH
CHUNK = 128
ATTN_HEADS = 32
KV_HEADS = 8
HEAD_DIM = 64
KV_WIDTH = KV_HEADS * HEAD_DIM
WINDOW = 128
ROPE_DIM = HEAD_DIM // 4
ROPE_THETA = 500000.0
N_EXPERTS = 32
TOP_K = 4
D_FF = D_MODEL
SWIGLU_LIMIT = 7.0
SWIGLU_ALPHA = 1.702
N_MOD = 6
EPS = 1e-6
Z_END = D_INNER
XBC_END = Z_END + CONV_DIM
DT_END = XBC_END + SSM_HEADS
Q_END = DT_END + D_MODEL
K_END = Q_END + KV_WIDTH
V_END = K_END + KV_WIDTH

LANES = 128
BF16_SUBLANES = 16
VMEM_LIMIT = 56 * 1024 * 1024

PC_Z = 0
PC_X = D_INNER
PC_B = PC_X + D_INNER
PC_C = PC_B + BC_WIDTH
PC_Q = PC_C + BC_WIDTH
PC_K = PC_Q + D_MODEL
PC_V = PC_K + KV_WIDTH
PC_G = PC_V + KV_WIDTH
PC_DT = PC_G + 2 * D_MODEL
DT_PAD = 512
PROJ_W = PC_DT + DT_PAD
IN_TN = 1280

MOE_BLK = 512
MOE_TF = 1024


def _cparams(sem):
    return pltpu.CompilerParams(dimension_semantics=sem, vmem_limit_bytes=VMEM_LIMIT)


def _silu(x):
    return x * jax.nn.sigmoid(x)


def _ada_kernel(c_ref, w_ref, b_ref, o_ref):
    s = _silu(c_ref[...]).astype(BF16)
    o_ref[...] = jnp.dot(s, w_ref[...].astype(BF16), preferred_element_type=F32) + b_ref[...]


def _ada(c, w_ada, b_ada):
    n, d = c.shape
    nout = w_ada.shape[1]
    tn = 1024
    return pl.pallas_call(
        _ada_kernel,
        out_shape=jax.ShapeDtypeStruct((n, nout), F32),
        grid=(nout // tn,),
        in_specs=[pl.BlockSpec((n, d), lambda j: (0, 0)),
                  pl.BlockSpec((d, tn), lambda j: (0, j)),
                  pl.BlockSpec((1, tn), lambda j: (0, j))],
        out_specs=pl.BlockSpec((n, tn), lambda j: (0, j)),
        compiler_params=_cparams(("arbitrary",)),
        name="ada_mod",
    )(c, w_ada, b_ada.reshape(1, nout))


def _inproj_kernel(x_ref, sh_ref, sc_ref, g_ref, w_ref, o_ref, h_ref):
    @pl.when(pl.program_id(1) == 0)
    def _():
        x = x_ref[...]
        y = x * lax.rsqrt(jnp.mean(x * x, axis=-1, keepdims=True) + EPS) * g_ref[...]
        h_ref[...] = (y * (1.0 + sc_ref[0]) + sh_ref[0]).astype(BF16)

    o_ref[...] = jnp.dot(h_ref[...], w_ref[...], preferred_element_type=F32)


def _narrow_job(a, n_steps, step_of):
    flat = a.reshape(-1, a.shape[-1])
    rows = flat.shape[0]
    slab = pl.cdiv(pl.cdiv(rows, n_steps), BF16_SUBLANES) * BF16_SUBLANES
    last = pl.cdiv(rows, slab) - 1
    spec = pl.BlockSpec((slab, flat.shape[1]), lambda *g: (jnp.minimum(step_of(*g), last), 0))
    return flat, spec, jax.ShapeDtypeStruct(flat.shape, BF16)


def _mod_spec(tm, seq_len):
    if seq_len == 1:
        return pl.BlockSpec((1, tm, D_MODEL), lambda i, *_: (0, i, 0))
    return pl.BlockSpec((1, 1, D_MODEL), lambda i, *_: ((i * tm) // seq_len, 0, 0))


def _row_tile(cap, m, seq_len):
    tm = min(cap, m if seq_len == 1 else seq_len)
    assert m % tm == 0 and (seq_len == 1 or seq_len % tm == 0)
    return tm


def _mod_arr(m, seq_len):
    return m[None] if seq_len == 1 else m[:, None, :]


def _inproj(x, shift, scale, g, w_in_r, seq_len):
    m = x.shape[0]
    tm = _row_tile(1024, m, seq_len)
    return pl.pallas_call(
        _inproj_kernel,
        out_shape=jax.ShapeDtypeStruct((m, PROJ_W), F32),
        grid=(m // tm, PROJ_W // IN_TN),
        in_specs=[pl.BlockSpec((tm, D_MODEL), lambda i, j: (i, 0)),
                  _mod_spec(tm, seq_len), _mod_spec(tm, seq_len),
                  pl.BlockSpec((1, D_MODEL), lambda i, j: (0, 0)),
                  pl.BlockSpec((D_MODEL, IN_TN), lambda i, j: (0, j))],
        out_specs=pl.BlockSpec((tm, IN_TN), lambda i, j: (i, j)),
        scratch_shapes=[pltpu.VMEM((tm, D_MODEL), BF16)],
        compiler_params=_cparams(("parallel", "arbitrary")),
        name="in_proj",
    )(x, _mod_arr(shift, seq_len), _mod_arr(scale, seq_len), g.reshape(1, D_MODEL), w_in_r)


def _softplus(x):
    return jnp.maximum(x, 0.0) + jnp.log1p(jnp.exp(-jnp.abs(x)))


def _split3(a):
    hi = a.astype(BF16)
    r = a - hi.astype(F32)
    mid = r.astype(BF16)
    lo = (r - mid.astype(F32)).astype(BF16)
    return hi, mid, lo


def _spread(pieces, ones2):
    hi, mid, lo = pieces
    out = jnp.dot(jnp.concatenate([hi, mid], axis=1), ones2, preferred_element_type=F32)
    if lo is not None:
        out = out + jnp.dot(lo, ones2[0:LANES], preferred_element_type=F32)
    return out


def _ssd_kernel(z_ref, x_ref, b_ref, c_ref, dt_ref,
                cw_ref, cb_ref, dtb_ref, alog_ref, dskip_ref, gn_ref, exp_ref, sel_ref, tri_ref, wide_ref,
                y_ref, st_ref, conv_ref, narrow_ref,
                ext_s, xs_s, xdt_s, eax_s, tex_s, elx_s, acst_s, bm_s, cm_s, state_s):
    ci = pl.program_id(1)
    last_chunk = ci == pl.num_programs(1) - 1
    narrow_ref[...] = wide_ref[...].astype(BF16)

    @pl.when(ci == 0)
    def _():
        ext_s[0:8, :] = jnp.zeros((8, CONV_DIM), F32)
        state_s[...] = jnp.zeros_like(state_s)

    ext_s[8:8 + CHUNK, 0:D_INNER] = x_ref[...]
    ext_s[8:8 + CHUNK, D_INNER:D_INNER + BC_WIDTH] = b_ref[...]
    ext_s[8:8 + CHUNK, D_INNER + BC_WIDTH:CONV_DIM] = c_ref[...]

    def conv(lo, hi):
        acc = cb_ref[:, lo:hi] + cw_ref[3:4, lo:hi] * ext_s[8:8 + CHUNK, lo:hi]
        for j in range(CONV_WIDTH - 1):
            acc = acc + cw_ref[j:j + 1, lo:hi] * ext_s[5 + j:5 + j + CHUNK, lo:hi]
        return _silu(acc)

    xs_s[...] = conv(0, D_INNER)
    bm_s[...] = conv(D_INNER, D_INNER + BC_WIDTH)
    cm_s[...] = conv(D_INNER + BC_WIDTH, CONV_DIM)
    tail = ext_s[5 + CHUNK:8 + CHUNK, :]
    conv_ref[0] = tail
    ext_s[5:8, :] = tail

    dt = _softplus(dt_ref[:, 0:LANES] + dtb_ref[...])
    da = dt * (-jnp.exp(alog_ref[...]))
    tri = tri_ref[...]
    acs = sum(jnp.dot(tri, p, preferred_element_type=F32) for p in _split3(da))
    acst_s[...] = acs.T
    acs_pieces = _split3(acs)
    last = acs[CHUNK - 1:CHUNK, :]
    two = lambda a: _split3(a)[:2] + (None,)
    exp2 = exp_ref[...]
    xdt_s[...] = xs_s[...] * _spread(two(dt), exp2)
    eax_s[...] = _spread(two(jnp.exp(acs)), exp2)
    tex_s[...] = _spread(two(jnp.exp(last - acs)), exp2)
    elx_s[...] = _spread(_split3(jnp.broadcast_to(jnp.exp(last), (8, LANES))), exp2)

    row = lax.broadcasted_iota(I32, (CHUNK, CHUNK), 0)
    col = lax.broadcasted_iota(I32, (CHUNK, CHUNK), 1)
    causal = row >= col
    low_half = col < SSM_HEAD_DIM

    def group(g, carry):
        o512 = pl.multiple_of(g * GROUP_WIDTH, GROUP_WIDTH)
        o128 = pl.multiple_of(g * SSM_STATE, SSM_STATE)
        bg = bm_s[:, pl.ds(o128, SSM_STATE)]
        cg16 = cm_s[:, pl.ds(o128, SSM_STATE)].astype(BF16)
        bg16 = bg.astype(BF16)
        cb = lax.dot_general(cg16, bg16, (((1,), (1,)), ((), ())), preferred_element_type=F32)
        sel2 = sel_ref[:, pl.ds(pl.multiple_of(g * HEADS_PER_GROUP * CHUNK, HEADS_PER_GROUP * CHUNK),
                                HEADS_PER_GROUP * CHUNK)]
        colb = _spread(acs_pieces, sel2)
        rows = acst_s[pl.ds(pl.multiple_of(g * HEADS_PER_GROUP, HEADS_PER_GROUP), HEADS_PER_GROUP), :]
        xdt_g = xdt_s[:, pl.ds(o512, GROUP_WIDTH)]
        parts = []
        for jp in range(HEADS_PER_GROUP // 2):
            ms = []
            for j in (2 * jp, 2 * jp + 1):
                diff = colb[:, j * CHUNK:(j + 1) * CHUNK] - rows[j:j + 1, :]
                dec = jnp.exp(jnp.where(causal, diff, -jnp.inf))
                ms.append((cb * dec).astype(BF16))
            xp = xdt_g[:, jp * LANES:(jp + 1) * LANES]
            rhs = jnp.concatenate([jnp.where(low_half, xp, 0.0), jnp.where(low_half, 0.0, xp)], axis=0)
            parts.append(jnp.dot(jnp.concatenate(ms, axis=1), rhs.astype(BF16), preferred_element_type=F32))
        y = jnp.concatenate(parts, axis=1)

        st = state_s[:, pl.ds(o512, GROUP_WIDTH)]
        y = y + jnp.dot(cg16, st.astype(BF16), preferred_element_type=F32) * eax_s[:, pl.ds(o512, GROUP_WIDTH)]
        xw = (xdt_g * tex_s[:, pl.ds(o512, GROUP_WIDTH)]).astype(BF16)
        st_new = (st * elx_s[0:1, pl.ds(o512, GROUP_WIDTH)]
                  + jnp.dot(bg.T.astype(BF16), xw, preferred_element_type=F32))
        state_s[:, pl.ds(o512, GROUP_WIDTH)] = st_new

        @pl.when(last_chunk)
        def _():
            st_ref[0, pl.ds(o512, GROUP_WIDTH), :] = st_new.T

        y = y + dskip_ref[:, pl.ds(o512, GROUP_WIDTH)] * xs_s[:, pl.ds(o512, GROUP_WIDTH)]
        u = y * _silu(z_ref[:, pl.ds(o512, GROUP_WIDTH)])
        u = u * lax.rsqrt(jnp.mean(u * u, axis=-1, keepdims=True) + EPS)
        y_ref[:, pl.ds(o512, GROUP_WIDTH)] = (u * gn_ref[:, pl.ds(o512, GROUP_WIDTH)]).astype(BF16)
        return carry

    lax.fori_loop(0, SSM_GROUPS, group, 0, unroll=4)


def _pad_heads(v):
    return jnp.pad(v.astype(F32), (0, LANES - SSM_HEADS)).reshape(1, LANES)


def _head_expand(dtype=BF16, copies=2):
    h = jnp.arange(LANES)[:, None]
    ch = jnp.arange(D_INNER)[None, :] // SSM_HEAD_DIM
    return jnp.tile((h == ch).astype(dtype), (copies, 1))


def _head_select():
    h = jnp.arange(LANES)[:, None]
    blk = jnp.arange(SSM_HEADS * CHUNK)[None, :] // CHUNK
    return jnp.tile((h == blk).astype(BF16), (2, 1))


def _ssd_prompt(proj, bsz, seq_len, lw, wide):
    nc = seq_len // CHUNK
    m = bsz * seq_len
    tri = (jnp.arange(CHUNK)[:, None] >= jnp.arange(CHUNK)[None, :]).astype(BF16)
    const = lambda shape: pl.BlockSpec(shape, lambda b, c: (0, 0))
    rowblk = lambda width, col: pl.BlockSpec((CHUNK, width), lambda b, c: (b * nc + c, col // width))
    wide2d, wide_spec, narrow_shape = _narrow_job(wide, bsz * nc, lambda b, c: b * nc + c)
    y, st, conv, narrow = pl.pallas_call(
        _ssd_kernel,
        out_shape=(jax.ShapeDtypeStruct((m, D_INNER), BF16),
                   jax.ShapeDtypeStruct((bsz, D_INNER, SSM_STATE), F32),
                   jax.ShapeDtypeStruct((bsz, CONV_WIDTH - 1, CONV_DIM), F32), narrow_shape),
        grid=(bsz, nc),
        in_specs=[rowblk(D_INNER, PC_Z), rowblk(D_INNER, PC_X), rowblk(BC_WIDTH, PC_B), rowblk(BC_WIDTH, PC_C),
                  rowblk(DT_PAD, PC_DT),
                  const((CONV_WIDTH, CONV_DIM)), const((1, CONV_DIM)), const((1, LANES)), const((1, LANES)),
                  const((1, D_INNER)), const((1, D_INNER)), const((2 * LANES, D_INNER)),
                  const((2 * LANES, SSM_HEADS * CHUNK)), const((CHUNK, CHUNK)), wide_spec],
        out_specs=(pl.BlockSpec((CHUNK, D_INNER), lambda b, c: (b * nc + c, 0)),
                   pl.BlockSpec((1, D_INNER, SSM_STATE), lambda b, c: (b, 0, 0)),
                   pl.BlockSpec((1, CONV_WIDTH - 1, CONV_DIM), lambda b, c: (b, 0, 0)), wide_spec),
        scratch_shapes=[pltpu.VMEM((8 + CHUNK, CONV_DIM), F32),
                        pltpu.VMEM((CHUNK, D_INNER), F32), pltpu.VMEM((CHUNK, D_INNER), F32),
                        pltpu.VMEM((CHUNK, D_INNER), F32), pltpu.VMEM((CHUNK, D_INNER), F32),
                        pltpu.VMEM((8, D_INNER), F32), pltpu.VMEM((LANES, CHUNK), F32),
                        pltpu.VMEM((CHUNK, BC_WIDTH), F32), pltpu.VMEM((CHUNK, BC_WIDTH), F32),
                        pltpu.VMEM((SSM_STATE, D_INNER), F32)],
        compiler_params=_cparams(("arbitrary", "arbitrary")),
        name="ssd_chunks",
    )(proj, proj, proj, proj, proj,
      lw['w_conv'], lw['b_conv'].reshape(1, CONV_DIM), _pad_heads(lw['dt_bias']), _pad_heads(lw['a_log']),
      jnp.repeat(lw['d_skip'].astype(F32), SSM_HEAD_DIM).reshape(1, D_INNER),
      lw['g_ssm_norm'].reshape(1, D_INNER), _head_expand(), _head_select(), tri, wide2d)
    return y, st.reshape(bsz, SSM_HEADS, SSM_HEAD_DIM, SSM_STATE), conv, narrow.reshape(wide.shape)


def _ssd_step_pre_kernel(x_ref, b_ref, c_ref, dt_ref, p0_ref, p1_ref, p2_ref,
                         cw_ref, cb_ref, dtb_ref, alog_ref, exp_ref,
                         xs_ref, xdt_ref, bm_ref, cm_ref, dec_ref, conv_ref):
    def conv(u_ref, lo, hi):
        acc = cb_ref[:, lo:hi] + cw_ref[3:4, lo:hi] * u_ref[...]
        for j, p_ref in enumerate((p0_ref, p1_ref, p2_ref)):
            acc = acc + cw_ref[j:j + 1, lo:hi] * p_ref[:, lo:hi]
        return _silu(acc)

    xs = conv(x_ref, 0, D_INNER)
    xs_ref[...] = xs
    bm_ref[...] = conv(b_ref, D_INNER, D_INNER + BC_WIDTH)
    cm_ref[...] = conv(c_ref, D_INNER + BC_WIDTH, CONV_DIM)
    dt = _softplus(dt_ref[:, 0:LANES] + dtb_ref[...])
    dec_ref[...] = jnp.exp(dt * (-jnp.exp(alog_ref[...])))
    xdt_ref[...] = xs * jnp.dot(dt, exp_ref[...], precision=HIGHEST, preferred_element_type=F32)
    conv_ref[0] = p1_ref[...]
    conv_ref[1] = p2_ref[...]
    conv_ref[2, :, 0:D_INNER] = x_ref[...]
    conv_ref[2, :, D_INNER:D_INNER + BC_WIDTH] = b_ref[...]
    conv_ref[2, :, D_INNER + BC_WIDTH:CONV_DIM] = c_ref[...]


def _ssd_step_kernel(dec_ref, st_ref, xdtt_ref, bm_ref, cmt_ref, xs_ref, z_ref, dskip_ref, gn_ref,
                     sto_ref, y_ref, yt_s, *, tb):
    i = pl.program_id(0)
    nb = xs_ref.shape[0]
    sub = lax.broadcasted_iota(I32, (nb, SSM_STATE), 0)
    lane = lax.broadcasted_iota(I32, (SSM_STATE, nb), 1)

    @pl.when(i == 0)
    def _():
        yt_s[...] = jnp.zeros_like(yt_s)

    for bi in range(tb):
        b = i * tb + bi
        for g in range(SSM_GROUPS):
            r0 = g * GROUP_WIDTH
            b_row = jnp.where(sub == b, bm_ref[:, g * SSM_STATE:(g + 1) * SSM_STATE], 0.0).astype(BF16)
            term = jnp.dot(xdtt_ref[r0:r0 + GROUP_WIDTH, :].astype(BF16), b_row, preferred_element_type=F32)
            news = []
            for j in range(HEADS_PER_GROUP):
                h = g * HEADS_PER_GROUP + j
                rr = r0 + j * SSM_HEAD_DIM
                new = (st_ref[bi, rr:rr + SSM_HEAD_DIM, :] * dec_ref[b, h]
                       + term[j * SSM_HEAD_DIM:(j + 1) * SSM_HEAD_DIM, :])
                sto_ref[bi, rr:rr + SSM_HEAD_DIM, :] = new
                news.append(new)
            new_g = jnp.concatenate(news, axis=0).astype(BF16)
            c_col = jnp.where(lane == b, cmt_ref[g], 0.0).astype(BF16)
            yt_s[r0:r0 + GROUP_WIDTH, :] += jnp.dot(new_g, c_col, preferred_element_type=F32)

    @pl.when(i == pl.num_programs(0) - 1)
    def _():
        for g in range(SSM_GROUPS):
            sl = slice(g * GROUP_WIDTH, (g + 1) * GROUP_WIDTH)
            y = yt_s[sl, :].T + dskip_ref[:, sl] * xs_ref[:, sl]
            u = y * _silu(z_ref[:, sl])
            u = u * lax.rsqrt(jnp.mean(u * u, axis=-1, keepdims=True) + EPS)
            y_ref[:, sl] = (u * gn_ref[:, sl]).astype(BF16)


def _ssd_sample(proj, state_ssm, state_conv, lw):
    nb = proj.shape[0]
    full = lambda shape: pl.BlockSpec(shape, lambda *_: tuple(0 for _ in shape))
    colblk = lambda width, col: pl.BlockSpec((nb, width), lambda *_: (0, col // width))
    prev = [state_conv[:, j, :] for j in range(CONV_WIDTH - 1)]
    tr = min(32, nb)
    rows = lambda width, col=0: pl.BlockSpec((tr, width), lambda i: (i, col // width))
    xs, xdt, bm, cm, dec, conv = pl.pallas_call(
        _ssd_step_pre_kernel,
        out_shape=(jax.ShapeDtypeStruct((nb, D_INNER), F32), jax.ShapeDtypeStruct((nb, D_INNER), F32),
                   jax.ShapeDtypeStruct((nb, BC_WIDTH), F32), jax.ShapeDtypeStruct((nb, BC_WIDTH), F32),
                   jax.ShapeDtypeStruct((nb, LANES), F32),
                   jax.ShapeDtypeStruct((CONV_WIDTH - 1, nb, CONV_DIM), F32)),
        grid=(nb // tr,),
        in_specs=[rows(D_INNER, PC_X), rows(BC_WIDTH, PC_B), rows(BC_WIDTH, PC_C), rows(DT_PAD, PC_DT),
                  rows(CONV_DIM), rows(CONV_DIM), rows(CONV_DIM),
                  full((CONV_WIDTH, CONV_DIM)), full((1, CONV_DIM)), full((1, LANES)), full((1, LANES)),
                  full((LANES, D_INNER))],
        out_specs=(rows(D_INNER), rows(D_INNER), rows(BC_WIDTH), rows(BC_WIDTH), rows(LANES),
                   pl.BlockSpec((CONV_WIDTH - 1, tr, CONV_DIM), lambda i: (0, i, 0))),
        compiler_params=_cparams(("parallel",)),
        name="ssd_step_pre",
    )(proj, proj, proj, proj, *prev,
      lw['w_conv'], lw['b_conv'].reshape(1, CONV_DIM), _pad_heads(lw['dt_bias']), _pad_heads(lw['a_log']),
      _head_expand(F32, 1))

    tb = 2
    cmt = cm.reshape(nb, SSM_GROUPS, SSM_STATE).transpose(1, 2, 0)
    st_new, y = pl.pallas_call(
        functools.partial(_ssd_step_kernel, tb=tb),
        out_shape=(jax.ShapeDtypeStruct((nb, D_INNER, SSM_STATE), F32),
                   jax.ShapeDtypeStruct((nb, D_INNER), BF16)),
        grid=(nb // tb,),
        in_specs=[pl.BlockSpec(memory_space=pltpu.SMEM),
                  pl.BlockSpec((tb, D_INNER, SSM_STATE), lambda i: (i, 0, 0)),
                  full((D_INNER, nb)), full((nb, BC_WIDTH)), full((SSM_GROUPS, SSM_STATE, nb)),
                  full((nb, D_INNER)), colblk(D_INNER, PC_Z), full((1, D_INNER)), full((1, D_INNER))],
        out_specs=(pl.BlockSpec((tb, D_INNER, SSM_STATE), lambda i: (i, 0, 0)),
                   full((nb, D_INNER))),
        scratch_shapes=[pltpu.VMEM((D_INNER, nb), F32)],
        compiler_params=_cparams(("arbitrary",)),
        name="ssd_step",
    )(dec, state_ssm.reshape(nb, D_INNER, SSM_STATE), xdt.T, bm, cmt, xs, proj,
      jnp.repeat(lw['d_skip'].astype(F32), SSM_HEAD_DIM).reshape(1, D_INNER),
      lw['g_ssm_norm'].reshape(1, D_INNER))
    return (y, st_new.reshape(nb, SSM_HEADS, SSM_HEAD_DIM, SSM_STATE), conv.transpose(1, 0, 2))


def _rope_tables(pos):
    half = ROPE_DIM // 2
    inv_freq = ROPE_THETA ** (-jnp.arange(half, dtype=F32) * 2.0 / ROPE_DIM)
    ang = pos.astype(F32)[:, None] * inv_freq[None, :]
    cos, sin = jnp.cos(ang), jnp.sin(ang)
    n = pos.shape[0]
    ones = jnp.ones((n, HEAD_DIM - ROPE_DIM), F32)
    zeros = jnp.zeros((n, HEAD_DIM - ROPE_DIM), F32)
    zh = jnp.zeros((n, half), F32)
    cos_f = jnp.concatenate([cos, cos, ones], axis=1)
    sin_a = jnp.concatenate([zh, sin, zeros], axis=1)
    sin_b = jnp.concatenate([-sin, zh, zeros], axis=1)
    tile = lambda t: jnp.tile(t, (1, LANES // HEAD_DIM))
    return tile(cos_f), tile(sin_a), tile(sin_b)


def _rope_slab(x, cos_f, sin_a, sin_b):
    half = ROPE_DIM // 2
    return x * cos_f + pltpu.roll(x, half, 1) * sin_a + pltpu.roll(x, LANES - half, 1) * sin_b


def _attn_kernel(sink_ref, q_ref, k_ref, v_ref, cos_ref, sin_a_ref, sin_b_ref, wide_ref,
                 y_ref, kn_ref, vn_ref, narrow_ref, kspan_s, vspan_s, sc_s, p_s):
    bi = pl.program_id(1)
    nq = q_ref.shape[0]
    narrow_ref[...] = wide_ref[...].astype(BF16)

    @pl.when(bi == 0)
    def _():
        kspan_s[0:nq, :] = jnp.zeros((nq, KV_WIDTH), F32)
        vspan_s[0:nq, :] = jnp.zeros((nq, KV_WIDTH), F32)

    cos_f, sin_a, sin_b = cos_ref[...], sin_a_ref[...], sin_b_ref[...]
    rope = lambda x: _rope_slab(x, cos_f, sin_a, sin_b)
    kr = jnp.concatenate([rope(k_ref[:, s * LANES:(s + 1) * LANES]) for s in range(KV_WIDTH // LANES)], axis=1)
    v = v_ref[...]
    kspan_s[nq:2 * nq, :] = kr
    vspan_s[nq:2 * nq, :] = v
    kn_ref[0] = kr
    vn_ref[0] = v

    t = lax.broadcasted_iota(I32, (nq, nq), 0)
    j = lax.broadcasted_iota(I32, (nq, nq), 1)
    own_block = j <= t
    low = lax.broadcasted_iota(I32, (2 * nq, LANES), 1) < HEAD_DIM
    scale = HEAD_DIM ** -0.5
    grp = ATTN_HEADS // KV_HEADS

    def block_diag(span_ref, kh):
        nat = span_ref[:, (kh // 2) * LANES:(kh // 2 + 1) * LANES]
        swp = pltpu.roll(nat, HEAD_DIM, 1)
        lo, hi = (nat, swp) if kh % 2 == 0 else (swp, nat)
        return jnp.concatenate([jnp.where(low, lo, 0.0), jnp.where(low, 0.0, hi)], axis=0).astype(BF16)

    for kh in range(KV_HEADS):
        kbd = block_diag(kspan_s, kh)
        for p in range(grp // 2):
            h0 = kh * grp + 2 * p
            qp = (rope(q_ref[:, h0 * HEAD_DIM:(h0 + 2) * HEAD_DIM]) * scale).astype(BF16)
            sc = lax.dot_general(qp, kbd, (((1,), (1,)), ((), ())), preferred_element_type=F32)
            for a in range(2):
                prev = sc[:, 2 * a * nq:(2 * a + 1) * nq]
                cur = sc[:, (2 * a + 1) * nq:(2 * a + 2) * nq]
                sc_s[h0 + a] = jnp.where(own_block, cur, prev)

    have_prev = bi > 0
    for h in range(ATTN_HEADS):
        sa = sc_s[h]
        sa = jnp.where(own_block | have_prev, sa, -jnp.inf)
        sink = sink_ref[h]
        mx = jnp.maximum(jnp.max(sa, axis=-1, keepdims=True), sink)
        pa = jnp.exp(sa - mx)
        pa = pa / (jnp.sum(pa, axis=-1, keepdims=True) + jnp.exp(sink - mx))
        p_s[h] = pa.astype(BF16)

    zero = jnp.zeros((nq, nq), BF16)
    for kh in range(KV_HEADS):
        vbd = block_diag(vspan_s, kh)
        for p in range(grp // 2):
            h0 = kh * grp + 2 * p
            parts = []
            for a in range(2):
                pa = p_s[h0 + a]
                parts += [jnp.where(own_block, zero, pa), jnp.where(own_block, pa, zero)]
            y_ref[:, h0 * HEAD_DIM:(h0 + 2) * HEAD_DIM] = jnp.dot(
                jnp.concatenate(parts, axis=1), vbd, preferred_element_type=F32).astype(BF16)

    kspan_s[0:nq, :] = kspan_s[nq:2 * nq, :]
    vspan_s[0:nq, :] = vspan_s[nq:2 * nq, :]


def _attn_prompt(proj, bsz, seq_len, sinks, wide):
    nq = CHUNK
    nb = seq_len // nq
    m = bsz * seq_len
    cos_f, sin_a, sin_b = _rope_tables(jnp.arange(seq_len))
    rowblk = lambda width, col: pl.BlockSpec((nq, width), lambda b, i: (b * nb + i, col // width))
    tab = pl.BlockSpec((nq, LANES), lambda b, i: (i, 0))
    wide2d, wide_spec, narrow_shape = _narrow_job(wide, bsz * nb, lambda b, i: b * nb + i)
    y, kn, vn, narrow = pl.pallas_call(
        _attn_kernel,
        out_shape=(jax.ShapeDtypeStruct((m, D_MODEL), BF16),
                   jax.ShapeDtypeStruct((bsz, WINDOW, KV_WIDTH), F32),
                   jax.ShapeDtypeStruct((bsz, WINDOW, KV_WIDTH), F32), narrow_shape),
        grid=(bsz, nb),
        in_specs=[pl.BlockSpec(memory_space=pltpu.SMEM),
                  rowblk(D_MODEL, PC_Q), rowblk(KV_WIDTH, PC_K), rowblk(KV_WIDTH, PC_V), tab, tab, tab,
                  wide_spec],
        out_specs=(pl.BlockSpec((nq, D_MODEL), lambda b, i: (b * nb + i, 0)),
                   pl.BlockSpec((1, WINDOW, KV_WIDTH), lambda b, i: (b, 0, 0)),
                   pl.BlockSpec((1, WINDOW, KV_WIDTH), lambda b, i: (b, 0, 0)), wide_spec),
        scratch_shapes=[pltpu.VMEM((2 * nq, KV_WIDTH), F32), pltpu.VMEM((2 * nq, KV_WIDTH), F32),
                        pltpu.VMEM((ATTN_HEADS, nq, nq), F32), pltpu.VMEM((ATTN_HEADS, nq, nq), BF16)],
        compiler_params=_cparams(("arbitrary", "arbitrary")),
        name="attn_blocks",
    )(sinks.astype(F32), proj, proj, proj, cos_f, sin_a, sin_b, wide2d)
    shape = (bsz, WINDOW, KV_HEADS, HEAD_DIM)
    return y, kn.reshape(shape), vn.reshape(shape), narrow.reshape(wide.shape)


def _attn_step_kernel(sink_ref, q_ref, k_ref, v_ref, cos_ref, sin_a_ref, sin_b_ref, fold_ref, foldt_ref,
                      kc_ref, vc_ref, y_ref, ko_ref, vo_ref, *, tb):
    cos_f, sin_a, sin_b = cos_ref[...], sin_a_ref[...], sin_b_ref[...]
    rope = lambda ref, w: jnp.concatenate(
        [_rope_slab(ref[:, s * LANES:(s + 1) * LANES], cos_f, sin_a, sin_b) for s in range(w // LANES)], axis=1)
    qr = rope(q_ref, D_MODEL) * (HEAD_DIM ** -0.5)
    kr = rope(k_ref, KV_WIDTH)
    v = v_ref[...]

    own = (lax.broadcasted_iota(I32, (ATTN_HEADS, D_MODEL), 1) // HEAD_DIM
           == lax.broadcasted_iota(I32, (ATTN_HEADS, D_MODEL), 0))
    qb = jnp.concatenate([jnp.where(own, jnp.broadcast_to(qr[bi:bi + 1], (ATTN_HEADS, D_MODEL)), 0.0)
                          for bi in range(tb)], axis=0).astype(BF16)
    qm = jnp.dot(qb, fold_ref[...], preferred_element_type=F32).astype(BF16)

    row = lax.broadcasted_iota(I32, (WINDOW, KV_WIDTH), 0)
    w = lax.broadcasted_iota(I32, (ATTN_HEADS, WINDOW), 1)
    sink = sink_ref[...]
    bf = lambda x: x.astype(BF16)
    outs = []
    for bi in range(tb):
        qmb = qm[bi * ATTN_HEADS:(bi + 1) * ATTN_HEADS]
        kc, vc = kc_ref[bi], vc_ref[bi]
        kn, vn = kr[bi:bi + 1], v[bi:bi + 1]
        sc = lax.dot_general(qmb, bf(kc), (((1,), (1,)), ((), ())), preferred_element_type=F32)
        sc = jnp.where(w >= 1, sc, -jnp.inf)
        s_new = jnp.sum(qmb.astype(F32) * bf(kn).astype(F32), axis=-1, keepdims=True)
        mx = jnp.maximum(jnp.maximum(jnp.max(sc, axis=-1, keepdims=True), s_new), sink)
        p = jnp.exp(sc - mx)
        p_new = jnp.exp(s_new - mx)
        den = jnp.sum(p, axis=-1, keepdims=True) + p_new + jnp.exp(sink - mx)
        p = p / den
        p_new = p_new / den
        out = jnp.dot(bf(p), bf(vc), preferred_element_type=F32) + bf(p_new).astype(F32) * bf(vn).astype(F32)
        outs.append(bf(out))
        ko_ref[bi] = jnp.where(row == WINDOW - 1, kn, pltpu.roll(kc, WINDOW - 1, 0))
        vo_ref[bi] = jnp.where(row == WINDOW - 1, vn, pltpu.roll(vc, WINDOW - 1, 0))

    of = jnp.dot(jnp.concatenate(outs, axis=0), foldt_ref[...], preferred_element_type=F32)
    of = of.reshape(tb, ATTN_HEADS, D_MODEL)
    y_ref[...] = jnp.sum(jnp.where(own[None], of, 0.0), axis=1).astype(BF16)


def _attn_sample(proj, cache_k, cache_v, sinks):
    nb = proj.shape[0]
    tb = 8
    grp = ATTN_HEADS // KV_HEADS
    cos_f, sin_a, sin_b = _rope_tables(jnp.full((1,), PAST_LEN))
    c = jnp.arange(D_MODEL)[:, None]
    l = jnp.arange(KV_WIDTH)[None, :]
    fold = ((c % HEAD_DIM == l % HEAD_DIM) & ((c // HEAD_DIM) // grp == l // HEAD_DIM)).astype(BF16)
    full = lambda shape: pl.BlockSpec(shape, lambda i: tuple(0 for _ in shape))
    rowblk = lambda width, col: pl.BlockSpec((tb, width), lambda i: (i, col // width))
    win = pl.BlockSpec((tb, WINDOW, KV_WIDTH), lambda i: (i, 0, 0))
    y, kn, vn = pl.pallas_call(
        functools.partial(_attn_step_kernel, tb=tb),
        out_shape=(jax.ShapeDtypeStruct((nb, D_MODEL), BF16),
                   jax.ShapeDtypeStruct((nb, WINDOW, KV_WIDTH), F32),
                   jax.ShapeDtypeStruct((nb, WINDOW, KV_WIDTH), F32)),
        grid=(nb // tb,),
        in_specs=[full((ATTN_HEADS, 1)), rowblk(D_MODEL, PC_Q), rowblk(KV_WIDTH, PC_K), rowblk(KV_WIDTH, PC_V),
                  full((1, LANES)), full((1, LANES)), full((1, LANES)),
                  full((D_MODEL, KV_WIDTH)), full((KV_WIDTH, D_MODEL)), win, win],
        out_specs=(pl.BlockSpec((tb, D_MODEL), lambda i: (i, 0)), win, win),
        compiler_params=_cparams(("parallel",)),
        name="attn_step",
    )(sinks.astype(F32).reshape(ATTN_HEADS, 1), proj, proj, proj, cos_f, sin_a, sin_b, fold, fold.T,
      cache_k.reshape(nb, WINDOW, KV_WIDTH), cache_v.reshape(nb, WINDOW, KV_WIDTH))
    shape = (nb, WINDOW, KV_HEADS, HEAD_DIM)
    return y, kn.reshape(shape), vn.reshape(shape)


def _merge_kernel(ys_ref, ya_ref, ws_ref, wa_ref, gs_ref, ga_ref, o_ref):
    a = jnp.dot(ys_ref[...], ws_ref[...], preferred_element_type=F32)
    b = jnp.dot(ya_ref[...], wa_ref[...], preferred_element_type=F32)
    o_ref[...] = (jax.nn.sigmoid(gs_ref[...]) * a + jax.nn.sigmoid(ga_ref[...]) * b).astype(BF16)


def _merge(y_ssd, y_att, proj, w_ssd, w_att):
    m = y_ssd.shape[0]
    tm = min(512, m)
    tn = 512
    g0 = PC_G // tn
    return pl.pallas_call(
        _merge_kernel,
        out_shape=jax.ShapeDtypeStruct((m, D_MODEL), BF16),
        grid=(m // tm, D_MODEL // tn),
        in_specs=[pl.BlockSpec((tm, D_INNER), lambda i, j: (i, 0)),
                  pl.BlockSpec((tm, D_MODEL), lambda i, j: (i, 0)),
                  pl.BlockSpec((D_INNER, tn), lambda i, j: (0, j)),
                  pl.BlockSpec((D_MODEL, tn), lambda i, j: (0, j)),
                  pl.BlockSpec((tm, tn), lambda i, j: (i, g0 + j)),
                  pl.BlockSpec((tm, tn), lambda i, j: (i, g0 + D_MODEL // tn + j))],
        out_specs=pl.BlockSpec((tm, tn), lambda i, j: (i, j)),
        compiler_params=_cparams(("parallel", "arbitrary")),
        name="branch_merge",
    )(y_ssd, y_att, w_ssd, w_att, proj, proj)


def _outproj_kernel(pre_ref, x_ref, gate_ref, sh_ref, sc_ref, g_ref, w_ref, wr_ref, br_ref, before_ref, c0_ref,
                    x1_ref, h2_ref, tg_ref, ti_ref, tr_ref, cnt_ref, cnt_s):
    @pl.when(pl.program_id(0) == 0)
    def _():
        cnt_s[...] = c0_ref[...]

    mixed = jnp.dot(pre_ref[...], w_ref[...], preferred_element_type=F32)
    x1 = x_ref[...] + gate_ref[0] * mixed
    x1_ref[...] = x1
    y = x1 * lax.rsqrt(jnp.mean(x1 * x1, axis=-1, keepdims=True) + EPS) * g_ref[...]
    h2 = y * (1.0 + sc_ref[0]) + sh_ref[0]
    h2_ref[...] = h2
    h_hi, h_mid, _ = _split3(h2)
    hw = jnp.dot(h_hi, wr_ref[...], preferred_element_type=F32)
    logits = (hw[:, 0:LANES] + hw[:, LANES:2 * LANES]
              + jnp.dot(h_mid, wr_ref[:, 0:LANES], preferred_element_type=F32) + br_ref[...])
    lane = lax.broadcasted_iota(I32, logits.shape, 1)
    lane_f = lane.astype(F32)
    vals = jnp.zeros_like(logits)
    idxs = jnp.zeros_like(logits)
    top = None
    den = jnp.zeros((logits.shape[0], 1), F32)
    chosen = []
    for k in range(TOP_K):
        mx = jnp.max(logits, axis=-1, keepdims=True)
        ix = jnp.min(jnp.where(logits == mx, lane_f, float(LANES)), axis=-1, keepdims=True)
        top = mx if top is None else top
        e = jnp.exp(mx - top)
        den = den + e
        vals = jnp.where(lane == k, e, vals)
        idxs = jnp.where(lane == k, ix, idxs)
        chosen.append(lane_f == ix)
        logits = jnp.where(chosen[-1], -jnp.inf, logits)
    tg_ref[...] = vals / den
    ti_ref[...] = idxs.astype(I32)

    picked = jnp.zeros_like(vals)
    for c in chosen:
        picked = jnp.where(c, 1.0, picked)
    ahead = jnp.dot(before_ref[...], picked.astype(BF16), preferred_element_type=F32) + cnt_s[...]
    ranks = jnp.zeros_like(vals)
    for k, c in enumerate(chosen):
        ranks = jnp.where(lane == k, jnp.sum(jnp.where(c, ahead, 0.0), axis=-1, keepdims=True), ranks)
    tr_ref[...] = ranks.astype(I32)
    cnt_s[...] += jnp.sum(picked, axis=0, keepdims=True)
    cnt_ref[...] = cnt_s[...]


def _outproj(pre, x, gate1, shift2, scale2, g2, w_out, w_router, b_router, seq_len, counts0):
    m = x.shape[0]
    tm = _row_tile(512, m, seq_len)
    before = (jnp.arange(tm)[None, :] < jnp.arange(tm)[:, None]).astype(BF16)
    w_hi, w_mid, _ = _split3(jnp.pad(w_router.astype(F32), ((0, 0), (0, LANES - N_EXPERTS))))
    wr = jnp.concatenate([w_hi, w_mid], axis=1)
    br = jnp.pad(b_router.astype(F32), (0, LANES - N_EXPERTS), constant_values=-jnp.inf).reshape(1, LANES)
    row = lambda width: pl.BlockSpec((tm, width), lambda i: (i, 0))
    const = lambda shape: pl.BlockSpec(shape, lambda i: (0, 0))
    return pl.pallas_call(
        _outproj_kernel,
        out_shape=(jax.ShapeDtypeStruct((m, D_MODEL), F32), jax.ShapeDtypeStruct((m, D_MODEL), F32),
                   jax.ShapeDtypeStruct((m, LANES), F32), jax.ShapeDtypeStruct((m, LANES), I32),
                   jax.ShapeDtypeStruct((m, LANES), I32), jax.ShapeDtypeStruct((1, LANES), F32)),
        grid=(m // tm,),
        in_specs=[row(D_MODEL), row(D_MODEL), _mod_spec(tm, seq_len), _mod_spec(tm, seq_len), _mod_spec(tm, seq_len),
                  const((1, D_MODEL)), const((D_MODEL, D_MODEL)), const((D_MODEL, 2 * LANES)), const((1, LANES)),
                  const((tm, tm)), const((1, LANES))],
        out_specs=(row(D_MODEL), row(D_MODEL), row(LANES), row(LANES), row(LANES), const((1, LANES))),
        scratch_shapes=[pltpu.VMEM((1, LANES), F32)],
        compiler_params=_cparams(("arbitrary",)),
        name="out_proj_router",
    )(pre, x, _mod_arr(gate1, seq_len), _mod_arr(shift2, seq_len), _mod_arr(scale2, seq_len),
      g2.reshape(1, D_MODEL), w_out, wr, br, before, counts0)


def _ffn_kernel(be_ref, na_ref, tok_hbm, h_hbm, w1g_ref, w1l_ref, b1g_ref, b1l_ref, w2_ref, b2_ref,
                o_ref, tok_s, rows_s, xb_s, tok_sem, row_sem):
    b = pl.program_id(0)
    f = pl.program_id(1)
    n_blocks = pl.num_programs(0)
    nf = D_FF // MOE_TF
    rows_per_step = MOE_BLK // nf
    n_active = na_ref[0]
    active = b < n_active
    nxt = jnp.minimum(b + 1, n_blocks - 1)
    nxt_slot = (b + 1) % 2

    def tok_copy(blk, sl):
        return pltpu.make_async_copy(tok_hbm.at[blk], tok_s.at[sl], tok_sem.at[sl])

    def row_copy(tok, r):
        return pltpu.make_async_copy(h_hbm.at[pl.ds(tok, 1)], rows_s.at[pl.ds(r, 1)], row_sem)

    def wait_rows():
        for r in range(MOE_BLK):
            row_copy(0, r).wait()

    @pl.when(active & (f == 0))
    def _():
        @pl.when(b == 0)
        def _():
            first = tok_copy(0, 0)
            first.start()
            first.wait()

            def body(r, c):
                row_copy(tok_s[0, r], r).start()
                return c
            lax.fori_loop(0, MOE_BLK, body, 0, unroll=8)
            tok_copy(nxt, 1).start()

        wait_rows()
        xb_s[...] = rows_s[...].astype(BF16)
        o_ref[...] = jnp.broadcast_to(b2_ref[0], o_ref.shape)
        tok_copy(nxt, nxt_slot).wait()

    @pl.when(active)
    def _():
        base = f * rows_per_step
        for i in range(rows_per_step):
            row_copy(tok_s[nxt_slot, base + i], base + i).start()
        xb = xb_s[...]
        glu = jnp.dot(xb, w1g_ref[0], preferred_element_type=F32) + b1g_ref[0]
        lin = jnp.dot(xb, w1l_ref[0], preferred_element_type=F32) + b1l_ref[0]
        glu = jnp.minimum(glu, SWIGLU_LIMIT)
        lin = jnp.clip(lin, -SWIGLU_LIMIT, SWIGLU_LIMIT)
        act = glu * jax.nn.sigmoid(SWIGLU_ALPHA * glu) * (lin + 1.0)
        o_ref[...] += jnp.dot(act.astype(BF16), w2_ref[0], preferred_element_type=F32)

        @pl.when(f == nf - 1)
        def _():
            tok_copy(jnp.minimum(b + 2, n_blocks - 1), b % 2).start()

    @pl.when(((b == n_active) & (f == 0)) | (active & (b == n_blocks - 1) & (f == nf - 1)))
    def _():
        wait_rows()
        tok_copy(0, jnp.where(active, b, b + 1) % 2).wait()

    @pl.when((b >= n_active) & (f == nf - 1))
    def _():
        o_ref[...] = jnp.zeros_like(o_ref)


def _ffn(h_all, slot_tok, block_expert, n_active, w1, b1, w2, b2):
    n_blocks = slot_tok.shape[0]
    nf = D_FF // MOE_TF

    def widx(col_off):
        def index_map(b, f, be, na):
            live = b < na[0]
            bb = jnp.where(live, b, na[0] - 1)
            ff = jnp.where(live, f, nf - 1)
            return be[bb], 0, col_off + ff
        return index_map

    def w2idx(b, f, be, na):
        live = b < na[0]
        return be[jnp.where(live, b, na[0] - 1)], jnp.where(live, f, nf - 1), 0

    def b2idx(b, f, be, na):
        return be[jnp.where(b < na[0], b, na[0] - 1)], 0, 0

    grid_spec = pltpu.PrefetchScalarGridSpec(
        num_scalar_prefetch=2,
        grid=(n_blocks, nf),
        in_specs=[pl.BlockSpec(memory_space=pl.ANY), pl.BlockSpec(memory_space=pl.ANY),
                  pl.BlockSpec((1, D_MODEL, MOE_TF), widx(0)), pl.BlockSpec((1, D_MODEL, MOE_TF), widx(nf)),
                  pl.BlockSpec((1, 1, MOE_TF), widx(0)), pl.BlockSpec((1, 1, MOE_TF), widx(nf)),
                  pl.BlockSpec((1, MOE_TF, D_MODEL), w2idx), pl.BlockSpec((1, 1, D_MODEL), b2idx)],
        out_specs=pl.BlockSpec((MOE_BLK, D_MODEL), lambda b, f, be, na: (b, 0)),
        scratch_shapes=[pltpu.SMEM((2, MOE_BLK), I32),
                        pltpu.VMEM((MOE_BLK, D_MODEL), F32),
                        pltpu.VMEM((MOE_BLK, D_MODEL), BF16),
                        pltpu.SemaphoreType.DMA((2,)), pltpu.SemaphoreType.DMA])
    return pl.pallas_call(
        _ffn_kernel,
        out_shape=jax.ShapeDtypeStruct((n_blocks * MOE_BLK, D_MODEL), F32),
        grid_spec=grid_spec,
        compiler_params=_cparams(("arbitrary", "arbitrary")),
        name="expert_ffn",
    )(block_expert, n_active, slot_tok, h_all, w1, w1, b1[:, None, :], b1[:, None, :], w2, b2[:, None, :])


def _route(top_idx, rank, counts):
    n_tok = top_idx.shape[0]
    n_assign = n_tok * TOP_K
    e_flat = top_idx.reshape(-1)
    rank = rank.reshape(-1)
    onehot = (e_flat[:, None] == jnp.arange(N_EXPERTS)[None, :]).astype(I32)
    nblk = (counts + MOE_BLK - 1) // MOE_BLK
    blk_end = jnp.cumsum(nblk)
    pad_start = (blk_end - nblk) * MOE_BLK
    dest = jnp.sum(onehot * pad_start[None, :], axis=1) + rank
    n_blocks = -(-n_assign // MOE_BLK) + N_EXPERTS
    slot_tok = jnp.zeros((n_blocks * MOE_BLK,), I32).at[dest].set(
        jnp.arange(n_assign, dtype=I32) // TOP_K, unique_indices=True)
    block_expert = jnp.minimum(jnp.sum(blk_end[None, :] <= jnp.arange(n_blocks)[:, None], axis=1),
                               N_EXPERTS - 1).astype(I32)
    n_active = blk_end[-1:].astype(I32)
    return dest.reshape(n_tok, TOP_K).astype(I32), slot_tok.reshape(n_blocks, MOE_BLK), block_expert, n_active


def _combine_kernel(dest_hbm, y_hbm, x1_ref, tg_ref, gate_ref, gf_ref, o_ref, dest_s, rows_s, dest_sem, row_sem,
                    *, tc):
    i = pl.program_id(0)
    n = pl.num_programs(0)
    n_rows = TOP_K * tc
    slot = i % 2
    nxt = jnp.minimum(i + 1, n - 1)

    def dest_copy(tile, sl):
        return pltpu.make_async_copy(dest_hbm.at[tile], dest_s.at[sl], dest_sem.at[sl])

    def row_copy(src, r, sl):
        return pltpu.make_async_copy(y_hbm.at[pl.ds(src, 1)], rows_s.at[sl, pl.ds(r, 1)], row_sem.at[sl])

    def wait_rows(sl):
        for r in range(n_rows):
            row_copy(0, r, sl).wait()

    @pl.when(i == 0)
    def _():
        first = dest_copy(0, 0)
        first.start()
        first.wait()

        def body(r, c):
            row_copy(dest_s[0, r], r, 0).start()
            return c
        lax.fori_loop(0, n_rows, body, 0, unroll=8)
        dest_copy(nxt, 1).start()

    def step(cur):
        dest_copy(nxt, 1 - cur).wait()
        wait_rows(cur)
        for r in range(n_rows):
            row_copy(dest_s[1 - cur, r], r, 1 - cur).start()
        tg = tg_ref[...]
        ffn = tg[:, 0:1] * rows_s[cur, 0:tc, :]
        for k in range(1, TOP_K):
            ffn = ffn + tg[:, k:k + 1] * rows_s[cur, k * tc:(k + 1) * tc, :]
        x2 = x1_ref[...] + gate_ref[0] * ffn
        o_ref[...] = x2 * lax.rsqrt(jnp.mean(x2 * x2, axis=-1, keepdims=True) + EPS) * gf_ref[...]
        dest_copy(jnp.minimum(i + 2, n - 1), cur).start()

        @pl.when(i == n - 1)
        def _():
            wait_rows(1 - cur)
            dest_copy(0, cur).wait()

    for cur in range(2):
        pl.when(slot == cur)(functools.partial(step, cur))


def _combine(dest, y_sorted, x1, top_gate, gate2, g_final, seq_len):
    m = x1.shape[0]
    tc = _row_tile(256, m, seq_len)
    dest_t = dest.reshape(m // tc, tc, TOP_K).transpose(0, 2, 1).reshape(m // tc, TOP_K * tc)
    row = lambda width: pl.BlockSpec((tc, width), lambda i: (i, 0))
    return pl.pallas_call(
        functools.partial(_combine_kernel, tc=tc),
        out_shape=jax.ShapeDtypeStruct((m, D_MODEL), F32),
        grid=(m // tc,),
        in_specs=[pl.BlockSpec(memory_space=pl.ANY), pl.BlockSpec(memory_space=pl.ANY),
                  row(D_MODEL), row(LANES), _mod_spec(tc, seq_len), pl.BlockSpec((1, D_MODEL), lambda i: (0, 0))],
        out_specs=row(D_MODEL),
        scratch_shapes=[pltpu.SMEM((2, TOP_K * tc), I32), pltpu.VMEM((2, TOP_K * tc, D_MODEL), F32),
                        pltpu.SemaphoreType.DMA((2,)), pltpu.SemaphoreType.DMA((2,))],
        compiler_params=_cparams(("arbitrary",)),
        name="moe_combine",
    )(dest_t, y_sorted, x1, top_gate, _mod_arr(gate2, seq_len), g_final.reshape(1, D_MODEL))


def _reorder_w_in(w_in):
    w = w_in.astype(BF16)
    dt_cols = jnp.pad(w[:, XBC_END:DT_END], ((0, 0), (0, DT_PAD - SSM_HEADS)))
    return jnp.concatenate([w[:, :XBC_END], w[:, DT_END:], dt_cols], axis=1)


def _mixer_tail(x, y_ssd, y_att, proj, mod, lw, seq_len, counts0):
    pre = _merge(y_ssd, y_att, proj, lw['w_br_ssd'], lw['w_br_att'])
    return _outproj(pre, x, mod[2], mod[3], mod[4], lw['g_norm2'], lw['w_out'], lw['w_router'], lw['b_router'],
                    seq_len, counts0)


def _forward(x_prompt, x_sample, state_ssm, state_conv, cache_win_k, cache_win_v, c_prompt, c_sample, lw, g_final):
    bp, lp, d = x_prompt.shape
    bs = x_sample.shape[0]
    mp = bp * lp
    xp = x_prompt.reshape(mp, d)
    xs = x_sample.reshape(bs, d)

    mod = _ada(jnp.concatenate([c_prompt, c_sample], axis=0), lw['w_ada'], lw['b_ada'])
    mod_p = [mod[:bp, i * d:(i + 1) * d] for i in range(N_MOD)]
    mod_s = [mod[bp:, i * d:(i + 1) * d] for i in range(N_MOD)]

    proj_p = _inproj(xp, mod_p[0], mod_p[1], lw['g_norm1'], lw['w_in'], lp)
    proj_s = _inproj(xs, mod_s[0], mod_s[1], lw['g_norm1'], lw['w_in'], 1)

    yssd_p, ssm_p, conv_p, w_e_out = _ssd_prompt(proj_p, bp, lp, lw, lw['w_expert_out'])
    yatt_p, wk_p, wv_p, w_e_in = _attn_prompt(proj_p, bp, lp, lw['sinks'], lw['w_expert_in'])
    yssd_s, ssm_s, conv_s = _ssd_sample(proj_s, state_ssm, state_conv, lw)
    yatt_s, wk_s, wv_s = _attn_sample(proj_s, cache_win_k, cache_win_v, lw['sinks'])

    x1_p, h2_p, tg_p, ti_p, tr_p, cnt_p = _mixer_tail(xp, yssd_p, yatt_p, proj_p, mod_p, lw, lp,
                                                      jnp.zeros((1, LANES), F32))
    x1_s, h2_s, tg_s, ti_s, tr_s, cnt_all = _mixer_tail(xs, yssd_s, yatt_s, proj_s, mod_s, lw, 1, cnt_p)

    h_all = jnp.concatenate([h2_p, h2_s], axis=0)
    top_idx = jnp.concatenate([ti_p[:, :TOP_K], ti_s[:, :TOP_K]], axis=0)
    rank = jnp.concatenate([tr_p[:, :TOP_K], tr_s[:, :TOP_K]], axis=0)
    dest, slot_tok, block_expert, n_active = _route(top_idx, rank, cnt_all[0, :N_EXPERTS].astype(I32))
    y_sorted = _ffn(h_all, slot_tok, block_expert, n_active,
                    w_e_in, lw['b_expert_in'], w_e_out, lw['b_expert_out'])

    y_p = _combine(dest[:mp], y_sorted, x1_p, tg_p, mod_p[5], g_final, lp)
    y_s = _combine(dest[mp:], y_sorted, x1_s, tg_s, mod_s[5], g_final, 1)
    return (y_p.reshape(bp, lp, d), y_s.reshape(bs, 1, d),
            ssm_p[None], conv_p[None], wk_p[None], wv_p[None],
            ssm_s[None], conv_s[None], wk_s[None], wv_s[None])


def kernel(x_prompt, x_sample, state_ssm, state_conv, cache_win_k, cache_win_v, c_prompt, c_sample, w_ada, b_ada, g_norm1, w_in, w_conv, b_conv, dt_bias, a_log, d_skip, g_ssm_norm, sinks, w_br_ssd, w_br_att, w_out, g_norm2, w_router, b_router, w_expert_in, b_expert_in, w_expert_out, b_expert_out, g_final):
    assert w_ada.shape[0] == 1, "single-layer stack"
    lw = dict(w_ada=w_ada[0], b_ada=b_ada[0], g_norm1=g_norm1[0], w_in=_reorder_w_in(w_in[0]),
              w_conv=w_conv[0], b_conv=b_conv[0], dt_bias=dt_bias[0], a_log=a_log[0], d_skip=d_skip[0],
              g_ssm_norm=g_ssm_norm[0], sinks=sinks[0],
              w_br_ssd=w_br_ssd[0].astype(BF16), w_br_att=w_br_att[0].astype(BF16), w_out=w_out[0].astype(BF16),
              g_norm2=g_norm2[0], w_router=w_router[0], b_router=b_router[0],
              w_expert_in=w_expert_in[0], b_expert_in=b_expert_in[0],
              w_expert_out=w_expert_out[0], b_expert_out=b_expert_out[0])
    return _forward(x_prompt, x_sample, state_ssm[0], state_conv[0], cache_win_k[0], cache_win_v[0],
                    c_prompt, c_sample, lw, g_final)
```

```python
import functools
import math

import jax
import jax.numpy as jnp
from jax import lax
from jax.experimental import pallas as pl
from jax.experimental.pallas import tpu as pltpu

F32 = jnp.float32
BF16 = jnp.bfloat16
I32 = jnp.int32
HIGHEST = lax.Precision.HIGHEST

D_MODEL = 2048
PAST_LEN = 16384
D_INNER = 2 * D_MODEL
SSM_HEAD_DIM = 64
SSM_HEADS = D_INNER // SSM_HEAD_DIM
SSM_GROUPS = 8
HEADS_PER_GROUP = SSM_HEADS // SSM_GROUPS
GROUP_WIDTH = HEADS_PER_GROUP * SSM_HEAD_DIM
SSM_STATE = 128
CONV_WIDTH = 4
BC_WIDTH = SSM_GROUPS * SSM_STATE
CONV_DIM = D_INNER + 2 * BC_WIDTH
CHUNK = 128
ATTN_HEADS = 32
KV_HEADS = 8
HEAD_DIM = 64
KV_WIDTH = KV_HEADS * HEAD_DIM
WINDOW = 128
ROPE_DIM = HEAD_DIM // 4
ROPE_THETA = 500000.0
N_EXPERTS = 32
TOP_K = 4
D_FF = D_MODEL
SWIGLU_LIMIT = 7.0
SWIGLU_ALPHA = 1.702
N_MOD = 6
EPS = 1e-6
Z_END = D_INNER
XBC_END = Z_END + CONV_DIM
DT_END = XBC_END + SSM_HEADS
Q_END = DT_END + D_MODEL
K_END = Q_END + KV_WIDTH
V_END = K_END + KV_WIDTH

LANES = 128
BF16_SUBLANES = 16
VMEM_LIMIT = 56 * 1024 * 1024

PC_Z = 0
PC_X = D_INNER
PC_B = PC_X + D_INNER
PC_C = PC_B + BC_WIDTH
PC_Q = PC_C + BC_WIDTH
PC_K = PC_Q + D_MODEL
PC_V = PC_K + KV_WIDTH
PC_G = PC_V + KV_WIDTH
PC_DT = PC_G + 2 * D_MODEL
DT_PAD = 512
PROJ_W = PC_DT + DT_PAD
IN_TN = 1280

MOE_BLK = 512
MOE_TF = 1024


def _cparams(sem):
    return pltpu.CompilerParams(dimension_semantics=sem, vmem_limit_bytes=VMEM_LIMIT)


def _silu(x):
    return x * jax.nn.sigmoid(x)


def _ada_kernel(c_ref, w_ref, b_ref, o_ref):
    s = _silu(c_ref[...]).astype(BF16)
    o_ref[...] = jnp.dot(s, w_ref[...].astype(BF16), preferred_element_type=F32) + b_ref[...]


def _ada(c, w_ada, b_ada):
    n, d = c.shape
    nout = w_ada.shape[1]
    tn = 1024
    return pl.pallas_call(
        _ada_kernel,
        out_shape=jax.ShapeDtypeStruct((n, nout), F32),
        grid=(nout // tn,),
        in_specs=[pl.BlockSpec((n, d), lambda j: (0, 0)),
                  pl.BlockSpec((d, tn), lambda j: (0, j)),
                  pl.BlockSpec((1, tn), lambda j: (0, j))],
        out_specs=pl.BlockSpec((n, tn), lambda j: (0, j)),
        compiler_params=_cparams(("arbitrary",)),
        name="ada_mod",
    )(c, w_ada, b_ada.reshape(1, nout))


def _inproj_kernel(x_ref, sh_ref, sc_ref, g_ref, w_ref, o_ref, h_ref):
    @pl.when(pl.program_id(1) == 0)
    def _():
        x = x_ref[...]
        y = x * lax.rsqrt(jnp.mean(x * x, axis=-1, keepdims=True) + EPS) * g_ref[...]
        h_ref[...] = (y * (1.0 + sc_ref[0]) + sh_ref[0]).astype(BF16)

    o_ref[...] = jnp.dot(h_ref[...], w_ref[...], preferred_element_type=F32)


def _narrow_job(a, n_steps, step_of):
    flat = a.reshape(-1, a.shape[-1])
    rows = flat.shape[0]
    slab = pl.cdiv(pl.cdiv(rows, n_steps), BF16_SUBLANES) * BF16_SUBLANES
    last = pl.cdiv(rows, slab) - 1
    spec = pl.BlockSpec((slab, flat.shape[1]), lambda *g: (jnp.minimum(step_of(*g), last), 0))
    return flat, spec, jax.ShapeDtypeStruct(flat.shape, BF16)


def _mod_spec(tm, seq_len):
    if seq_len == 1:
        return pl.BlockSpec((1, tm, D_MODEL), lambda i, *_: (0, i, 0))
    return pl.BlockSpec((1, 1, D_MODEL), lambda i, *_: ((i * tm) // seq_len, 0, 0))


def _row_tile(cap, m, seq_len):
    tm = min(cap, m if seq_len == 1 else seq_len)
    assert m % tm == 0 and (seq_len == 1 or seq_len % tm == 0)
    return tm


def _mod_arr(m, seq_len):
    return m[None] if seq_len == 1 else m[:, None, :]


def _inproj(x, shift, scale, g, w_in_r, seq_len):
    m = x.shape[0]
    tm = _row_tile(1024, m, seq_len)
    return pl.pallas_call(
        _inproj_kernel,
        out_shape=jax.ShapeDtypeStruct((m, PROJ_W), F32),
        grid=(m // tm, PROJ_W // IN_TN),
        in_specs=[pl.BlockSpec((tm, D_MODEL), lambda i, j: (i, 0)),
                  _mod_spec(tm, seq_len), _mod_spec(tm, seq_len),
                  pl.BlockSpec((1, D_MODEL), lambda i, j: (0, 0)),
                  pl.BlockSpec((D_MODEL, IN_TN), lambda i, j: (0, j))],
        out_specs=pl.BlockSpec((tm, IN_TN), lambda i, j: (i, j)),
        scratch_shapes=[pltpu.VMEM((tm, D_MODEL), BF16)],
        compiler_params=_cparams(("parallel", "arbitrary")),
        name="in_proj",
    )(x, _mod_arr(shift, seq_len), _mod_arr(scale, seq_len), g.reshape(1, D_MODEL), w_in_r)


def _softplus(x):
    return jnp.maximum(x, 0.0) + jnp.log1p(jnp.exp(-jnp.abs(x)))


def _split3(a):
    hi = a.astype(BF16)
    r = a - hi.astype(F32)
    mid = r.astype(BF16)
    lo = (r - mid.astype(F32)).astype(BF16)
    return hi, mid, lo


def _spread(pieces, ones2):
    hi, mid, lo = pieces
    out = jnp.dot(jnp.concatenate([hi, mid], axis=1), ones2, preferred_element_type=F32)
    if lo is not None:
        out = out + jnp.dot(lo, ones2[0:LANES], preferred_element_type=F32)
    return out


def _ssd_kernel(z_ref, x_ref, b_ref, c_ref, dt_ref,
                cw_ref, cb_ref, dtb_ref, alog_ref, dskip_ref, gn_ref, exp_ref, sel_ref, tri_ref, wide_ref,
                y_ref, st_ref, conv_ref, narrow_ref,
                ext_s, xs_s, xdt_s, eax_s, tex_s, elx_s, acst_s, bm_s, cm_s, state_s):
    ci = pl.program_id(1)
    last_chunk = ci == pl.num_programs(1) - 1
    narrow_ref[...] = wide_ref[...].astype(BF16)

    @pl.when(ci == 0)
    def _():
        ext_s[0:8, :] = jnp.zeros((8, CONV_DIM), F32)
        state_s[...] = jnp.zeros_like(state_s)

    ext_s[8:8 + CHUNK, 0:D_INNER] = x_ref[...]
    ext_s[8:8 + CHUNK, D_INNER:D_INNER + BC_WIDTH] = b_ref[...]
    ext_s[8:8 + CHUNK, D_INNER + BC_WIDTH:CONV_DIM] = c_ref[...]

    def conv(lo, hi):
        acc = cb_ref[:, lo:hi] + cw_ref[3:4, lo:hi] * ext_s[8:8 + CHUNK, lo:hi]
        for j in range(CONV_WIDTH - 1):
            acc = acc + cw_ref[j:j + 1, lo:hi] * ext_s[5 + j:5 + j + CHUNK, lo:hi]
        return _silu(acc)

    xs_s[...] = conv(0, D_INNER)
    bm_s[...] = conv(D_INNER, D_INNER + BC_WIDTH)
    cm_s[...] = conv(D_INNER + BC_WIDTH, CONV_DIM)
    tail = ext_s[5 + CHUNK:8 + CHUNK, :]
    conv_ref[0] = tail
    ext_s[5:8, :] = tail

    dt = _softplus(dt_ref[:, 0:LANES] + dtb_ref[...])
    da = dt * (-jnp.exp(alog_ref[...]))
    tri = tri_ref[...]
    acs = sum(jnp.dot(tri, p, preferred_element_type=F32) for p in _split3(da))
    acst_s[...] = acs.T
    acs_pieces = _split3(acs)
    last = acs[CHUNK - 1:CHUNK, :]
    two = lambda a: _split3(a)[:2] + (None,)
    exp2 = exp_ref[...]
    xdt_s[...] = xs_s[...] * _spread(two(dt), exp2)
    eax_s[...] = _spread(two(jnp.exp(acs)), exp2)
    tex_s[...] = _spread(two(jnp.exp(last - acs)), exp2)
    elx_s[...] = _spread(_split3(jnp.broadcast_to(jnp.exp(last), (8, LANES))), exp2)

    row = lax.broadcasted_iota(I32, (CHUNK, CHUNK), 0)
    col = lax.broadcasted_iota(I32, (CHUNK, CHUNK), 1)
    causal = row >= col
    low_half = col < SSM_HEAD_DIM

    def group(g, carry):
        o512 = pl.multiple_of(g * GROUP_WIDTH, GROUP_WIDTH)
        o128 = pl.multiple_of(g * SSM_STATE, SSM_STATE)
        bg = bm_s[:, pl.ds(o128, SSM_STATE)]
        cg16 = cm_s[:, pl.ds(o128, SSM_STATE)].astype(BF16)
        bg16 = bg.astype(BF16)
        cb = lax.dot_general(cg16, bg16, (((1,), (1,)), ((), ())), preferred_element_type=F32)
        sel2 = sel_ref[:, pl.ds(pl.multiple_of(g * HEADS_PER_GROUP * CHUNK, HEADS_PER_GROUP * CHUNK),
                                HEADS_PER_GROUP * CHUNK)]
        colb = _spread(acs_pieces, sel2)
        rows = acst_s[pl.ds(pl.multiple_of(g * HEADS_PER_GROUP, HEADS_PER_GROUP), HEADS_PER_GROUP), :]
        xdt_g = xdt_s[:, pl.ds(o512, GROUP_WIDTH)]
        parts = []
        for jp in range(HEADS_PER_GROUP // 2):
            ms = []
            for j in (2 * jp, 2 * jp + 1):
                diff = colb[:, j * CHUNK:(j + 1) * CHUNK] - rows[j:j + 1, :]
                dec = jnp.exp(jnp.where(causal, diff, -jnp.inf))
                ms.append((cb * dec).astype(BF16))
            xp = xdt_g[:, jp * LANES:(jp + 1) * LANES]
            rhs = jnp.concatenate([jnp.where(low_half, xp, 0.0), jnp.where(low_half, 0.0, xp)], axis=0)
            parts.append(jnp.dot(jnp.concatenate(ms, axis=1), rhs.astype(BF16), preferred_element_type=F32))
        y = jnp.concatenate(parts, axis=1)

        st = state_s[:, pl.ds(o512, GROUP_WIDTH)]
        y = y + jnp.dot(cg16, st.astype(BF16), preferred_element_type=F32) * eax_s[:, pl.ds(o512, GROUP_WIDTH)]
        xw = (xdt_g * tex_s[:, pl.ds(o512, GROUP_WIDTH)]).astype(BF16)
        st_new = (st * elx_s[0:1, pl.ds(o512, GROUP_WIDTH)]
                  + jnp.dot(bg.T.astype(BF16), xw, preferred_element_type=F32))
        state_s[:, pl.ds(o512, GROUP_WIDTH)] = st_new

        @pl.when(last_chunk)
        def _():
            st_ref[0, pl.ds(o512, GROUP_WIDTH), :] = st_new.T

        y = y + dskip_ref[:, pl.ds(o512, GROUP_WIDTH)] * xs_s[:, pl.ds(o512, GROUP_WIDTH)]
        u = y * _silu(z_ref[:, pl.ds(o512, GROUP_WIDTH)])
        u = u * lax.rsqrt(jnp.mean(u * u, axis=-1, keepdims=True) + EPS)
        y_ref[:, pl.ds(o512, GROUP_WIDTH)] = (u * gn_ref[:, pl.ds(o512, GROUP_WIDTH)]).astype(BF16)
        return carry

    lax.fori_loop(0, SSM_GROUPS, group, 0, unroll=4)


def _pad_heads(v):
    return jnp.pad(v.astype(F32), (0, LANES - SSM_HEADS)).reshape(1, LANES)


def _head_expand(dtype=BF16, copies=2):
    h = jnp.arange(LANES)[:, None]
    ch = jnp.arange(D_INNER)[None, :] // SSM_HEAD_DIM
    return jnp.tile((h == ch).astype(dtype), (copies, 1))


def _head_select():
    h = jnp.arange(LANES)[:, None]
    blk = jnp.arange(SSM_HEADS * CHUNK)[None, :] // CHUNK
    return jnp.tile((h == blk).astype(BF16), (2, 1))


def _ssd_prompt(proj, bsz, seq_len, lw, wide):
    nc = seq_len // CHUNK
    m = bsz * seq_len
    tri = (jnp.arange(CHUNK)[:, None] >= jnp.arange(CHUNK)[None, :]).astype(BF16)
    const = lambda shape: pl.BlockSpec(shape, lambda b, c: (0, 0))
    rowblk = lambda width, col: pl.BlockSpec((CHUNK, width), lambda b, c: (b * nc + c, col // width))
    wide2d, wide_spec, narrow_shape = _narrow_job(wide, bsz * nc, lambda b, c: b * nc + c)
    y, st, conv, narrow = pl.pallas_call(
        _ssd_kernel,
        out_shape=(jax.ShapeDtypeStruct((m, D_INNER), BF16),
                   jax.ShapeDtypeStruct((bsz, D_INNER, SSM_STATE), F32),
                   jax.ShapeDtypeStruct((bsz, CONV_WIDTH - 1, CONV_DIM), F32), narrow_shape),
        grid=(bsz, nc),
        in_specs=[rowblk(D_INNER, PC_Z), rowblk(D_INNER, PC_X), rowblk(BC_WIDTH, PC_B), rowblk(BC_WIDTH, PC_C),
                  rowblk(DT_PAD, PC_DT),
                  const((CONV_WIDTH, CONV_DIM)), const((1, CONV_DIM)), const((1, LANES)), const((1, LANES)),
                  const((1, D_INNER)), const((1, D_INNER)), const((2 * LANES, D_INNER)),
                  const((2 * LANES, SSM_HEADS * CHUNK)), const((CHUNK, CHUNK)), wide_spec],
        out_specs=(pl.BlockSpec((CHUNK, D_INNER), lambda b, c: (b * nc + c, 0)),
                   pl.BlockSpec((1, D_INNER, SSM_STATE), lambda b, c: (b, 0, 0)),
                   pl.BlockSpec((1, CONV_WIDTH - 1, CONV_DIM), lambda b, c: (b, 0, 0)), wide_spec),
        scratch_shapes=[pltpu.VMEM((8 + CHUNK, CONV_DIM), F32),
                        pltpu.VMEM((CHUNK, D_INNER), F32), pltpu.VMEM((CHUNK, D_INNER), F32),
                        pltpu.VMEM((CHUNK, D_INNER), F32), pltpu.VMEM((CHUNK, D_INNER), F32),
                        pltpu.VMEM((8, D_INNER), F32), pltpu.VMEM((LANES, CHUNK), F32),
                        pltpu.VMEM((CHUNK, BC_WIDTH), F32), pltpu.VMEM((CHUNK, BC_WIDTH), F32),
                        pltpu.VMEM((SSM_STATE, D_INNER), F32)],
        compiler_params=_cparams(("arbitrary", "arbitrary")),
        name="ssd_chunks",
    )(proj, proj, proj, proj, proj,
      lw['w_conv'], lw['b_conv'].reshape(1, CONV_DIM), _pad_heads(lw['dt_bias']), _pad_heads(lw['a_log']),
      jnp.repeat(lw['d_skip'].astype(F32), SSM_HEAD_DIM).reshape(1, D_INNER),
      lw['g_ssm_norm'].reshape(1, D_INNER), _head_expand(), _head_select(), tri, wide2d)
    return y, st.reshape(bsz, SSM_HEADS, SSM_HEAD_DIM, SSM_STATE), conv, narrow.reshape(wide.shape)


def _ssd_step_pre_kernel(x_ref, b_ref, c_ref, dt_ref, p0_ref, p1_ref, p2_ref,
                         cw_ref, cb_ref, dtb_ref, alog_ref, exp_ref,
                         xs_ref, xdt_ref, bm_ref, cm_ref, dec_ref, conv_ref):
    def conv(u_ref, lo, hi):
        acc = cb_ref[:, lo:hi] + cw_ref[3:4, lo:hi] * u_ref[...]
        for j, p_ref in enumerate((p0_ref, p1_ref, p2_ref)):
            acc = acc + cw_ref[j:j + 1, lo:hi] * p_ref[:, lo:hi]
        return _silu(acc)

    xs = conv(x_ref, 0, D_INNER)
    xs_ref[...] = xs
    bm_ref[...] = conv(b_ref, D_INNER, D_INNER + BC_WIDTH)
    cm_ref[...] = conv(c_ref, D_INNER + BC_WIDTH, CONV_DIM)
    dt = _softplus(dt_ref[:, 0:LANES] + dtb_ref[...])
    dec_ref[...] = jnp.exp(dt * (-jnp.exp(alog_ref[...])))
    xdt_ref[...] = xs * jnp.dot(dt, exp_ref[...], precision=HIGHEST, preferred_element_type=F32)
    conv_ref[0] = p1_ref[...]
    conv_ref[1] = p2_ref[...]
    conv_ref[2, :, 0:D_INNER] = x_ref[...]
    conv_ref[2, :, D_INNER:D_INNER + BC_WIDTH] = b_ref[...]
    conv_ref[2, :, D_INNER + BC_WIDTH:CONV_DIM] = c_ref[...]


def _ssd_step_kernel(dec_ref, st_ref, xdtt_ref, bm_ref, cmt_ref, xs_ref, z_ref, dskip_ref, gn_ref,
                     sto_ref, y_ref, yt_s, *, tb):
    i = pl.program_id(0)
    nb = xs_ref.shape[0]
    sub = lax.broadcasted_iota(I32, (nb, SSM_STATE), 0)
    lane = lax.broadcasted_iota(I32, (SSM_STATE, nb), 1)

    @pl.when(i == 0)
    def _():
        yt_s[...] = jnp.zeros_like(yt_s)

    for bi in range(tb):
        b = i * tb + bi
        for g in range(SSM_GROUPS):
            r0 = g * GROUP_WIDTH
            b_row = jnp.where(sub == b, bm_ref[:, g * SSM_STATE:(g + 1) * SSM_STATE], 0.0).astype(BF16)
            term = jnp.dot(xdtt_ref[r0:r0 + GROUP_WIDTH, :].astype(BF16), b_row, preferred_element_type=F32)
            news = []
            for j in range(HEADS_PER_GROUP):
                h = g * HEADS_PER_GROUP + j
                rr = r0 + j * SSM_HEAD_DIM
                new = (st_ref[bi, rr:rr + SSM_HEAD_DIM, :] * dec_ref[b, h]
                       + term[j * SSM_HEAD_DIM:(j + 1) * SSM_HEAD_DIM, :])
                sto_ref[bi, rr:rr + SSM_HEAD_DIM, :] = new
                news.append(new)
            new_g = jnp.concatenate(news, axis=0).astype(BF16)
            c_col = jnp.where(lane == b, cmt_ref[g], 0.0).astype(BF16)
            yt_s[r0:r0 + GROUP_WIDTH, :] += jnp.dot(new_g, c_col, preferred_element_type=F32)

    @pl.when(i == pl.num_programs(0) - 1)
    def _():
        for g in range(SSM_GROUPS):
            sl = slice(g * GROUP_WIDTH, (g + 1) * GROUP_WIDTH)
            y = yt_s[sl, :].T + dskip_ref[:, sl] * xs_ref[:, sl]
            u = y * _silu(z_ref[:, sl])
            u = u * lax.rsqrt(jnp.mean(u * u, axis=-1, keepdims=True) + EPS)
            y_ref[:, sl] = (u * gn_ref[:, sl]).astype(BF16)


def _ssd_sample(proj, state_ssm, state_conv, lw):
    nb = proj.shape[0]
    full = lambda shape: pl.BlockSpec(shape, lambda *_: tuple(0 for _ in shape))
    colblk = lambda width, col: pl.BlockSpec((nb, width), lambda *_: (0, col // width))
    prev = [state_conv[:, j, :] for j in range(CONV_WIDTH - 1)]
    tr = min(32, nb)
    rows = lambda width, col=0: pl.BlockSpec((tr, width), lambda i: (i, col // width))
    xs, xdt, bm, cm, dec, conv = pl.pallas_call(
        _ssd_step_pre_kernel,
        out_shape=(jax.ShapeDtypeStruct((nb, D_INNER), F32), jax.ShapeDtypeStruct((nb, D_INNER), F32),
                   jax.ShapeDtypeStruct((nb, BC_WIDTH), F32), jax.ShapeDtypeStruct((nb, BC_WIDTH), F32),
                   jax.ShapeDtypeStruct((nb, LANES), F32),
                   jax.ShapeDtypeStruct((CONV_WIDTH - 1, nb, CONV_DIM), F32)),
        grid=(nb // tr,),
        in_specs=[rows(D_INNER, PC_X), rows(BC_WIDTH, PC_B), rows(BC_WIDTH, PC_C), rows(DT_PAD, PC_DT),
                  rows(CONV_DIM), rows(CONV_DIM), rows(CONV_DIM),
                  full((CONV_WIDTH, CONV_DIM)), full((1, CONV_DIM)), full((1, LANES)), full((1, LANES)),
                  full((LANES, D_INNER))],
        out_specs=(rows(D_INNER), rows(D_INNER), rows(BC_WIDTH), rows(BC_WIDTH), rows(LANES),
                   pl.BlockSpec((CONV_WIDTH - 1, tr, CONV_DIM), lambda i: (0, i, 0))),
        compiler_params=_cparams(("parallel",)),
        name="ssd_step_pre",
    )(proj, proj, proj, proj, *prev,
      lw['w_conv'], lw['b_conv'].reshape(1, CONV_DIM), _pad_heads(lw['dt_bias']), _pad_heads(lw['a_log']),
      _head_expand(F32, 1))

    tb = 2
    cmt = cm.reshape(nb, SSM_GROUPS, SSM_STATE).transpose(1, 2, 0)
    st_new, y = pl.pallas_call(
        functools.partial(_ssd_step_kernel, tb=tb),
        out_shape=(jax.ShapeDtypeStruct((nb, D_INNER, SSM_STATE), F32),
                   jax.ShapeDtypeStruct((nb, D_INNER), BF16)),
        grid=(nb // tb,),
        in_specs=[pl.BlockSpec(memory_space=pltpu.SMEM),
                  pl.BlockSpec((tb, D_INNER, SSM_STATE), lambda i: (i, 0, 0)),
                  full((D_INNER, nb)), full((nb, BC_WIDTH)), full((SSM_GROUPS, SSM_STATE, nb)),
                  full((nb, D_INNER)), colblk(D_INNER, PC_Z), full((1, D_INNER)), full((1, D_INNER))],
        out_specs=(pl.BlockSpec((tb, D_INNER, SSM_STATE), lambda i: (i, 0, 0)),
                   full((nb, D_INNER))),
        scratch_shapes=[pltpu.VMEM((D_INNER, nb), F32)],
        compiler_params=_cparams(("arbitrary",)),
        name="ssd_step",
    )(dec, state_ssm.reshape(nb, D_INNER, SSM_STATE), xdt.T, bm, cmt, xs, proj,
      jnp.repeat(lw['d_skip'].astype(F32), SSM_HEAD_DIM).reshape(1, D_INNER),
      lw['g_ssm_norm'].reshape(1, D_INNER))
    return (y, st_new.reshape(nb, SSM_HEADS, SSM_HEAD_DIM, SSM_STATE), conv.transpose(1, 0, 2))


def _rope_tables(pos):
    half = ROPE_DIM // 2
    inv_freq = ROPE_THETA ** (-jnp.arange(half, dtype=F32) * 2.0 / ROPE_DIM)
    ang = pos.astype(F32)[:, None] * inv_freq[None, :]
    cos, sin = jnp.cos(ang), jnp.sin(ang)
    n = pos.shape[0]
    ones = jnp.ones((n, HEAD_DIM - ROPE_DIM), F32)
    zeros = jnp.zeros((n, HEAD_DIM - ROPE_DIM), F32)
    zh = jnp.zeros((n, half), F32)
    cos_f = jnp.concatenate([cos, cos, ones], axis=1)
    sin_a = jnp.concatenate([zh, sin, zeros], axis=1)
    sin_b = jnp.concatenate([-sin, zh, zeros], axis=1)
    tile = lambda t: jnp.tile(t, (1, LANES // HEAD_DIM))
    return tile(cos_f), tile(sin_a), tile(sin_b)


def _rope_slab(x, cos_f, sin_a, sin_b):
    half = ROPE_DIM // 2
    return x * cos_f + pltpu.roll(x, half, 1) * sin_a + pltpu.roll(x, LANES - half, 1) * sin_b


def _attn_kernel(sink_ref, q_ref, k_ref, v_ref, cos_ref, sin_a_ref, sin_b_ref, wide_ref,
                 y_ref, kn_ref, vn_ref, narrow_ref, kspan_s, vspan_s, sc_s, p_s):
    bi = pl.program_id(1)
    nq = q_ref.shape[0]
    narrow_ref[...] = wide_ref[...].astype(BF16)

    @pl.when(bi == 0)
    def _():
        kspan_s[0:nq, :] = jnp.zeros((nq, KV_WIDTH), F32)
        vspan_s[0:nq, :] = jnp.zeros((nq, KV_WIDTH), F32)

    cos_f, sin_a, sin_b = cos_ref[...], sin_a_ref[...], sin_b_ref[...]
    rope = lambda x: _rope_slab(x, cos_f, sin_a, sin_b)
    kr = jnp.concatenate([rope(k_ref[:, s * LANES:(s + 1) * LANES]) for s in range(KV_WIDTH // LANES)], axis=1)
    v = v_ref[...]
    kspan_s[nq:2 * nq, :] = kr
    vspan_s[nq:2 * nq, :] = v
    kn_ref[0] = kr
    vn_ref[0] = v

    t = lax.broadcasted_iota(I32, (nq, nq), 0)
    j = lax.broadcasted_iota(I32, (nq, nq), 1)
    own_block = j <= t
    low = lax.broadcasted_iota(I32, (2 * nq, LANES), 1) < HEAD_DIM
    scale = HEAD_DIM ** -0.5
    grp = ATTN_HEADS // KV_HEADS

    def block_diag(span_ref, kh):
        nat = span_ref[:, (kh // 2) * LANES:(kh // 2 + 1) * LANES]
        swp = pltpu.roll(nat, HEAD_DIM, 1)
        lo, hi = (nat, swp) if kh % 2 == 0 else (swp, nat)
        return jnp.concatenate([jnp.where(low, lo, 0.0), jnp.where(low, 0.0, hi)], axis=0).astype(BF16)

    for kh in range(KV_HEADS):
        kbd = block_diag(kspan_s, kh)
        for p in range(grp // 2):
            h0 = kh * grp + 2 * p
            qp = (rope(q_ref[:, h0 * HEAD_DIM:(h0 + 2) * HEAD_DIM]) * scale).astype(BF16)
            sc = lax.dot_general(qp, kbd, (((1,), (1,)), ((), ())), preferred_element_type=F32)
            for a in range(2):
                prev = sc[:, 2 * a * nq:(2 * a + 1) * nq]
                cur = sc[:, (2 * a + 1) * nq:(2 * a + 2) * nq]
                sc_s[h0 + a] = jnp.where(own_block, cur, prev)

    have_prev = bi > 0
    for h in range(ATTN_HEADS):
        sa = sc_s[h]
        sa = jnp.where(own_block | have_prev, sa, -jnp.inf)
        sink = sink_ref[h]
        mx = jnp.maximum(jnp.max(sa, axis=-1, keepdims=True), sink)
        pa = jnp.exp(sa - mx)
        pa = pa / (jnp.sum(pa, axis=-1, keepdims=True) + jnp.exp(sink - mx))
        p_s[h] = pa.astype(BF16)

    zero = jnp.zeros((nq, nq), BF16)
    for kh in range(KV_HEADS):
        vbd = block_diag(vspan_s, kh)
        for p in range(grp // 2):
            h0 = kh * grp + 2 * p
            parts = []
            for a in range(2):
                pa = p_s[h0 + a]
                parts += [jnp.where(own_block, zero, pa), jnp.where(own_block, pa, zero)]
            y_ref[:, h0 * HEAD_DIM:(h0 + 2) * HEAD_DIM] = jnp.dot(
                jnp.concatenate(parts, axis=1), vbd, preferred_element_type=F32).astype(BF16)

    kspan_s[0:nq, :] = kspan_s[nq:2 * nq, :]
    vspan_s[0:nq, :] = vspan_s[nq:2 * nq, :]


def _attn_prompt(proj, bsz, seq_len, sinks, wide):
    nq = CHUNK
    nb = seq_len // nq
    m = bsz * seq_len
    cos_f, sin_a, sin_b = _rope_tables(jnp.arange(seq_len))
    rowblk = lambda width, col: pl.BlockSpec((nq, width), lambda b, i: (b * nb + i, col // width))
    tab = pl.BlockSpec((nq, LANES), lambda b, i: (i, 0))
    wide2d, wide_spec, narrow_shape = _narrow_job(wide, bsz * nb, lambda b, i: b * nb + i)
    y, kn, vn, narrow = pl.pallas_call(
        _attn_kernel,
        out_shape=(jax.ShapeDtypeStruct((m, D_MODEL), BF16),
                   jax.ShapeDtypeStruct((bsz, WINDOW, KV_WIDTH), F32),
                   jax.ShapeDtypeStruct((bsz, WINDOW, KV_WIDTH), F32), narrow_shape),
        grid=(bsz, nb),
        in_specs=[pl.BlockSpec(memory_space=pltpu.SMEM),
                  rowblk(D_MODEL, PC_Q), rowblk(KV_WIDTH, PC_K), rowblk(KV_WIDTH, PC_V), tab, tab, tab,
                  wide_spec],
        out_specs=(pl.BlockSpec((nq, D_MODEL), lambda b, i: (b * nb + i, 0)),
                   pl.BlockSpec((1, WINDOW, KV_WIDTH), lambda b, i: (b, 0, 0)),
                   pl.BlockSpec((1, WINDOW, KV_WIDTH), lambda b, i: (b, 0, 0)), wide_spec),
        scratch_shapes=[pltpu.VMEM((2 * nq, KV_WIDTH), F32), pltpu.VMEM((2 * nq, KV_WIDTH), F32),
                        pltpu.VMEM((ATTN_HEADS, nq, nq), F32), pltpu.VMEM((ATTN_HEADS, nq, nq), BF16)],
        compiler_params=_cparams(("arbitrary", "arbitrary")),
        name="attn_blocks",
    )(sinks.astype(F32), proj, proj, proj, cos_f, sin_a, sin_b, wide2d)
    shape = (bsz, WINDOW, KV_HEADS, HEAD_DIM)
    return y, kn.reshape(shape), vn.reshape(shape), narrow.reshape(wide.shape)


def _attn_step_kernel(sink_ref, q_ref, k_ref, v_ref, cos_ref, sin_a_ref, sin_b_ref, fold_ref, foldt_ref,
                      kc_ref, vc_ref, y_ref, ko_ref, vo_ref, *, tb):
    cos_f, sin_a, sin_b = cos_ref[...], sin_a_ref[...], sin_b_ref[...]
    rope = lambda ref, w: jnp.concatenate(
        [_rope_slab(ref[:, s * LANES:(s + 1) * LANES], cos_f, sin_a, sin_b) for s in range(w // LANES)], axis=1)
    qr = rope(q_ref, D_MODEL) * (HEAD_DIM ** -0.5)
    kr = rope(k_ref, KV_WIDTH)
    v = v_ref[...]

    own = (lax.broadcasted_iota(I32, (ATTN_HEADS, D_MODEL), 1) // HEAD_DIM
           == lax.broadcasted_iota(I32, (ATTN_HEADS, D_MODEL), 0))
    qb = jnp.concatenate([jnp.where(own, jnp.broadcast_to(qr[bi:bi + 1], (ATTN_HEADS, D_MODEL)), 0.0)
                          for bi in range(tb)], axis=0).astype(BF16)
    qm = jnp.dot(qb, fold_ref[...], preferred_element_type=F32).astype(BF16)

    row = lax.broadcasted_iota(I32, (WINDOW, KV_WIDTH), 0)
    w = lax.broadcasted_iota(I32, (ATTN_HEADS, WINDOW), 1)
    sink = sink_ref[...]
    bf = lambda x: x.astype(BF16)
    outs = []
    for bi in range(tb):
        qmb = qm[bi * ATTN_HEADS:(bi + 1) * ATTN_HEADS]
        kc, vc = kc_ref[bi], vc_ref[bi]
        kn, vn = kr[bi:bi + 1], v[bi:bi + 1]
        sc = lax.dot_general(qmb, bf(kc), (((1,), (1,)), ((), ())), preferred_element_type=F32)
        sc = jnp.where(w >= 1, sc, -jnp.inf)
        s_new = jnp.sum(qmb.astype(F32) * bf(kn).astype(F32), axis=-1, keepdims=True)
        mx = jnp.maximum(jnp.maximum(jnp.max(sc, axis=-1, keepdims=True), s_new), sink)
        p = jnp.exp(sc - mx)
        p_new = jnp.exp(s_new - mx)
        den = jnp.sum(p, axis=-1, keepdims=True) + p_new + jnp.exp(sink - mx)
        p = p / den
        p_new = p_new / den
        out = jnp.dot(bf(p), bf(vc), preferred_element_type=F32) + bf(p_new).astype(F32) * bf(vn).astype(F32)
        outs.append(bf(out))
        ko_ref[bi] = jnp.where(row == WINDOW - 1, kn, pltpu.roll(kc, WINDOW - 1, 0))
        vo_ref[bi] = jnp.where(row == WINDOW - 1, vn, pltpu.roll(vc, WINDOW - 1, 0))

    of = jnp.dot(jnp.concatenate(outs, axis=0), foldt_ref[...], preferred_element_type=F32)
    of = of.reshape(tb, ATTN_HEADS, D_MODEL)
    y_ref[...] = jnp.sum(jnp.where(own[None], of, 0.0), axis=1).astype(BF16)


def _attn_sample(proj, cache_k, cache_v, sinks):
    nb = proj.shape[0]
    tb = 8
    grp = ATTN_HEADS // KV_HEADS
    cos_f, sin_a, sin_b = _rope_tables(jnp.full((1,), PAST_LEN))
    c = jnp.arange(D_MODEL)[:, None]
    l = jnp.arange(KV_WIDTH)[None, :]
    fold = ((c % HEAD_DIM == l % HEAD_DIM) & ((c // HEAD_DIM) // grp == l // HEAD_DIM)).astype(BF16)
    full = lambda shape: pl.BlockSpec(shape, lambda i: tuple(0 for _ in shape))
    rowblk = lambda width, col: pl.BlockSpec((tb, width), lambda i: (i, col // width))
    win = pl.BlockSpec((tb, WINDOW, KV_WIDTH), lambda i: (i, 0, 0))
    y, kn, vn = pl.pallas_call(
        functools.partial(_attn_step_kernel, tb=tb),
        out_shape=(jax.ShapeDtypeStruct((nb, D_MODEL), BF16),
                   jax.ShapeDtypeStruct((nb, WINDOW, KV_WIDTH), F32),
                   jax.ShapeDtypeStruct((nb, WINDOW, KV_WIDTH), F32)),
        grid=(nb // tb,),
        in_specs=[full((ATTN_HEADS, 1)), rowblk(D_MODEL, PC_Q), rowblk(KV_WIDTH, PC_K), rowblk(KV_WIDTH, PC_V),
                  full((1, LANES)), full((1, LANES)), full((1, LANES)),
                  full((D_MODEL, KV_WIDTH)), full((KV_WIDTH, D_MODEL)), win, win],
        out_specs=(pl.BlockSpec((tb, D_MODEL), lambda i: (i, 0)), win, win),
        compiler_params=_cparams(("parallel",)),
        name="attn_step",
    )(sinks.astype(F32).reshape(ATTN_HEADS, 1), proj, proj, proj, cos_f, sin_a, sin_b, fold, fold.T,
      cache_k.reshape(nb, WINDOW, KV_WIDTH), cache_v.reshape(nb, WINDOW, KV_WIDTH))
    shape = (nb, WINDOW, KV_HEADS, HEAD_DIM)
    return y, kn.reshape(shape), vn.reshape(shape)


def _merge_kernel(ys_ref, ya_ref, ws_ref, wa_ref, gs_ref, ga_ref, o_ref):
    a = jnp.dot(ys_ref[...], ws_ref[...], preferred_element_type=F32)
    b = jnp.dot(ya_ref[...], wa_ref[...], preferred_element_type=F32)
    o_ref[...] = (jax.nn.sigmoid(gs_ref[...]) * a + jax.nn.sigmoid(ga_ref[...]) * b).astype(BF16)


def _merge(y_ssd, y_att, proj, w_ssd, w_att):
    m = y_ssd.shape[0]
    tm = min(512, m)
    tn = 512
    g0 = PC_G // tn
    return pl.pallas_call(
        _merge_kernel,
        out_shape=jax.ShapeDtypeStruct((m, D_MODEL), BF16),
        grid=(m // tm, D_MODEL // tn),
        in_specs=[pl.BlockSpec((tm, D_INNER), lambda i, j: (i, 0)),
                  pl.BlockSpec((tm, D_MODEL), lambda i, j: (i, 0)),
                  pl.BlockSpec((D_INNER, tn), lambda i, j: (0, j)),
                  pl.BlockSpec((D_MODEL, tn), lambda i, j: (0, j)),
                  pl.BlockSpec((tm, tn), lambda i, j: (i, g0 + j)),
                  pl.BlockSpec((tm, tn), lambda i, j: (i, g0 + D_MODEL // tn + j))],
        out_specs=pl.BlockSpec((tm, tn), lambda i, j: (i, j)),
        compiler_params=_cparams(("parallel", "arbitrary")),
        name="branch_merge",
    )(y_ssd, y_att, w_ssd, w_att, proj, proj)


def _outproj_kernel(*refs, n_tiles, n_tail):
    if n_tail == 0:
        _outproj_tile(*refs)
        return
    n_in = 11
    tail_ref, own = refs[n_in], refs[:n_in] + refs[n_in + 1:]
    h2_ref = own[n_in + 1]
    i = pl.program_id(0)
    pl.when(i < n_tiles)(functools.partial(_outproj_tile, *own))

    @pl.when(i == n_tiles)
    def _():
        h2_ref[...] = jnp.zeros_like(h2_ref)
        h2_ref[0:n_tail, :] = tail_ref[...]


def _outproj_tile(pre_ref, x_ref, gate_ref, sh_ref, sc_ref, g_ref, w_ref, wr_ref, br_ref, before_ref, c0_ref,
                  x1_ref, h2_ref, tg_ref, ti_ref, tr_ref, cnt_ref, cnt_s):
    @pl.when(pl.program_id(0) == 0)
    def _():
        cnt_s[...] = c0_ref[...]

    mixed = jnp.dot(pre_ref[...], w_ref[...], preferred_element_type=F32)
    x1 = x_ref[...] + gate_ref[0] * mixed
    x1_ref[...] = x1
    y = x1 * lax.rsqrt(jnp.mean(x1 * x1, axis=-1, keepdims=True) + EPS) * g_ref[...]
    h2 = y * (1.0 + sc_ref[0]) + sh_ref[0]
    h2_ref[...] = h2
    h_hi, h_mid, _ = _split3(h2)
    hw = jnp.dot(h_hi, wr_ref[...], preferred_element_type=F32)
    logits = (hw[:, 0:LANES] + hw[:, LANES:2 * LANES]
              + jnp.dot(h_mid, wr_ref[:, 0:LANES], preferred_element_type=F32) + br_ref[...])
    lane = lax.broadcasted_iota(I32, logits.shape, 1)
    lane_f = lane.astype(F32)
    vals = jnp.zeros_like(logits)
    idxs = jnp.zeros_like(logits)
    top = None
    den = jnp.zeros((logits.shape[0], 1), F32)
    chosen = []
    for k in range(TOP_K):
        mx = jnp.max(logits, axis=-1, keepdims=True)
        ix = jnp.min(jnp.where(logits == mx, lane_f, float(LANES)), axis=-1, keepdims=True)
        top = mx if top is None else top
        e = jnp.exp(mx - top)
        den = den + e
        vals = jnp.where(lane == k, e, vals)
        idxs = jnp.where(lane == k, ix, idxs)
        chosen.append(lane_f == ix)
        logits = jnp.where(chosen[-1], -jnp.inf, logits)
    tg_ref[...] = vals / den
    ti_ref[...] = idxs.astype(I32)

    picked = jnp.zeros_like(vals)
    for c in chosen:
        picked = jnp.where(c, 1.0, picked)
    ahead = jnp.dot(before_ref[...], picked.astype(BF16), preferred_element_type=F32) + cnt_s[...]
    ranks = jnp.zeros_like(vals)
    for k, c in enumerate(chosen):
        ranks = jnp.where(lane == k, jnp.sum(jnp.where(c, ahead, 0.0), axis=-1, keepdims=True), ranks)
    tr_ref[...] = ranks.astype(I32)
    cnt_s[...] += jnp.sum(picked, axis=0, keepdims=True)
    cnt_ref[...] = cnt_s[...]


def _outproj(pre, x, gate1, shift2, scale2, g2, w_out, w_router, b_router, seq_len, counts0, tail=None):
    m = x.shape[0]
    tm = _row_tile(512, m, seq_len)
    nt = m // tm
    n_tail = 0 if tail is None else tail.shape[0]
    assert n_tail <= tm
    before = (jnp.arange(tm)[None, :] < jnp.arange(tm)[:, None]).astype(BF16)
    w_hi, w_mid, _ = _split3(jnp.pad(w_router.astype(F32), ((0, 0), (0, LANES - N_EXPERTS))))
    wr = jnp.concatenate([w_hi, w_mid], axis=1)
    br = jnp.pad(b_router.astype(F32), (0, LANES - N_EXPERTS), constant_values=-jnp.inf).reshape(1, LANES)
    own = lambda i: jnp.minimum(i, nt - 1)
    row = lambda width: pl.BlockSpec((tm, width), lambda i: (own(i), 0))
    const = lambda shape: pl.BlockSpec(shape, lambda i: (0, 0))
    if seq_len == 1:
        mod = pl.BlockSpec((1, tm, D_MODEL), lambda i: (0, own(i), 0))
    else:
        mod = pl.BlockSpec((1, 1, D_MODEL), lambda i: ((own(i) * tm) // seq_len, 0, 0))
    in_specs = [row(D_MODEL), row(D_MODEL), mod, mod, mod,
                const((1, D_MODEL)), const((D_MODEL, D_MODEL)), const((D_MODEL, 2 * LANES)), const((1, LANES)),
                const((tm, tm)), const((1, LANES))]
    args = [pre, x, _mod_arr(gate1, seq_len), _mod_arr(shift2, seq_len), _mod_arr(scale2, seq_len),
            g2.reshape(1, D_MODEL), w_out, wr, br, before, counts0]
    if n_tail:
        in_specs.append(const((n_tail, D_MODEL)))
        args.append(tail)
    return pl.pallas_call(
        functools.partial(_outproj_kernel, n_tiles=nt, n_tail=n_tail),
        out_shape=(jax.ShapeDtypeStruct((m, D_MODEL), F32), jax.ShapeDtypeStruct((m + n_tail, D_MODEL), F32),
                   jax.ShapeDtypeStruct((m, LANES), F32), jax.ShapeDtypeStruct((m, LANES), I32),
                   jax.ShapeDtypeStruct((m, LANES), I32), jax.ShapeDtypeStruct((1, LANES), F32)),
        grid=(nt + (1 if n_tail else 0),),
        in_specs=in_specs,
        out_specs=(row(D_MODEL), pl.BlockSpec((tm, D_MODEL), lambda i: (i, 0)), row(LANES), row(LANES), row(LANES),
                   const((1, LANES))),
        scratch_shapes=[pltpu.VMEM((1, LANES), F32)],
        compiler_params=_cparams(("arbitrary",)),
        name="out_proj_router",
    )(*args)


def _ffn_kernel(be_ref, na_ref, tok_hbm, h_hbm, w1g_ref, w1l_ref, b1g_ref, b1l_ref, w2_ref, b2_ref,
                o_ref, tok_s, rows_s, xb_s, acc_s, tok_sem, row_sem):
    b = pl.program_id(0)
    f = pl.program_id(1)
    n_blocks = pl.num_programs(0)
    nf = D_FF // MOE_TF
    rows_per_step = MOE_BLK // nf
    n_active = na_ref[0]
    active = b < n_active
    nxt = jnp.minimum(b + 1, n_blocks - 1)
    nxt_slot = (b + 1) % 2

    def tok_copy(blk, sl):
        return pltpu.make_async_copy(tok_hbm.at[blk], tok_s.at[sl], tok_sem.at[sl])

    def row_copy(tok, r):
        return pltpu.make_async_copy(h_hbm.at[pl.ds(tok, 1)], rows_s.at[pl.ds(r, 1)], row_sem)

    def wait_rows():
        for r in range(MOE_BLK):
            row_copy(0, r).wait()

    @pl.when(active & (f == 0))
    def _():
        @pl.when(b == 0)
        def _():
            first = tok_copy(0, 0)
            first.start()
            first.wait()

            def body(r, c):
                row_copy(tok_s[0, r], r).start()
                return c
            lax.fori_loop(0, MOE_BLK, body, 0, unroll=8)
            tok_copy(nxt, 1).start()

        wait_rows()
        xb_s[...] = rows_s[...].astype(BF16)
        tok_copy(nxt, nxt_slot).wait()

    @pl.when(active)
    def _():
        base = f * rows_per_step
        for i in range(rows_per_step):
            row_copy(tok_s[nxt_slot, base + i], base + i).start()
        xb = xb_s[...]
        glu = jnp.dot(xb, w1g_ref[0], preferred_element_type=F32) + b1g_ref[0]
        lin = jnp.dot(xb, w1l_ref[0], preferred_element_type=F32) + b1l_ref[0]
        glu = jnp.minimum(glu, SWIGLU_LIMIT)
        lin = jnp.clip(lin, -SWIGLU_LIMIT, SWIGLU_LIMIT)
        act = glu * jax.nn.sigmoid(SWIGLU_ALPHA * glu) * (lin + 1.0)
        part = jnp.dot(act.astype(BF16), w2_ref[0], preferred_element_type=F32)

        @pl.when(f == 0)
        def _():
            acc_s[...] = part + b2_ref[0]

        @pl.when((f > 0) & (f < nf - 1))
        def _():
            acc_s[...] += part

        @pl.when(f == nf - 1)
        def _():
            o_ref[...] = acc_s[...] + part
            tok_copy(jnp.minimum(b + 2, n_blocks - 1), b % 2).start()

    @pl.when(((b == n_active) & (f == 0)) | (active & (b == n_blocks - 1) & (f == nf - 1)))
    def _():
        wait_rows()
        tok_copy(0, jnp.where(active, b, b + 1) % 2).wait()

    @pl.when((b >= n_active) & (f == nf - 1))
    def _():
        o_ref[...] = jnp.zeros_like(o_ref)


def _ffn(h_all, slot_tok, block_expert, n_active, w1, b1, w2, b2):
    n_blocks = slot_tok.shape[0]
    nf = D_FF // MOE_TF

    def widx(col_off):
        def index_map(b, f, be, na):
            live = b < na[0]
            bb = jnp.where(live, b, na[0] - 1)
            ff = jnp.where(live, f, nf - 1)
            return be[bb], 0, col_off + ff
        return index_map

    def w2idx(b, f, be, na):
        live = b < na[0]
        return be[jnp.where(live, b, na[0] - 1)], jnp.where(live, f, nf - 1), 0

    def b2idx(b, f, be, na):
        return be[jnp.where(b < na[0], b, na[0] - 1)], 0, 0

    grid_spec = pltpu.PrefetchScalarGridSpec(
        num_scalar_prefetch=2,
        grid=(n_blocks, nf),
        in_specs=[pl.BlockSpec(memory_space=pl.ANY), pl.BlockSpec(memory_space=pl.ANY),
                  pl.BlockSpec((1, D_MODEL, MOE_TF), widx(0)), pl.BlockSpec((1, D_MODEL, MOE_TF), widx(nf)),
                  pl.BlockSpec((1, 1, MOE_TF), widx(0)), pl.BlockSpec((1, 1, MOE_TF), widx(nf)),
                  pl.BlockSpec((1, MOE_TF, D_MODEL), w2idx), pl.BlockSpec((1, 1, D_MODEL), b2idx)],
        out_specs=pl.BlockSpec((MOE_BLK, D_MODEL), lambda b, f, be, na: (b, 0)),
        scratch_shapes=[pltpu.SMEM((2, MOE_BLK), I32),
                        pltpu.VMEM((MOE_BLK, D_MODEL), F32),
                        pltpu.VMEM((MOE_BLK, D_MODEL), BF16),
                        pltpu.VMEM((MOE_BLK, D_MODEL), F32),
                        pltpu.SemaphoreType.DMA((2,)), pltpu.SemaphoreType.DMA])
    return pl.pallas_call(
        _ffn_kernel,
        out_shape=jax.ShapeDtypeStruct((n_blocks * MOE_BLK, D_MODEL), F32),
        grid_spec=grid_spec,
        compiler_params=_cparams(("arbitrary", "arbitrary")),
        name="expert_ffn",
    )(block_expert, n_active, slot_tok, h_all, w1, w1, b1[:, None, :], b1[:, None, :], w2, b2[:, None, :])


def _route(top_idx, rank, counts):
    n_tok = top_idx.shape[0]
    n_assign = n_tok * TOP_K
    e_flat = top_idx.reshape(-1)
    rank = rank.reshape(-1)
    onehot = (e_flat[:, None] == jnp.arange(N_EXPERTS)[None, :]).astype(I32)
    nblk = (counts + MOE_BLK - 1) // MOE_BLK
    blk_end = jnp.cumsum(nblk)
    pad_start = (blk_end - nblk) * MOE_BLK
    dest = jnp.sum(onehot * pad_start[None, :], axis=1) + rank
    n_blocks = -(-n_assign // MOE_BLK) + N_EXPERTS
    slot_tok = jnp.zeros((n_blocks * MOE_BLK,), I32).at[dest].set(
        jnp.arange(n_assign, dtype=I32) // TOP_K, unique_indices=True)
    block_expert = jnp.minimum(jnp.sum(blk_end[None, :] <= jnp.arange(n_blocks)[:, None], axis=1),
                               N_EXPERTS - 1).astype(I32)
    n_active = blk_end[-1:].astype(I32)
    return dest.reshape(n_tok, TOP_K).astype(I32), slot_tok.reshape(n_blocks, MOE_BLK), block_expert, n_active


def _combine_kernel(dest_hbm, y_hbm, x1_ref, tg_ref, gate_ref, gf_ref, o_ref, dest_s, rows_s, dest_sem, row_sem,
                    *, tc):
    i = pl.program_id(0)
    n = pl.num_programs(0)
    n_rows = TOP_K * tc
    slot = i % 2
    nxt = jnp.minimum(i + 1, n - 1)

    def dest_copy(tile, sl):
        return pltpu.make_async_copy(dest_hbm.at[tile], dest_s.at[sl], dest_sem.at[sl])

    def row_copy(src, r, sl):
        return pltpu.make_async_copy(y_hbm.at[pl.ds(src, 1)], rows_s.at[sl, pl.ds(r, 1)], row_sem.at[sl])

    def wait_rows(sl):
        for r in range(n_rows):
            row_copy(0, r, sl).wait()

    @pl.when(i == 0)
    def _():
        first = dest_copy(0, 0)
        first.start()
        first.wait()

        def body(r, c):
            row_copy(dest_s[0, r], r, 0).start()
            return c
        lax.fori_loop(0, n_rows, body, 0, unroll=8)
        dest_copy(nxt, 1).start()

    def step(cur):
        dest_copy(nxt, 1 - cur).wait()
        wait_rows(cur)
        for r in range(n_rows):
            row_copy(dest_s[1 - cur, r], r, 1 - cur).start()
        tg = tg_ref[...]
        ffn = tg[:, 0:1] * rows_s[cur, 0:tc, :]
        for k in range(1, TOP_K):
            ffn = ffn + tg[:, k:k + 1] * rows_s[cur, k * tc:(k + 1) * tc, :]
        x2 = x1_ref[...] + gate_ref[0] * ffn
        o_ref[...] = x2 * lax.rsqrt(jnp.mean(x2 * x2, axis=-1, keepdims=True) + EPS) * gf_ref[...]
        dest_copy(jnp.minimum(i + 2, n - 1), cur).start()

        @pl.when(i == n - 1)
        def _():
            wait_rows(1 - cur)
            dest_copy(0, cur).wait()

    for cur in range(2):
        pl.when(slot == cur)(functools.partial(step, cur))


def _combine(dest, y_sorted, x1, top_gate, gate2, g_final, seq_len):
    m = x1.shape[0]
    tc = _row_tile(256, m, seq_len)
    dest_t = dest.reshape(m // tc, tc, TOP_K).transpose(0, 2, 1).reshape(m // tc, TOP_K * tc)
    row = lambda width: pl.BlockSpec((tc, width), lambda i: (i, 0))
    return pl.pallas_call(
        functools.partial(_combine_kernel, tc=tc),
        out_shape=jax.ShapeDtypeStruct((m, D_MODEL), F32),
        grid=(m // tc,),
        in_specs=[pl.BlockSpec(memory_space=pl.ANY), pl.BlockSpec(memory_space=pl.ANY),
                  row(D_MODEL), row(LANES), _mod_spec(tc, seq_len), pl.BlockSpec((1, D_MODEL), lambda i: (0, 0))],
        out_specs=row(D_MODEL),
        scratch_shapes=[pltpu.SMEM((2, TOP_K * tc), I32), pltpu.VMEM((2, TOP_K * tc, D_MODEL), F32),
                        pltpu.SemaphoreType.DMA((2,)), pltpu.SemaphoreType.DMA((2,))],
        compiler_params=_cparams(("arbitrary",)),
        name="moe_combine",
    )(dest_t, y_sorted, x1, top_gate, _mod_arr(gate2, seq_len), g_final.reshape(1, D_MODEL))


def _reorder_w_in(w_in):
    w = w_in.astype(BF16)
    dt_cols = jnp.pad(w[:, XBC_END:DT_END], ((0, 0), (0, DT_PAD - SSM_HEADS)))
    return jnp.concatenate([w[:, :XBC_END], w[:, DT_END:], dt_cols], axis=1)


def _mixer_tail(x, y_ssd, y_att, proj, mod, lw, seq_len, counts0, tail=None):
    pre = _merge(y_ssd, y_att, proj, lw['w_br_ssd'], lw['w_br_att'])
    return _outproj(pre, x, mod[2], mod[3], mod[4], lw['g_norm2'], lw['w_out'], lw['w_router'], lw['b_router'],
                    seq_len, counts0, tail)


def _forward(x_prompt, x_sample, state_ssm, state_conv, cache_win_k, cache_win_v, c_prompt, c_sample, lw, g_final):
    bp, lp, d = x_prompt.shape
    bs = x_sample.shape[0]
    mp = bp * lp
    xp = x_prompt.reshape(mp, d)
    xs = x_sample.reshape(bs, d)

    mod = _ada(jnp.concatenate([c_prompt, c_sample], axis=0), lw['w_ada'], lw['b_ada'])
    mod_p = [mod[:bp, i * d:(i + 1) * d] for i in range(N_MOD)]
    mod_s = [mod[bp:, i * d:(i + 1) * d] for i in range(N_MOD)]

    proj_p = _inproj(xp, mod_p[0], mod_p[1], lw['g_norm1'], lw['w_in'], lp)
    proj_s = _inproj(xs, mod_s[0], mod_s[1], lw['g_norm1'], lw['w_in'], 1)

    yssd_p, ssm_p, conv_p, w_e_out = _ssd_prompt(proj_p, bp, lp, lw, lw['w_expert_out'])
    yatt_p, wk_p, wv_p, w_e_in = _attn_prompt(proj_p, bp, lp, lw['sinks'], lw['w_expert_in'])
    yssd_s, ssm_s, conv_s = _ssd_sample(proj_s, state_ssm, state_conv, lw)
    yatt_s, wk_s, wv_s = _attn_sample(proj_s, cache_win_k, cache_win_v, lw['sinks'])

    x1_s, h2_s, tg_s, ti_s, tr_s, cnt_s = _mixer_tail(xs, yssd_s, yatt_s, proj_s, mod_s, lw, 1,
                                                      jnp.zeros((1, LANES), F32))
    x1_p, h_all, tg_p, ti_p, tr_p, cnt_all = _mixer_tail(xp, yssd_p, yatt_p, proj_p, mod_p, lw, lp, cnt_s, h2_s)
    top_idx = jnp.concatenate([ti_p[:, :TOP_K], ti_s[:, :TOP_K]], axis=0)
    rank = jnp.concatenate([tr_p[:, :TOP_K], tr_s[:, :TOP_K]], axis=0)
    dest, slot_tok, block_expert, n_active = _route(top_idx, rank, cnt_all[0, :N_EXPERTS].astype(I32))
    y_sorted = _ffn(h_all, slot_tok, block_expert, n_active,
                    w_e_in, lw['b_expert_in'], w_e_out, lw['b_expert_out'])

    y_p = _combine(dest[:mp], y_sorted, x1_p, tg_p, mod_p[5], g_final, lp)
    y_s = _combine(dest[mp:], y_sorted, x1_s, tg_s, mod_s[5], g_final, 1)
    return (y_p.reshape(bp, lp, d), y_s.reshape(bs, 1, d),
            ssm_p[None], conv_p[None], wk_p[None], wv_p[None],
            ssm_s[None], conv_s[None], wk_s[None], wv_s[None])


def kernel(x_prompt, x_sample, state_ssm, state_conv, cache_win_k, cache_win_v, c_prompt, c_sample, w_ada, b_ada, g_norm1, w_in, w_conv, b_conv, dt_bias, a_log, d_skip, g_ssm_norm, sinks, w_br_ssd, w_br_att, w_out, g_norm2, w_router, b_router, w_expert_in, b_expert_in, w_expert_out, b_expert_out, g_final):
    assert w_ada.shape[0] == 1, "single-layer stack"
    lw = dict(w_ada=w_ada[0], b_ada=b_ada[0], g_norm1=g_norm1[0], w_in=_reorder_w_in(w_in[0]),
              w_conv=w_conv[0], b_conv=b_conv[0], dt_bias=dt_bias[0], a_log=a_log[0], d_skip=d_skip[0],
              g_ssm_norm=g_ssm_norm[0], sinks=sinks[0],
              w_br_ssd=w_br_ssd[0].astype(BF16), w_br_att=w_br_att[0].astype(BF16), w_out=w_out[0].astype(BF16),
              g_norm2=g_norm2[0], w_router=w_router[0], b_router=b_router[0],
              w_expert_in=w_expert_in[0], b_expert_in=b_expert_in[0],
              w_expert_out=w_expert_out[0], b_expert_out=b_expert_out[0])
    return _forward(x_prompt, x_sample, state_ssm[0], state_conv[0], cache_win_k[0], cache_win_v[0],
                    c_prompt, c_sample, lw, g_final)
```

```python
import functools
import math

import jax
import jax.numpy as jnp
from jax import lax
from jax.experimental import pallas as pl
from jax.experimental.pallas import tpu as pltpu

F32 = jnp.float32
BF16 = jnp.bfloat16
I32 = jnp.int32
HIGHEST = lax.Precision.HIGHEST

D_MODEL = 2048
PAST_LEN = 16384
D_INNER = 2 * D_MODEL
SSM_HEAD_DIM = 64
SSM_HEADS = D_INNER // SSM_HEAD_DIM
SSM_GROUPS = 8
HEADS_PER_GROUP = SSM_HEADS // SSM_GROUPS
GROUP_WIDTH = HEADS_PER_GROUP * SSM_HEAD_DIM
SSM_STATE = 128
CONV_WIDTH = 4
BC_WIDTH = SSM_GROUPS * SSM_STATE
CONV_DIM = D_INNER + 2 * BC_WIDTH
CHUNK = 128
ATTN_HEADS = 32
KV_HEADS = 8
HEAD_DIM = 64
KV_WIDTH = KV_HEADS * HEAD_DIM
WINDOW = 128
ROPE_DIM = HEAD_DIM // 4
ROPE_THETA = 500000.0
N_EXPERTS = 32
TOP_K = 4
D_FF = D_MODEL
SWIGLU_LIMIT = 7.0
SWIGLU_ALPHA = 1.702
N_MOD = 6
EPS = 1e-6
Z_END = D_INNER
XBC_END = Z_END + CONV_DIM
DT_END = XBC_END + SSM_HEADS
Q_END = DT_END + D_MODEL
K_END = Q_END + KV_WIDTH
V_END = K_END + KV_WIDTH

LANES = 128
BF16_SUBLANES = 16
VMEM_LIMIT = 56 * 1024 * 1024

PC_Z = 0
PC_X = D_INNER
PC_B = PC_X + D_INNER
PC_C = PC_B + BC_WIDTH
PC_Q = PC_C + BC_WIDTH
PC_K = PC_Q + D_MODEL
PC_V = PC_K + KV_WIDTH
PC_G = PC_V + KV_WIDTH
PC_DT = PC_G + 2 * D_MODEL
DT_PAD = 512
PROJ_W = PC_DT + DT_PAD
IN_TN = 1280

MOE_BLK = 512
MOE_TF = 1024


def _cparams(sem):
    return pltpu.CompilerParams(dimension_semantics=sem, vmem_limit_bytes=VMEM_LIMIT)


def _silu(x):
    return x * jax.nn.sigmoid(x)


def _ada_kernel(c_ref, w_ref, b_ref, o_ref):
    s = _silu(c_ref[...]).astype(BF16)
    o_ref[...] = jnp.dot(s, w_ref[...].astype(BF16), preferred_element_type=F32) + b_ref[...]


def _ada(c, w_ada, b_ada):
    n, d = c.shape
    nout = w_ada.shape[1]
    tn = 1024
    return pl.pallas_call(
        _ada_kernel,
        out_shape=jax.ShapeDtypeStruct((n, nout), F32),
        grid=(nout // tn,),
        in_specs=[pl.BlockSpec((n, d), lambda j: (0, 0)),
                  pl.BlockSpec((d, tn), lambda j: (0, j)),
                  pl.BlockSpec((1, tn), lambda j: (0, j))],
        out_specs=pl.BlockSpec((n, tn), lambda j: (0, j)),
        compiler_params=_cparams(("arbitrary",)),
        name="ada_mod",
    )(c, w_ada, b_ada.reshape(1, nout))


def _inproj_kernel(x_ref, sh_ref, sc_ref, g_ref, w_ref, o_ref, h_ref):
    @pl.when(pl.program_id(1) == 0)
    def _():
        x = x_ref[...]
        y = x * lax.rsqrt(jnp.mean(x * x, axis=-1, keepdims=True) + EPS) * g_ref[...]
        h_ref[...] = (y * (1.0 + sc_ref[0]) + sh_ref[0]).astype(BF16)

    o_ref[...] = jnp.dot(h_ref[...], w_ref[...], preferred_element_type=F32)


def _narrow_job(a, n_steps, step_of):
    flat = a.reshape(-1, a.shape[-1])
    rows = flat.shape[0]
    slab = pl.cdiv(pl.cdiv(rows, n_steps), BF16_SUBLANES) * BF16_SUBLANES
    last = pl.cdiv(rows, slab) - 1
    spec = pl.BlockSpec((slab, flat.shape[1]), lambda *g: (jnp.minimum(step_of(*g), last), 0))
    return flat, spec, jax.ShapeDtypeStruct(flat.shape, BF16)


def _mod_spec(tm, seq_len):
    if seq_len == 1:
        return pl.BlockSpec((1, tm, D_MODEL), lambda i, *_: (0, i, 0))
    return pl.BlockSpec((1, 1, D_MODEL), lambda i, *_: ((i * tm) // seq_len, 0, 0))


def _row_tile(cap, m, seq_len):
    tm = min(cap, m if seq_len == 1 else seq_len)
    assert m % tm == 0 and (seq_len == 1 or seq_len % tm == 0)
    return tm


def _mod_arr(m, seq_len):
    return m[None] if seq_len == 1 else m[:, None, :]


def _inproj(x, shift, scale, g, w_in_r, seq_len):
    m = x.shape[0]
    tm = _row_tile(1024, m, seq_len)
    return pl.pallas_call(
        _inproj_kernel,
        out_shape=jax.ShapeDtypeStruct((m, PROJ_W), F32),
        grid=(m // tm, PROJ_W // IN_TN),
        in_specs=[pl.BlockSpec((tm, D_MODEL), lambda i, j: (i, 0)),
                  _mod_spec(tm, seq_len), _mod_spec(tm, seq_len),
                  pl.BlockSpec((1, D_MODEL), lambda i, j: (0, 0)),
                  pl.BlockSpec((D_MODEL, IN_TN), lambda i, j: (0, j))],
        out_specs=pl.BlockSpec((tm, IN_TN), lambda i, j: (i, j)),
        scratch_shapes=[pltpu.VMEM((tm, D_MODEL), BF16)],
        compiler_params=_cparams(("parallel", "arbitrary")),
        name="in_proj",
    )(x, _mod_arr(shift, seq_len), _mod_arr(scale, seq_len), g.reshape(1, D_MODEL), w_in_r)


def _softplus(x):
    return jnp.maximum(x, 0.0) + jnp.log1p(jnp.exp(-jnp.abs(x)))


def _split3(a):
    hi = a.astype(BF16)
    r = a - hi.astype(F32)
    mid = r.astype(BF16)
    lo = (r - mid.astype(F32)).astype(BF16)
    return hi, mid, lo


def _spread(pieces, ones2):
    hi, mid, lo = pieces
    out = jnp.dot(jnp.concatenate([hi, mid], axis=1), ones2, preferred_element_type=F32)
    if lo is not None:
        out = out + jnp.dot(lo, ones2[0:LANES], preferred_element_type=F32)
    return out


def _ssd_kernel(z_ref, x_ref, b_ref, c_ref, dt_ref,
                cw_ref, cb_ref, dtb_ref, alog_ref, dskip_ref, gn_ref, exp_ref, sel_ref, tri_ref, wide_ref,
                y_ref, st_ref, conv_ref, narrow_ref,
                ext_s, xs_s, xdt_s, eax_s, tex_s, elx_s, acst_s, bm_s, cm_s, state_s):
    ci = pl.program_id(1)
    last_chunk = ci == pl.num_programs(1) - 1
    narrow_ref[...] = wide_ref[...].astype(BF16)

    @pl.when(ci == 0)
    def _():
        ext_s[0:8, :] = jnp.zeros((8, CONV_DIM), F32)
        state_s[...] = jnp.zeros_like(state_s)

    ext_s[8:8 + CHUNK, 0:D_INNER] = x_ref[...]
    ext_s[8:8 + CHUNK, D_INNER:D_INNER + BC_WIDTH] = b_ref[...]
    ext_s[8:8 + CHUNK, D_INNER + BC_WIDTH:CONV_DIM] = c_ref[...]

    def conv(lo, hi):
        acc = cb_ref[:, lo:hi] + cw_ref[3:4, lo:hi] * ext_s[8:8 + CHUNK, lo:hi]
        for j in range(CONV_WIDTH - 1):
            acc = acc + cw_ref[j:j + 1, lo:hi] * ext_s[5 + j:5 + j + CHUNK, lo:hi]
        return _silu(acc)

    xs_s[...] = conv(0, D_INNER)
    bm_s[...] = conv(D_INNER, D_INNER + BC_WIDTH)
    cm_s[...] = conv(D_INNER + BC_WIDTH, CONV_DIM)
    tail = ext_s[5 + CHUNK:8 + CHUNK, :]
    conv_ref[0] = tail
    ext_s[5:8, :] = tail

    dt = _softplus(dt_ref[:, 0:LANES] + dtb_ref[...])
    da = dt * (-jnp.exp(alog_ref[...]))
    tri = tri_ref[...]
    acs = sum(jnp.dot(tri, p, preferred_element_type=F32) for p in _split3(da))
    acst_s[...] = acs.T
    acs_pieces = _split3(acs)
    last = acs[CHUNK - 1:CHUNK, :]
    two = lambda a: _split3(a)[:2] + (None,)
    exp2 = exp_ref[...]
    xdt_s[...] = xs_s[...] * _spread(two(dt), exp2)
    eax_s[...] = _spread(two(jnp.exp(acs)), exp2)
    tex_s[...] = _spread(two(jnp.exp(last - acs)), exp2)
    elx_s[...] = _spread(_split3(jnp.broadcast_to(jnp.exp(last), (8, LANES))), exp2)

    row = lax.broadcasted_iota(I32, (CHUNK, CHUNK), 0)
    col = lax.broadcasted_iota(I32, (CHUNK, CHUNK), 1)
    causal = row >= col
    low_half = col < SSM_HEAD_DIM

    def group(g, carry):
        o512 = pl.multiple_of(g * GROUP_WIDTH, GROUP_WIDTH)
        o128 = pl.multiple_of(g * SSM_STATE, SSM_STATE)
        bg = bm_s[:, pl.ds(o128, SSM_STATE)]
        cg16 = cm_s[:, pl.ds(o128, SSM_STATE)].astype(BF16)
        bg16 = bg.astype(BF16)
        cb = lax.dot_general(cg16, bg16, (((1,), (1,)), ((), ())), preferred_element_type=F32)
        sel2 = sel_ref[:, pl.ds(pl.multiple_of(g * HEADS_PER_GROUP * CHUNK, HEADS_PER_GROUP * CHUNK),
                                HEADS_PER_GROUP * CHUNK)]
        colb = _spread(acs_pieces, sel2)
        rows = acst_s[pl.ds(pl.multiple_of(g * HEADS_PER_GROUP, HEADS_PER_GROUP), HEADS_PER_GROUP), :]
        xdt_g = xdt_s[:, pl.ds(o512, GROUP_WIDTH)]
        parts = []
        for jp in range(HEADS_PER_GROUP // 2):
            ms = []
            for j in (2 * jp, 2 * jp + 1):
                diff = colb[:, j * CHUNK:(j + 1) * CHUNK] - rows[j:j + 1, :]
                dec = jnp.exp(jnp.where(causal, diff, -jnp.inf))
                ms.append((cb * dec).astype(BF16))
            xp = xdt_g[:, jp * LANES:(jp + 1) * LANES]
            rhs = jnp.concatenate([jnp.where(low_half, xp, 0.0), jnp.where(low_half, 0.0, xp)], axis=0)
            parts.append(jnp.dot(jnp.concatenate(ms, axis=1), rhs.astype(BF16), preferred_element_type=F32))
        y = jnp.concatenate(parts, axis=1)

        st = state_s[:, pl.ds(o512, GROUP_WIDTH)]
        y = y + jnp.dot(cg16, st.astype(BF16), preferred_element_type=F32) * eax_s[:, pl.ds(o512, GROUP_WIDTH)]
        xw = (xdt_g * tex_s[:, pl.ds(o512, GROUP_WIDTH)]).astype(BF16)
        st_new = (st * elx_s[0:1, pl.ds(o512, GROUP_WIDTH)]
                  + jnp.dot(bg.T.astype(BF16), xw, preferred_element_type=F32))
        state_s[:, pl.ds(o512, GROUP_WIDTH)] = st_new

        @pl.when(last_chunk)
        def _():
            st_ref[0, pl.ds(o512, GROUP_WIDTH), :] = st_new.T

        y = y + dskip_ref[:, pl.ds(o512, GROUP_WIDTH)] * xs_s[:, pl.ds(o512, GROUP_WIDTH)]
        u = y * _silu(z_ref[:, pl.ds(o512, GROUP_WIDTH)])
        u = u * lax.rsqrt(jnp.mean(u * u, axis=-1, keepdims=True) + EPS)
        y_ref[:, pl.ds(o512, GROUP_WIDTH)] = (u * gn_ref[:, pl.ds(o512, GROUP_WIDTH)]).astype(BF16)
        return carry

    lax.fori_loop(0, SSM_GROUPS, group, 0, unroll=4)


def _pad_heads(v):
    return jnp.pad(v.astype(F32), (0, LANES - SSM_HEADS)).reshape(1, LANES)


def _head_expand(dtype=BF16, copies=2):
    h = jnp.arange(LANES)[:, None]
    ch = jnp.arange(D_INNER)[None, :] // SSM_HEAD_DIM
    return jnp.tile((h == ch).astype(dtype), (copies, 1))


def _head_select():
    h = jnp.arange(LANES)[:, None]
    blk = jnp.arange(SSM_HEADS * CHUNK)[None, :] // CHUNK
    return jnp.tile((h == blk).astype(BF16), (2, 1))


def _ssd_prompt(proj, bsz, seq_len, lw, wide):
    nc = seq_len // CHUNK
    m = bsz * seq_len
    tri = (jnp.arange(CHUNK)[:, None] >= jnp.arange(CHUNK)[None, :]).astype(BF16)
    const = lambda shape: pl.BlockSpec(shape, lambda b, c: (0, 0))
    rowblk = lambda width, col: pl.BlockSpec((CHUNK, width), lambda b, c: (b * nc + c, col // width))
    wide2d, wide_spec, narrow_shape = _narrow_job(wide, bsz * nc, lambda b, c: b * nc + c)
    y, st, conv, narrow = pl.pallas_call(
        _ssd_kernel,
        out_shape=(jax.ShapeDtypeStruct((m, D_INNER), BF16),
                   jax.ShapeDtypeStruct((bsz, D_INNER, SSM_STATE), F32),
                   jax.ShapeDtypeStruct((bsz, CONV_WIDTH - 1, CONV_DIM), F32), narrow_shape),
        grid=(bsz, nc),
        in_specs=[rowblk(D_INNER, PC_Z), rowblk(D_INNER, PC_X), rowblk(BC_WIDTH, PC_B), rowblk(BC_WIDTH, PC_C),
                  rowblk(DT_PAD, PC_DT),
                  const((CONV_WIDTH, CONV_DIM)), const((1, CONV_DIM)), const((1, LANES)), const((1, LANES)),
                  const((1, D_INNER)), const((1, D_INNER)), const((2 * LANES, D_INNER)),
                  const((2 * LANES, SSM_HEADS * CHUNK)), const((CHUNK, CHUNK)), wide_spec],
        out_specs=(pl.BlockSpec((CHUNK, D_INNER), lambda b, c: (b * nc + c, 0)),
                   pl.BlockSpec((1, D_INNER, SSM_STATE), lambda b, c: (b, 0, 0)),
                   pl.BlockSpec((1, CONV_WIDTH - 1, CONV_DIM), lambda b, c: (b, 0, 0)), wide_spec),
        scratch_shapes=[pltpu.VMEM((8 + CHUNK, CONV_DIM), F32),
                        pltpu.VMEM((CHUNK, D_INNER), F32), pltpu.VMEM((CHUNK, D_INNER), F32),
                        pltpu.VMEM((CHUNK, D_INNER), F32), pltpu.VMEM((CHUNK, D_INNER), F32),
                        pltpu.VMEM((8, D_INNER), F32), pltpu.VMEM((LANES, CHUNK), F32),
                        pltpu.VMEM((CHUNK, BC_WIDTH), F32), pltpu.VMEM((CHUNK, BC_WIDTH), F32),
                        pltpu.VMEM((SSM_STATE, D_INNER), F32)],
        compiler_params=_cparams(("arbitrary", "arbitrary")),
        name="ssd_chunks",
    )(proj, proj, proj, proj, proj,
      lw['w_conv'], lw['b_conv'].reshape(1, CONV_DIM), _pad_heads(lw['dt_bias']), _pad_heads(lw['a_log']),
      jnp.repeat(lw['d_skip'].astype(F32), SSM_HEAD_DIM).reshape(1, D_INNER),
      lw['g_ssm_norm'].reshape(1, D_INNER), _head_expand(), _head_select(), tri, wide2d)
    return y, st.reshape(bsz, SSM_HEADS, SSM_HEAD_DIM, SSM_STATE), conv, narrow.reshape(wide.shape)


def _ssd_step_pre_kernel(x_ref, b_ref, c_ref, dt_ref, p0_ref, p1_ref, p2_ref,
                         cw_ref, cb_ref, dtb_ref, alog_ref, exp_ref,
                         xs_ref, xdt_ref, bm_ref, cm_ref, dec_ref, conv_ref):
    def conv(u_ref, lo, hi):
        acc = cb_ref[:, lo:hi] + cw_ref[3:4, lo:hi] * u_ref[...]
        for j, p_ref in enumerate((p0_ref, p1_ref, p2_ref)):
            acc = acc + cw_ref[j:j + 1, lo:hi] * p_ref[:, lo:hi]
        return _silu(acc)

    xs = conv(x_ref, 0, D_INNER)
    xs_ref[...] = xs
    bm_ref[...] = conv(b_ref, D_INNER, D_INNER + BC_WIDTH)
    cm_ref[...] = conv(c_ref, D_INNER + BC_WIDTH, CONV_DIM)
    dt = _softplus(dt_ref[:, 0:LANES] + dtb_ref[...])
    dec_ref[...] = jnp.exp(dt * (-jnp.exp(alog_ref[...])))
    xdt_ref[...] = xs * jnp.dot(dt, exp_ref[...], precision=HIGHEST, preferred_element_type=F32)
    conv_ref[0] = p1_ref[...]
    conv_ref[1] = p2_ref[...]
    conv_ref[2, :, 0:D_INNER] = x_ref[...]
    conv_ref[2, :, D_INNER:D_INNER + BC_WIDTH] = b_ref[...]
    conv_ref[2, :, D_INNER + BC_WIDTH:CONV_DIM] = c_ref[...]


def _ssd_step_kernel(dec_ref, st_ref, xdtt_ref, bm_ref, cmt_ref, xs_ref, z_ref, dskip_ref, gn_ref,
                     sto_ref, y_ref, yt_s, *, tb):
    i = pl.program_id(0)
    nb = xs_ref.shape[0]
    sub = lax.broadcasted_iota(I32, (nb, SSM_STATE), 0)
    lane = lax.broadcasted_iota(I32, (SSM_STATE, nb), 1)

    @pl.when(i == 0)
    def _():
        yt_s[...] = jnp.zeros_like(yt_s)

    for bi in range(tb):
        b = i * tb + bi
        for g in range(SSM_GROUPS):
            r0 = g * GROUP_WIDTH
            b_row = jnp.where(sub == b, bm_ref[:, g * SSM_STATE:(g + 1) * SSM_STATE], 0.0).astype(BF16)
            term = jnp.dot(xdtt_ref[r0:r0 + GROUP_WIDTH, :].astype(BF16), b_row, preferred_element_type=F32)
            news = []
            for j in range(HEADS_PER_GROUP):
                h = g * HEADS_PER_GROUP + j
                rr = r0 + j * SSM_HEAD_DIM
                new = (st_ref[bi, rr:rr + SSM_HEAD_DIM, :] * dec_ref[b, h]
                       + term[j * SSM_HEAD_DIM:(j + 1) * SSM_HEAD_DIM, :])
                sto_ref[bi, rr:rr + SSM_HEAD_DIM, :] = new
                news.append(new)
            new_g = jnp.concatenate(news, axis=0).astype(BF16)
            c_col = jnp.where(lane == b, cmt_ref[g], 0.0).astype(BF16)
            yt_s[r0:r0 + GROUP_WIDTH, :] += jnp.dot(new_g, c_col, preferred_element_type=F32)

    @pl.when(i == pl.num_programs(0) - 1)
    def _():
        for g in range(SSM_GROUPS):
            sl = slice(g * GROUP_WIDTH, (g + 1) * GROUP_WIDTH)
            y = yt_s[sl, :].T + dskip_ref[:, sl] * xs_ref[:, sl]
            u = y * _silu(z_ref[:, sl])
            u = u * lax.rsqrt(jnp.mean(u * u, axis=-1, keepdims=True) + EPS)
            y_ref[:, sl] = (u * gn_ref[:, sl]).astype(BF16)


def _ssd_sample(proj, state_ssm, state_conv, lw):
    nb = proj.shape[0]
    full = lambda shape: pl.BlockSpec(shape, lambda *_: tuple(0 for _ in shape))
    colblk = lambda width, col: pl.BlockSpec((nb, width), lambda *_: (0, col // width))
    prev = [state_conv[:, j, :] for j in range(CONV_WIDTH - 1)]
    tr = min(32, nb)
    rows = lambda width, col=0: pl.BlockSpec((tr, width), lambda i: (i, col // width))
    xs, xdt, bm, cm, dec, conv = pl.pallas_call(
        _ssd_step_pre_kernel,
        out_shape=(jax.ShapeDtypeStruct((nb, D_INNER), F32), jax.ShapeDtypeStruct((nb, D_INNER), F32),
                   jax.ShapeDtypeStruct((nb, BC_WIDTH), F32), jax.ShapeDtypeStruct((nb, BC_WIDTH), F32),
                   jax.ShapeDtypeStruct((nb, LANES), F32),
                   jax.ShapeDtypeStruct((CONV_WIDTH - 1, nb, CONV_DIM), F32)),
        grid=(nb // tr,),
        in_specs=[rows(D_INNER, PC_X), rows(BC_WIDTH, PC_B), rows(BC_WIDTH, PC_C), rows(DT_PAD, PC_DT),
                  rows(CONV_DIM), rows(CONV_DIM), rows(CONV_DIM),
                  full((CONV_WIDTH, CONV_DIM)), full((1, CONV_DIM)), full((1, LANES)), full((1, LANES)),
                  full((LANES, D_INNER))],
        out_specs=(rows(D_INNER), rows(D_INNER), rows(BC_WIDTH), rows(BC_WIDTH), rows(LANES),
                   pl.BlockSpec((CONV_WIDTH - 1, tr, CONV_DIM), lambda i: (0, i, 0))),
        compiler_params=_cparams(("parallel",)),
        name="ssd_step_pre",
    )(proj, proj, proj, proj, *prev,
      lw['w_conv'], lw['b_conv'].reshape(1, CONV_DIM), _pad_heads(lw['dt_bias']), _pad_heads(lw['a_log']),
      _head_expand(F32, 1))

    tb = 2
    cmt = cm.reshape(nb, SSM_GROUPS, SSM_STATE).transpose(1, 2, 0)
    st_new, y = pl.pallas_call(
        functools.partial(_ssd_step_kernel, tb=tb),
        out_shape=(jax.ShapeDtypeStruct((nb, D_INNER, SSM_STATE), F32),
                   jax.ShapeDtypeStruct((nb, D_INNER), BF16)),
        grid=(nb // tb,),
        in_specs=[pl.BlockSpec(memory_space=pltpu.SMEM),
                  pl.BlockSpec((tb, D_INNER, SSM_STATE), lambda i: (i, 0, 0)),
                  full((D_INNER, nb)), full((nb, BC_WIDTH)), full((SSM_GROUPS, SSM_STATE, nb)),
                  full((nb, D_INNER)), colblk(D_INNER, PC_Z), full((1, D_INNER)), full((1, D_INNER))],
        out_specs=(pl.BlockSpec((tb, D_INNER, SSM_STATE), lambda i: (i, 0, 0)),
                   full((nb, D_INNER))),
        scratch_shapes=[pltpu.VMEM((D_INNER, nb), F32)],
        compiler_params=_cparams(("arbitrary",)),
        name="ssd_step",
    )(dec, state_ssm.reshape(nb, D_INNER, SSM_STATE), xdt.T, bm, cmt, xs, proj,
      jnp.repeat(lw['d_skip'].astype(F32), SSM_HEAD_DIM).reshape(1, D_INNER),
      lw['g_ssm_norm'].reshape(1, D_INNER))
    return (y, st_new.reshape(nb, SSM_HEADS, SSM_HEAD_DIM, SSM_STATE), conv.transpose(1, 0, 2))


def _rope_tables(pos):
    half = ROPE_DIM // 2
    inv_freq = ROPE_THETA ** (-jnp.arange(half, dtype=F32) * 2.0 / ROPE_DIM)
    ang = pos.astype(F32)[:, None] * inv_freq[None, :]
    cos, sin = jnp.cos(ang), jnp.sin(ang)
    n = pos.shape[0]
    ones = jnp.ones((n, HEAD_DIM - ROPE_DIM), F32)
    zeros = jnp.zeros((n, HEAD_DIM - ROPE_DIM), F32)
    zh = jnp.zeros((n, half), F32)
    cos_f = jnp.concatenate([cos, cos, ones], axis=1)
    sin_a = jnp.concatenate([zh, sin, zeros], axis=1)
    sin_b = jnp.concatenate([-sin, zh, zeros], axis=1)
    tile = lambda t: jnp.tile(t, (1, LANES // HEAD_DIM))
    return tile(cos_f), tile(sin_a), tile(sin_b)


def _rope_slab(x, cos_f, sin_a, sin_b):
    half = ROPE_DIM // 2
    return x * cos_f + pltpu.roll(x, half, 1) * sin_a + pltpu.roll(x, LANES - half, 1) * sin_b


def _attn_kernel(sink_ref, q_ref, k_ref, v_ref, cos_ref, sin_a_ref, sin_b_ref, wide_ref,
                 y_ref, kn_ref, vn_ref, narrow_ref, kspan_s, vspan_s, sc_s, p_s):
    bi = pl.program_id(1)
    nq = q_ref.shape[0]
    narrow_ref[...] = wide_ref[...].astype(BF16)

    @pl.when(bi == 0)
    def _():
        kspan_s[0:nq, :] = jnp.zeros((nq, KV_WIDTH), F32)
        vspan_s[0:nq, :] = jnp.zeros((nq, KV_WIDTH), F32)

    cos_f, sin_a, sin_b = cos_ref[...], sin_a_ref[...], sin_b_ref[...]
    rope = lambda x: _rope_slab(x, cos_f, sin_a, sin_b)
    kr = jnp.concatenate([rope(k_ref[:, s * LANES:(s + 1) * LANES]) for s in range(KV_WIDTH // LANES)], axis=1)
    v = v_ref[...]
    kspan_s[nq:2 * nq, :] = kr
    vspan_s[nq:2 * nq, :] = v
    kn_ref[0] = kr
    vn_ref[0] = v

    t = lax.broadcasted_iota(I32, (nq, nq), 0)
    j = lax.broadcasted_iota(I32, (nq, nq), 1)
    own_block = j <= t
    low = lax.broadcasted_iota(I32, (2 * nq, LANES), 1) < HEAD_DIM
    scale = HEAD_DIM ** -0.5
    grp = ATTN_HEADS // KV_HEADS

    def block_diag(span_ref, kh):
        nat = span_ref[:, (kh // 2) * LANES:(kh // 2 + 1) * LANES]
        swp = pltpu.roll(nat, HEAD_DIM, 1)
        lo, hi = (nat, swp) if kh % 2 == 0 else (swp, nat)
        return jnp.concatenate([jnp.where(low, lo, 0.0), jnp.where(low, 0.0, hi)], axis=0).astype(BF16)

    for kh in range(KV_HEADS):
        kbd = block_diag(kspan_s, kh)
        for p in range(grp // 2):
            h0 = kh * grp + 2 * p
            qp = (rope(q_ref[:, h0 * HEAD_DIM:(h0 + 2) * HEAD_DIM]) * scale).astype(BF16)
            sc = lax.dot_general(qp, kbd, (((1,), (1,)), ((), ())), preferred_element_type=F32)
            for a in range(2):
                prev = sc[:, 2 * a * nq:(2 * a + 1) * nq]
                cur = sc[:, (2 * a + 1) * nq:(2 * a + 2) * nq]
                sc_s[h0 + a] = jnp.where(own_block, cur, prev)

    have_prev = bi > 0
    for h in range(ATTN_HEADS):
        sa = sc_s[h]
        sa = jnp.where(own_block | have_prev, sa, -jnp.inf)
        sink = sink_ref[h]
        mx = jnp.maximum(jnp.max(sa, axis=-1, keepdims=True), sink)
        pa = jnp.exp(sa - mx)
        pa = pa / (jnp.sum(pa, axis=-1, keepdims=True) + jnp.exp(sink - mx))
        p_s[h] = pa.astype(BF16)

    zero = jnp.zeros((nq, nq), BF16)
    for kh in range(KV_HEADS):
        vbd = block_diag(vspan_s, kh)
        for p in range(grp // 2):
            h0 = kh * grp + 2 * p
            parts = []
            for a in range(2):
                pa = p_s[h0 + a]
                parts += [jnp.where(own_block, zero, pa), jnp.where(own_block, pa, zero)]
            y_ref[:, h0 * HEAD_DIM:(h0 + 2) * HEAD_DIM] = jnp.dot(
                jnp.concatenate(parts, axis=1), vbd, preferred_element_type=F32).astype(BF16)

    kspan_s[0:nq, :] = kspan_s[nq:2 * nq, :]
    vspan_s[0:nq, :] = vspan_s[nq:2 * nq, :]


def _attn_prompt(proj, bsz, seq_len, sinks, wide):
    nq = CHUNK
    nb = seq_len // nq
    m = bsz * seq_len
    cos_f, sin_a, sin_b = _rope_tables(jnp.arange(seq_len))
    rowblk = lambda width, col: pl.BlockSpec((nq, width), lambda b, i: (b * nb + i, col // width))
    tab = pl.BlockSpec((nq, LANES), lambda b, i: (i, 0))
    wide2d, wide_spec, narrow_shape = _narrow_job(wide, bsz * nb, lambda b, i: b * nb + i)
    y, kn, vn, narrow = pl.pallas_call(
        _attn_kernel,
        out_shape=(jax.ShapeDtypeStruct((m, D_MODEL), BF16),
                   jax.ShapeDtypeStruct((bsz, WINDOW, KV_WIDTH), F32),
                   jax.ShapeDtypeStruct((bsz, WINDOW, KV_WIDTH), F32), narrow_shape),
        grid=(bsz, nb),
        in_specs=[pl.BlockSpec(memory_space=pltpu.SMEM),
                  rowblk(D_MODEL, PC_Q), rowblk(KV_WIDTH, PC_K), rowblk(KV_WIDTH, PC_V), tab, tab, tab,
                  wide_spec],
        out_specs=(pl.BlockSpec((nq, D_MODEL), lambda b, i: (b * nb + i, 0)),
                   pl.BlockSpec((1, WINDOW, KV_WIDTH), lambda b, i: (b, 0, 0)),
                   pl.BlockSpec((1, WINDOW, KV_WIDTH), lambda b, i: (b, 0, 0)), wide_spec),
        scratch_shapes=[pltpu.VMEM((2 * nq, KV_WIDTH), F32), pltpu.VMEM((2 * nq, KV_WIDTH), F32),
                        pltpu.VMEM((ATTN_HEADS, nq, nq), F32), pltpu.VMEM((ATTN_HEADS, nq, nq), BF16)],
        compiler_params=_cparams(("arbitrary", "arbitrary")),
        name="attn_blocks",
    )(sinks.astype(F32), proj, proj, proj, cos_f, sin_a, sin_b, wide2d)
    shape = (bsz, WINDOW, KV_HEADS, HEAD_DIM)
    return y, kn.reshape(shape), vn.reshape(shape), narrow.reshape(wide.shape)


def _attn_step_kernel(sink_ref, q_ref, k_ref, v_ref, cos_ref, sin_a_ref, sin_b_ref, fold_ref, foldt_ref,
                      kc_ref, vc_ref, y_ref, ko_ref, vo_ref, *, tb):
    cos_f, sin_a, sin_b = cos_ref[...], sin_a_ref[...], sin_b_ref[...]
    rope = lambda ref, w: jnp.concatenate(
        [_rope_slab(ref[:, s * LANES:(s + 1) * LANES], cos_f, sin_a, sin_b) for s in range(w // LANES)], axis=1)
    qr = rope(q_ref, D_MODEL) * (HEAD_DIM ** -0.5)
    kr = rope(k_ref, KV_WIDTH)
    v = v_ref[...]

    own = (lax.broadcasted_iota(I32, (ATTN_HEADS, D_MODEL), 1) // HEAD_DIM
           == lax.broadcasted_iota(I32, (ATTN_HEADS, D_MODEL), 0))
    qb = jnp.concatenate([jnp.where(own, jnp.broadcast_to(qr[bi:bi + 1], (ATTN_HEADS, D_MODEL)), 0.0)
                          for bi in range(tb)], axis=0).astype(BF16)
    qm = jnp.dot(qb, fold_ref[...], preferred_element_type=F32).astype(BF16)

    row = lax.broadcasted_iota(I32, (WINDOW, KV_WIDTH), 0)
    w = lax.broadcasted_iota(I32, (ATTN_HEADS, WINDOW), 1)
    sink = sink_ref[...]
    bf = lambda x: x.astype(BF16)
    outs = []
    for bi in range(tb):
        qmb = qm[bi * ATTN_HEADS:(bi + 1) * ATTN_HEADS]
        kc, vc = kc_ref[bi], vc_ref[bi]
        kn, vn = kr[bi:bi + 1], v[bi:bi + 1]
        sc = lax.dot_general(qmb, bf(kc), (((1,), (1,)), ((), ())), preferred_element_type=F32)
        sc = jnp.where(w >= 1, sc, -jnp.inf)
        s_new = jnp.sum(qmb.astype(F32) * bf(kn).astype(F32), axis=-1, keepdims=True)
        mx = jnp.maximum(jnp.maximum(jnp.max(sc, axis=-1, keepdims=True), s_new), sink)
        p = jnp.exp(sc - mx)
        p_new = jnp.exp(s_new - mx)
        den = jnp.sum(p, axis=-1, keepdims=True) + p_new + jnp.exp(sink - mx)
        p = p / den
        p_new = p_new / den
        out = jnp.dot(bf(p), bf(vc), preferred_element_type=F32) + bf(p_new).astype(F32) * bf(vn).astype(F32)
        outs.append(bf(out))
        ko_ref[bi] = jnp.where(row == WINDOW - 1, kn, pltpu.roll(kc, WINDOW - 1, 0))
        vo_ref[bi] = jnp.where(row == WINDOW - 1, vn, pltpu.roll(vc, WINDOW - 1, 0))

    of = jnp.dot(jnp.concatenate(outs, axis=0), foldt_ref[...], preferred_element_type=F32)
    of = of.reshape(tb, ATTN_HEADS, D_MODEL)
    y_ref[...] = jnp.sum(jnp.where(own[None], of, 0.0), axis=1).astype(BF16)


def _attn_sample(proj, cache_k, cache_v, sinks):
    nb = proj.shape[0]
    tb = 8
    grp = ATTN_HEADS // KV_HEADS
    cos_f, sin_a, sin_b = _rope_tables(jnp.full((1,), PAST_LEN))
    c = jnp.arange(D_MODEL)[:, None]
    l = jnp.arange(KV_WIDTH)[None, :]
    fold = ((c % HEAD_DIM == l % HEAD_DIM) & ((c // HEAD_DIM) // grp == l // HEAD_DIM)).astype(BF16)
    full = lambda shape: pl.BlockSpec(shape, lambda i: tuple(0 for _ in shape))
    rowblk = lambda width, col: pl.BlockSpec((tb, width), lambda i: (i, col // width))
    win = pl.BlockSpec((tb, WINDOW, KV_WIDTH), lambda i: (i, 0, 0))
    y, kn, vn = pl.pallas_call(
        functools.partial(_attn_step_kernel, tb=tb),
        out_shape=(jax.ShapeDtypeStruct((nb, D_MODEL), BF16),
                   jax.ShapeDtypeStruct((nb, WINDOW, KV_WIDTH), F32),
                   jax.ShapeDtypeStruct((nb, WINDOW, KV_WIDTH), F32)),
        grid=(nb // tb,),
        in_specs=[full((ATTN_HEADS, 1)), rowblk(D_MODEL, PC_Q), rowblk(KV_WIDTH, PC_K), rowblk(KV_WIDTH, PC_V),
                  full((1, LANES)), full((1, LANES)), full((1, LANES)),
                  full((D_MODEL, KV_WIDTH)), full((KV_WIDTH, D_MODEL)), win, win],
        out_specs=(pl.BlockSpec((tb, D_MODEL), lambda i: (i, 0)), win, win),
        compiler_params=_cparams(("parallel",)),
        name="attn_step",
    )(sinks.astype(F32).reshape(ATTN_HEADS, 1), proj, proj, proj, cos_f, sin_a, sin_b, fold, fold.T,
      cache_k.reshape(nb, WINDOW, KV_WIDTH), cache_v.reshape(nb, WINDOW, KV_WIDTH))
    shape = (nb, WINDOW, KV_HEADS, HEAD_DIM)
    return y, kn.reshape(shape), vn.reshape(shape)


def _merge_kernel(ys_ref, ya_ref, ws_ref, wa_ref, gs_ref, ga_ref, o_ref):
    a = jnp.dot(ys_ref[...], ws_ref[...], preferred_element_type=F32)
    b = jnp.dot(ya_ref[...], wa_ref[...], preferred_element_type=F32)
    o_ref[...] = (jax.nn.sigmoid(gs_ref[...]) * a + jax.nn.sigmoid(ga_ref[...]) * b).astype(BF16)


def _merge(y_ssd, y_att, proj, w_ssd, w_att):
    m = y_ssd.shape[0]
    tm = min(512, m)
    tn = 512
    g0 = PC_G // tn
    return pl.pallas_call(
        _merge_kernel,
        out_shape=jax.ShapeDtypeStruct((m, D_MODEL), BF16),
        grid=(m // tm, D_MODEL // tn),
        in_specs=[pl.BlockSpec((tm, D_INNER), lambda i, j: (i, 0)),
                  pl.BlockSpec((tm, D_MODEL), lambda i, j: (i, 0)),
                  pl.BlockSpec((D_INNER, tn), lambda i, j: (0, j)),
                  pl.BlockSpec((D_MODEL, tn), lambda i, j: (0, j)),
                  pl.BlockSpec((tm, tn), lambda i, j: (i, g0 + j)),
                  pl.BlockSpec((tm, tn), lambda i, j: (i, g0 + D_MODEL // tn + j))],
        out_specs=pl.BlockSpec((tm, tn), lambda i, j: (i, j)),
        compiler_params=_cparams(("parallel", "arbitrary")),
        name="branch_merge",
    )(y_ssd, y_att, w_ssd, w_att, proj, proj)


def _outproj_kernel(*refs, n_tiles, n_tail):
    if n_tail == 0:
        _outproj_tile(*refs)
        return
    n_in = 11
    tail_ref, own = refs[n_in], refs[:n_in] + refs[n_in + 1:]
    h2_ref = own[n_in + 1]
    i = pl.program_id(0)
    pl.when(i < n_tiles)(functools.partial(_outproj_tile, *own))

    @pl.when(i == n_tiles)
    def _():
        h2_ref[...] = jnp.zeros_like(h2_ref)
        h2_ref[0:n_tail, :] = tail_ref[...]


def _outproj_tile(pre_ref, x_ref, gate_ref, sh_ref, sc_ref, g_ref, w_ref, wr_ref, br_ref, before_ref, c0_ref,
                  x1_ref, h2_ref, tg_ref, ti_ref, tr_ref, cnt_ref, cnt_s):
    @pl.when(pl.program_id(0) == 0)
    def _():
        cnt_s[...] = c0_ref[...]

    mixed = jnp.dot(pre_ref[...], w_ref[...], preferred_element_type=F32)
    x1 = x_ref[...] + gate_ref[0] * mixed
    x1_ref[...] = x1
    y = x1 * lax.rsqrt(jnp.mean(x1 * x1, axis=-1, keepdims=True) + EPS) * g_ref[...]
    h2 = y * (1.0 + sc_ref[0]) + sh_ref[0]
    h2_ref[...] = h2
    h_hi, h_mid, _ = _split3(h2)
    hw = jnp.dot(h_hi, wr_ref[...], preferred_element_type=F32)
    logits = (hw[:, 0:LANES] + hw[:, LANES:2 * LANES]
              + jnp.dot(h_mid, wr_ref[:, 0:LANES], preferred_element_type=F32) + br_ref[...])
    lane = lax.broadcasted_iota(I32, logits.shape, 1)
    lane_f = lane.astype(F32)
    vals = jnp.zeros_like(logits)
    idxs = jnp.zeros_like(logits)
    top = None
    den = jnp.zeros((logits.shape[0], 1), F32)
    chosen = []
    for k in range(TOP_K):
        mx = jnp.max(logits, axis=-1, keepdims=True)
        ix = jnp.min(jnp.where(logits == mx, lane_f, float(LANES)), axis=-1, keepdims=True)
        top = mx if top is None else top
        e = jnp.exp(mx - top)
        den = den + e
        vals = jnp.where(lane == k, e, vals)
        idxs = jnp.where(lane == k, ix, idxs)
        chosen.append(lane_f == ix)
        logits = jnp.where(chosen[-1], -jnp.inf, logits)
    tg_ref[...] = vals / den
    ti_ref[...] = idxs.astype(I32)

    picked = jnp.zeros_like(vals)
    for c in chosen:
        picked = jnp.where(c, 1.0, picked)
    ahead = jnp.dot(before_ref[...], picked.astype(BF16), preferred_element_type=F32) + cnt_s[...]
    ranks = jnp.zeros_like(vals)
    for k, c in enumerate(chosen):
        ranks = jnp.where(lane == k, jnp.sum(jnp.where(c, ahead, 0.0), axis=-1, keepdims=True), ranks)
    tr_ref[...] = ranks.astype(I32)
    cnt_s[...] += jnp.sum(picked, axis=0, keepdims=True)
    cnt_ref[...] = cnt_s[...]


def _outproj(pre, x, gate1, shift2, scale2, g2, w_out, w_router, b_router, seq_len, counts0, tail=None):
    m = x.shape[0]
    tm = _row_tile(512, m, seq_len)
    nt = m // tm
    n_tail = 0 if tail is None else tail.shape[0]
    assert n_tail <= tm
    before = (jnp.arange(tm)[None, :] < jnp.arange(tm)[:, None]).astype(BF16)
    w_hi, w_mid, _ = _split3(jnp.pad(w_router.astype(F32), ((0, 0), (0, LANES - N_EXPERTS))))
    wr = jnp.concatenate([w_hi, w_mid], axis=1)
    br = jnp.pad(b_router.astype(F32), (0, LANES - N_EXPERTS), constant_values=-jnp.inf).reshape(1, LANES)
    own = lambda i: jnp.minimum(i, nt - 1)
    row = lambda width: pl.BlockSpec((tm, width), lambda i: (own(i), 0))
    const = lambda shape: pl.BlockSpec(shape, lambda i: (0, 0))
    if seq_len == 1:
        mod = pl.BlockSpec((1, tm, D_MODEL), lambda i: (0, own(i), 0))
    else:
        mod = pl.BlockSpec((1, 1, D_MODEL), lambda i: ((own(i) * tm) // seq_len, 0, 0))
    in_specs = [row(D_MODEL), row(D_MODEL), mod, mod, mod,
                const((1, D_MODEL)), const((D_MODEL, D_MODEL)), const((D_MODEL, 2 * LANES)), const((1, LANES)),
                const((tm, tm)), const((1, LANES))]
    args = [pre, x, _mod_arr(gate1, seq_len), _mod_arr(shift2, seq_len), _mod_arr(scale2, seq_len),
            g2.reshape(1, D_MODEL), w_out, wr, br, before, counts0]
    if n_tail:
        in_specs.append(const((n_tail, D_MODEL)))
        args.append(tail)
    return pl.pallas_call(
        functools.partial(_outproj_kernel, n_tiles=nt, n_tail=n_tail),
        out_shape=(jax.ShapeDtypeStruct((m, D_MODEL), F32), jax.ShapeDtypeStruct((m + n_tail, D_MODEL), F32),
                   jax.ShapeDtypeStruct((m, LANES), F32), jax.ShapeDtypeStruct((m, LANES), I32),
                   jax.ShapeDtypeStruct((m, LANES), I32), jax.ShapeDtypeStruct((1, LANES), F32)),
        grid=(nt + (1 if n_tail else 0),),
        in_specs=in_specs,
        out_specs=(row(D_MODEL), pl.BlockSpec((tm, D_MODEL), lambda i: (i, 0)), row(LANES), row(LANES), row(LANES),
                   const((1, LANES))),
        scratch_shapes=[pltpu.VMEM((1, LANES), F32)],
        compiler_params=_cparams(("arbitrary",)),
        name="out_proj_router",
    )(*args)


def _ffn_kernel(be_ref, na_ref, tok_hbm, h_hbm, w1g_ref, w1l_ref, b1g_ref, b1l_ref, w2_ref, b2_ref,
                o_ref, tok_s, rows_s, xb_s, acc_s, tok_sem, row_sem):
    b = pl.program_id(0)
    f = pl.program_id(1)
    n_blocks = pl.num_programs(0)
    nf = D_FF // MOE_TF
    rows_per_step = MOE_BLK // nf
    n_active = na_ref[0]
    active = b < n_active
    nxt = jnp.minimum(b + 1, n_blocks - 1)
    nxt_slot = (b + 1) % 2

    def tok_copy(blk, sl):
        return pltpu.make_async_copy(tok_hbm.at[blk], tok_s.at[sl], tok_sem.at[sl])

    def row_copy(tok, r):
        return pltpu.make_async_copy(h_hbm.at[pl.ds(tok, 1)], rows_s.at[pl.ds(r, 1)], row_sem)

    def wait_rows():
        for r in range(MOE_BLK):
            row_copy(0, r).wait()

    @pl.when(active & (f == 0))
    def _():
        @pl.when(b == 0)
        def _():
            first = tok_copy(0, 0)
            first.start()
            first.wait()

            def body(r, c):
                row_copy(tok_s[0, r], r).start()
                return c
            lax.fori_loop(0, MOE_BLK, body, 0, unroll=8)
            tok_copy(nxt, 1).start()

        wait_rows()
        xb_s[...] = rows_s[...].astype(BF16)
        tok_copy(nxt, nxt_slot).wait()

    @pl.when(active)
    def _():
        base = f * rows_per_step
        for i in range(rows_per_step):
            row_copy(tok_s[nxt_slot, base + i], base + i).start()
        xb = xb_s[...]
        glu = jnp.dot(xb, w1g_ref[0], preferred_element_type=F32) + b1g_ref[0]
        lin = jnp.dot(xb, w1l_ref[0], preferred_element_type=F32) + b1l_ref[0]
        glu = jnp.minimum(glu, SWIGLU_LIMIT)
        lin = jnp.clip(lin, -SWIGLU_LIMIT, SWIGLU_LIMIT)
        act = glu * jax.nn.sigmoid(SWIGLU_ALPHA * glu) * (lin + 1.0)
        part = jnp.dot(act.astype(BF16), w2_ref[0], preferred_element_type=F32)

        @pl.when(f == 0)
        def _():
            acc_s[...] = part + b2_ref[0]

        @pl.when((f > 0) & (f < nf - 1))
        def _():
            acc_s[...] += part

        @pl.when(f == nf - 1)
        def _():
            o_ref[...] = acc_s[...] + part
            tok_copy(jnp.minimum(b + 2, n_blocks - 1), b % 2).start()

    @pl.when(((b == n_active) & (f == 0)) | (active & (b == n_blocks - 1) & (f == nf - 1)))
    def _():
        wait_rows()
        tok_copy(0, jnp.where(active, b, b + 1) % 2).wait()

    @pl.when((b >= n_active) & (f == nf - 1))
    def _():
        o_ref[...] = jnp.zeros_like(o_ref)


def _ffn(h_all, slot_tok, block_expert, n_active, w1, b1, w2, b2):
    n_blocks = slot_tok.shape[0]
    nf = D_FF // MOE_TF

    def widx(col_off):
        def index_map(b, f, be, na):
            live = b < na[0]
            bb = jnp.where(live, b, na[0] - 1)
            ff = jnp.where(live, f, nf - 1)
            return be[bb], 0, col_off + ff
        return index_map

    def w2idx(b, f, be, na):
        live = b < na[0]
        return be[jnp.where(live, b, na[0] - 1)], jnp.where(live, f, nf - 1), 0

    def b2idx(b, f, be, na):
        return be[jnp.where(b < na[0], b, na[0] - 1)], 0, 0

    grid_spec = pltpu.PrefetchScalarGridSpec(
        num_scalar_prefetch=2,
        grid=(n_blocks, nf),
        in_specs=[pl.BlockSpec(memory_space=pl.ANY), pl.BlockSpec(memory_space=pl.ANY),
                  pl.BlockSpec((1, D_MODEL, MOE_TF), widx(0)), pl.BlockSpec((1, D_MODEL, MOE_TF), widx(nf)),
                  pl.BlockSpec((1, 1, MOE_TF), widx(0)), pl.BlockSpec((1, 1, MOE_TF), widx(nf)),
                  pl.BlockSpec((1, MOE_TF, D_MODEL), w2idx), pl.BlockSpec((1, 1, D_MODEL), b2idx)],
        out_specs=pl.BlockSpec((MOE_BLK, D_MODEL), lambda b, f, be, na: (b, 0)),
        scratch_shapes=[pltpu.SMEM((2, MOE_BLK), I32),
                        pltpu.VMEM((MOE_BLK, D_MODEL), F32),
                        pltpu.VMEM((MOE_BLK, D_MODEL), BF16),
                        pltpu.VMEM((MOE_BLK, D_MODEL), F32),
                        pltpu.SemaphoreType.DMA((2,)), pltpu.SemaphoreType.DMA])
    return pl.pallas_call(
        _ffn_kernel,
        out_shape=jax.ShapeDtypeStruct((n_blocks * MOE_BLK, D_MODEL), F32),
        grid_spec=grid_spec,
        compiler_params=_cparams(("arbitrary", "arbitrary")),
        name="expert_ffn",
    )(block_expert, n_active, slot_tok, h_all, w1, w1, b1[:, None, :], b1[:, None, :], w2, b2[:, None, :])


def _route(top_idx, rank, counts):
    n_tok = top_idx.shape[0]
    n_assign = n_tok * TOP_K
    e_flat = top_idx.reshape(-1)
    rank = rank.reshape(-1)
    onehot = (e_flat[:, None] == jnp.arange(N_EXPERTS)[None, :]).astype(I32)
    nblk = (counts + MOE_BLK - 1) // MOE_BLK
    blk_end = jnp.cumsum(nblk)
    pad_start = (blk_end - nblk) * MOE_BLK
    dest = jnp.sum(onehot * pad_start[None, :], axis=1) + rank
    n_blocks = -(-n_assign // MOE_BLK) + N_EXPERTS
    slot_tok = jnp.zeros((n_blocks * MOE_BLK,), I32).at[dest].set(
        jnp.arange(n_assign, dtype=I32) // TOP_K, unique_indices=True)
    block_expert = jnp.minimum(jnp.sum(blk_end[None, :] <= jnp.arange(n_blocks)[:, None], axis=1),
                               N_EXPERTS - 1).astype(I32)
    n_active = blk_end[-1:].astype(I32)
    return dest.reshape(n_tok, TOP_K).astype(I32), slot_tok.reshape(n_blocks, MOE_BLK), block_expert, n_active


def _combine_kernel(dest_hbm, y_hbm, x1_ref, tg_ref, gate_ref, gf_ref, o_ref, dest_s, rows_s, dest_sem, row_sem,
                    *, tc):
    i = pl.program_id(0)
    n = pl.num_programs(0)
    n_rows = TOP_K * tc
    slot = i % 2
    nxt = jnp.minimum(i + 1, n - 1)

    def dest_copy(tile, sl):
        return pltpu.make_async_copy(dest_hbm.at[tile], dest_s.at[sl], dest_sem.at[sl])

    def row_copy(src, r, sl):
        return pltpu.make_async_copy(y_hbm.at[pl.ds(src, 1)], rows_s.at[sl, pl.ds(r, 1)], row_sem.at[sl])

    def wait_rows(sl):
        for r in range(n_rows):
            row_copy(0, r, sl).wait()

    @pl.when(i == 0)
    def _():
        first = dest_copy(0, 0)
        first.start()
        first.wait()

        def body(r, c):
            row_copy(dest_s[0, r], r, 0).start()
            return c
        lax.fori_loop(0, n_rows, body, 0, unroll=8)
        dest_copy(nxt, 1).start()

    def step(cur):
        dest_copy(nxt, 1 - cur).wait()
        wait_rows(cur)
        for r in range(n_rows):
            row_copy(dest_s[1 - cur, r], r, 1 - cur).start(priority=r % 2)
        tg = tg_ref[...]
        ffn = tg[:, 0:1] * rows_s[cur, 0:tc, :]
        for k in range(1, TOP_K):
            ffn = ffn + tg[:, k:k + 1] * rows_s[cur, k * tc:(k + 1) * tc, :]
        x2 = x1_ref[...] + gate_ref[0] * ffn
        o_ref[...] = x2 * lax.rsqrt(jnp.mean(x2 * x2, axis=-1, keepdims=True) + EPS) * gf_ref[...]
        dest_copy(jnp.minimum(i + 2, n - 1), cur).start()

        @pl.when(i == n - 1)
        def _():
            wait_rows(1 - cur)
            dest_copy(0, cur).wait()

    for cur in range(2):
        pl.when(slot == cur)(functools.partial(step, cur))


def _combine(dest, y_sorted, x1, top_gate, gate2, g_final, seq_len):
    m = x1.shape[0]
    tc = _row_tile(256, m, seq_len)
    dest_t = dest.reshape(m // tc, tc, TOP_K).transpose(0, 2, 1).reshape(m // tc, TOP_K * tc)
    row = lambda width: pl.BlockSpec((tc, width), lambda i: (i, 0))
    return pl.pallas_call(
        functools.partial(_combine_kernel, tc=tc),
        out_shape=jax.ShapeDtypeStruct((m, D_MODEL), F32),
        grid=(m // tc,),
        in_specs=[pl.BlockSpec(memory_space=pl.ANY), pl.BlockSpec(memory_space=pl.ANY),
                  row(D_MODEL), row(LANES), _mod_spec(tc, seq_len), pl.BlockSpec((1, D_MODEL), lambda i: (0, 0))],
        out_specs=row(D_MODEL),
        scratch_shapes=[pltpu.SMEM((2, TOP_K * tc), I32), pltpu.VMEM((2, TOP_K * tc, D_MODEL), F32),
                        pltpu.SemaphoreType.DMA((2,)), pltpu.SemaphoreType.DMA((2,))],
        compiler_params=_cparams(("arbitrary",)),
        name="moe_combine",
    )(dest_t, y_sorted, x1, top_gate, _mod_arr(gate2, seq_len), g_final.reshape(1, D_MODEL))


def _reorder_w_in(w_in):
    w = w_in.astype(BF16)
    dt_cols = jnp.pad(w[:, XBC_END:DT_END], ((0, 0), (0, DT_PAD - SSM_HEADS)))
    return jnp.concatenate([w[:, :XBC_END], w[:, DT_END:], dt_cols], axis=1)


def _mixer_tail(x, y_ssd, y_att, proj, mod, lw, seq_len, counts0, tail=None):
    pre = _merge(y_ssd, y_att, proj, lw['w_br_ssd'], lw['w_br_att'])
    return _outproj(pre, x, mod[2], mod[3], mod[4], lw['g_norm2'], lw['w_out'], lw['w_router'], lw['b_router'],
                    seq_len, counts0, tail)


def _forward(x_prompt, x_sample, state_ssm, state_conv, cache_win_k, cache_win_v, c_prompt, c_sample, lw, g_final):
    bp, lp, d = x_prompt.shape
    bs = x_sample.shape[0]
    mp = bp * lp
    xp = x_prompt.reshape(mp, d)
    xs = x_sample.reshape(bs, d)

    mod = _ada(jnp.concatenate([c_prompt, c_sample], axis=0), lw['w_ada'], lw['b_ada'])
    mod_p = [mod[:bp, i * d:(i + 1) * d] for i in range(N_MOD)]
    mod_s = [mod[bp:, i * d:(i + 1) * d] for i in range(N_MOD)]

    proj_p = _inproj(xp, mod_p[0], mod_p[1], lw['g_norm1'], lw['w_in'], lp)
    proj_s = _inproj(xs, mod_s[0], mod_s[1], lw['g_norm1'], lw['w_in'], 1)

    yssd_p, ssm_p, conv_p, w_e_out = _ssd_prompt(proj_p, bp, lp, lw, lw['w_expert_out'])
    yatt_p, wk_p, wv_p, w_e_in = _attn_prompt(proj_p, bp, lp, lw['sinks'], lw['w_expert_in'])
    yssd_s, ssm_s, conv_s = _ssd_sample(proj_s, state_ssm, state_conv, lw)
    yatt_s, wk_s, wv_s = _attn_sample(proj_s, cache_win_k, cache_win_v, lw['sinks'])

    x1_s, h2_s, tg_s, ti_s, tr_s, cnt_s = _mixer_tail(xs, yssd_s, yatt_s, proj_s, mod_s, lw, 1,
                                                      jnp.zeros((1, LANES), F32))
    x1_p, h_all, tg_p, ti_p, tr_p, cnt_all = _mixer_tail(xp, yssd_p, yatt_p, proj_p, mod_p, lw, lp, cnt_s, h2_s)
    top_idx = jnp.concatenate([ti_p[:, :TOP_K], ti_s[:, :TOP_K]], axis=0)
    rank = jnp.concatenate([tr_p[:, :TOP_K], tr_s[:, :TOP_K]], axis=0)
    dest, slot_tok, block_expert, n_active = _route(top_idx, rank, cnt_all[0, :N_EXPERTS].astype(I32))
    y_sorted = _ffn(h_all, slot_tok, block_expert, n_active,
                    w_e_in, lw['b_expert_in'], w_e_out, lw['b_expert_out'])

    y_p = _combine(dest[:mp], y_sorted, x1_p, tg_p, mod_p[5], g_final, lp)
    y_s = _combine(dest[mp:], y_sorted, x1_s, tg_s, mod_s[5], g_final, 1)
    return (y_p.reshape(bp, lp, d), y_s.reshape(bs, 1, d),
            ssm_p[None], conv_p[None], wk_p[None], wv_p[None],
            ssm_s[None], conv_s[None], wk_s[None], wv_s[None])


def kernel(x_prompt, x_sample, state_ssm, state_conv, cache_win_k, cache_win_v, c_prompt, c_sample, w_ada, b_ada, g_norm1, w_in, w_conv, b_conv, dt_bias, a_log, d_skip, g_ssm_norm, sinks, w_br_ssd, w_br_att, w_out, g_norm2, w_router, b_router, w_expert_in, b_expert_in, w_expert_out, b_expert_out, g_final):
    assert w_ada.shape[0] == 1, "single-layer stack"
    lw = dict(w_ada=w_ada[0], b_ada=b_ada[0], g_norm1=g_norm1[0], w_in=_reorder_w_in(w_in[0]),
              w_conv=w_conv[0], b_conv=b_conv[0], dt_bias=dt_bias[0], a_log=a_log[0], d_skip=d_skip[0],
              g_ssm_norm=g_ssm_norm[0], sinks=sinks[0],
              w_br_ssd=w_br_ssd[0].astype(BF16), w_br_att=w_br_att[0].astype(BF16), w_out=w_out[0].astype(BF16),
              g_norm2=g_norm2[0], w_router=w_router[0], b_router=b_router[0],
              w_expert_in=w_expert_in[0], b_expert_in=b_expert_in[0],
              w_expert_out=w_expert_out[0], b_expert_out=b_expert_out[0])
    return _forward(x_prompt, x_sample, state_ssm[0], state_conv[0], cache_win_k[0], cache_win_v[0],
                    c_prompt, c_sample, lw, g_final)
```
